```python
import jax
import jax.numpy as jnp
from jax import lax
import numpy as np

D_MODEL = 1024
BATCH = 8
SEQ = 2048
DEPTH = 4

CTX_LEN = 256
GRID_W = 64
RMS_EPS = 1e-6
GLA_HEADS = 4
GLA_DK = 64
GLA_DV = 128
GLA_GATE_RANK = 16
GLA_GATE_NORM = 16.0
GLA_CHUNK = 64
SG_GROUPS = 4
SG_DIM = 128
SG_CHUNK = 128
FT_GROUPS = 4
FT_DIM = 64
MLA_HEADS = 6
MLA_NOPE = 128
MLA_ROPE = 64
MLA_V = 128
MLA_Q_RANK = 384
MLA_KV_RANK = 256
ROPE_BASE = 10000.0
ATTN_BLOCK = 128
MIX_W = GLA_HEADS * GLA_DV + SG_GROUPS * SG_DIM
FFN_HIDDEN = -(-8 * D_MODEL // (3 * 256)) * 256
EV_SIZES = (GLA_HEADS * GLA_DK, GLA_HEADS * GLA_DK, GLA_HEADS * GLA_DV, GLA_HEADS * GLA_DV,
            GLA_GATE_RANK, GLA_GATE_RANK, SG_GROUPS * SG_DIM, SG_GROUPS * SG_DIM)
EV_IN_W = 2 * GLA_HEADS * GLA_DK + 2 * GLA_HEADS * GLA_DV + 2 * GLA_GATE_RANK + 2 * SG_GROUPS * SG_DIM
OD_SIZES = (FT_GROUPS * FT_DIM, MLA_Q_RANK, MLA_KV_RANK, MLA_ROPE)
OD_IN_W = FT_GROUPS * FT_DIM + MLA_Q_RANK + MLA_KV_RANK + MLA_ROPE
OD_KV_COL0 = FT_GROUPS * FT_DIM + MLA_Q_RANK

kernel_name = "hybrid_gla_gmlp_fnet_mla_flow_block"


def rms_norm(x, g):
    xf = x.astype(jnp.float32)
    y = xf * lax.rsqrt(jnp.mean(xf * xf, axis=-1, keepdims=True) + RMS_EPS)
    return (y * g.astype(jnp.float32)).astype(x.dtype)


def split_cols(p, sizes):
    out, start = [], 0
    for s in sizes:
        out.append(p[..., start:start + s])
        start += s
    return out


def split_mod(m):
    return [t[:, None, :] for t in jnp.split(m, 6, axis=-1)]


def modulate(h, shift, scale):
    return h * (1.0 + scale) + shift


def swiglu(h, w_in, w_out):
    gu = h @ w_in
    return (jax.nn.silu(gu[..., :FFN_HIDDEN]) * gu[..., FFN_HIDDEN:]) @ w_out


def axial_rope(n_tok):
    rows = n_tok // GRID_W
    row_id = jnp.repeat(jnp.arange(rows, dtype=jnp.float32), GRID_W)
    col_id = jnp.tile(jnp.arange(GRID_W, dtype=jnp.float32), rows)
    axis_dim = MLA_ROPE // 2
    inv_freq = ROPE_BASE ** (-jnp.arange(0, axis_dim, 2, dtype=jnp.float32) / axis_dim)
    ang_r = row_id[:, None] * inv_freq
    ang_c = col_id[:, None] * inv_freq
    ang = jnp.concatenate([ang_r, ang_r, ang_c, ang_c], axis=-1)
    return jnp.cos(ang), jnp.sin(ang)


def apply_axial_rope(t, cos, sin):
    quarter = MLA_ROPE // 4
    blocks = t.reshape(*t.shape[:-1], 2, 2, quarter)
    rot = jnp.stack([-blocks[..., 1, :], blocks[..., 0, :]], axis=-2).reshape(t.shape)
    return t * cos[:, None, :] + rot * sin[:, None, :]


def rope_tail(t, rope):
    cos, sin = rope
    cos, sin = cos.astype(t.dtype), sin.astype(t.dtype)
    return jnp.concatenate([t[..., :MLA_NOPE], apply_axial_rope(t[..., MLA_NOPE:], cos, sin)], axis=-1)


def gla_chunk_scan(q, k, v, g, s0):
    B_, L, H, _ = q.shape
    dv = v.shape[-1]
    n = L // GLA_CHUNK

    def to_chunks(t):
        return t.reshape(B_, n, GLA_CHUNK, H, t.shape[-1]).transpose(1, 0, 3, 2, 4)

    causal = jnp.tril(jnp.ones((GLA_CHUNK, GLA_CHUNK), dtype=bool))

    def step(S, inp):
        qi, ki, vi, gi = inp
        b = jnp.cumsum(gi, axis=2)
        o_inter = jnp.einsum('bhtk,bhkv->bhtv', qi * jnp.exp(b), S)
        diff = b[:, :, :, None, :] - b[:, :, None, :, :]
        decay = jnp.exp(jnp.where(causal[:, :, None], diff, -jnp.inf))
        a = jnp.einsum('bhtk,bhsk,bhtsk->bhts', qi, ki, decay)
        o = o_inter + jnp.einsum('bhts,bhsv->bhtv', a, vi)
        b_last = b[:, :, -1:, :]
        S_new = jnp.exp(b_last[:, :, 0, :])[..., None] * S + jnp.einsum('bhsk,bhsv->bhkv', ki * jnp.exp(b_last - b), vi)
        return S_new, o

    S_fin, oc = lax.scan(step, s0, (to_chunks(q), to_chunks(k), to_chunks(v), to_chunks(g)))
    return oc.transpose(1, 0, 3, 2, 4).reshape(B_, L, H, dv), S_fin


def gla_prepare(q, k, v, a_f, a_b, wa_f, ba_f, wa_b, ba_b):
    B_, L, _ = q.shape
    f32 = jnp.float32

    def hd(t, d):
        return t.astype(f32).reshape(B_, L, GLA_HEADS, d)

    def log_decay(a, w, b):
        return hd(jax.nn.log_sigmoid((a @ w + b).astype(f32)) / GLA_GATE_NORM, GLA_DK)

    return (hd(q, GLA_DK) * GLA_DK ** -0.5, hd(k, GLA_DK), hd(v, GLA_DV),
            log_decay(a_f, wa_f, ba_f), log_decay(a_b, wa_b, ba_b))


def gla_bidirectional(con, lat):
    qc, kc, vc, gfc, gbc = con
    qx, kx, vx, gfx, gbx = lat
    s0 = jnp.zeros((qc.shape[0], GLA_HEADS, GLA_DK, GLA_DV), jnp.float32)
    rev = lambda t: t[:, ::-1]
    oc_f, sc_f = gla_chunk_scan(qc, kc, vc, gfc, s0)
    ox_f, _ = gla_chunk_scan(qx, kx, vx, gfx, sc_f)
    oc_b, sc_b = gla_chunk_scan(rev(qc), rev(kc), rev(vc), rev(gbc), s0)
    ox_b, _ = gla_chunk_scan(rev(qx), rev(kx), rev(vx), rev(gbx), sc_b)
    return oc_f + rev(oc_b), ox_f + rev(ox_b)


def gla_output(o, g_out, onorm_g):
    B_, L = o.shape[:2]
    on = rms_norm(o, onorm_g).reshape(B_, L, GLA_HEADS * GLA_DV).astype(g_out.dtype)
    return on * jax.nn.silu(g_out)


def spatial_gating(u, v, vnorm_g, ws, bs):
    B_, L, _ = v.shape
    n = L // SG_CHUNK
    u = jax.nn.gelu(u, approximate=False)
    vn = rms_norm(jax.nn.gelu(v, approximate=False).reshape(B_, L, SG_GROUPS, SG_DIM), vnorm_g)
    vc = vn.reshape(B_, n, SG_CHUNK, SG_GROUPS, SG_DIM)
    mixed = jnp.einsum('gts,bnsgc->bntgc', ws, vc) + bs.T[:, :, None]
    return u * mixed.reshape(B_, L, SG_GROUPS * SG_DIM)


def fourier_mix(h):
    B_, L, _ = h.shape
    hg = h.reshape(B_, L, FT_GROUPS, FT_DIM).astype(jnp.float32)
    out = jnp.fft.fft2(hg, axes=(1, 3), norm='ortho').real
    return out.astype(h.dtype).reshape(B_, L, FT_GROUPS * FT_DIM)


def mla_queries(qa, qa_g, wuq, qn_g, rope):
    B_, L, _ = qa.shape
    q = (rms_norm(qa, qa_g) @ wuq).reshape(B_, L, MLA_HEADS, MLA_NOPE + MLA_ROPE)
    q = rms_norm(q, qn_g)
    return q if rope is None else rope_tail(q, rope)


def mla_keys_values(kva, kpe, kva_g, wukv, kn_g, rope):
    B_, L, _ = kva.shape
    kv = (rms_norm(kva, kva_g) @ wukv).reshape(B_, L, MLA_HEADS, MLA_NOPE + MLA_V)
    k_pe = jnp.broadcast_to(kpe[:, :, None, :], (B_, L, MLA_HEADS, MLA_ROPE))
    k = rms_norm(jnp.concatenate([kv[..., :MLA_NOPE], k_pe], axis=-1), kn_g)
    k = k if rope is None else rope_tail(k, rope)
    return k, kv[..., MLA_NOPE:]


def block_attention(q, k, v):
    B_, Lq, H, dq = q.shape
    dv = v.shape[-1]
    nb = Lq // ATTN_BLOCK
    qb = q.reshape(B_, nb, ATTN_BLOCK, H, dq).transpose(1, 0, 2, 3, 4)
    scale = dq ** -0.5

    def one_block(qi):
        s = jnp.einsum('bqhd,bkhd->bhqk', qi, k, preferred_element_type=jnp.float32) * scale
        p = jax.nn.softmax(s, axis=-1).astype(v.dtype)
        return jnp.einsum('bhqk,bkhd->bqhd', p, v)

    ob = lax.map(one_block, qb)
    return ob.transpose(1, 0, 2, 3, 4).reshape(B_, Lq, H * dv)


def even_mixer(zx, zc, w_in, wa_f, ba_f, wa_b, ba_b, onorm_g, vnorm_g, ws, bs, need_ctx):
    qx, kx, vx, gx, afx, abx, ux, svx = split_cols(zx @ w_in, EV_SIZES)
    qc, kc, vc, gc, afc, abc, uc, svc = split_cols(zc @ w_in, EV_SIZES)
    lat = gla_prepare(qx, kx, vx, afx, abx, wa_f, ba_f, wa_b, ba_b)
    con = gla_prepare(qc, kc, vc, afc, abc, wa_f, ba_f, wa_b, ba_b)
    o_c, o_x = gla_bidirectional(con, lat)
    mx = jnp.concatenate([gla_output(o_x, gx, onorm_g), spatial_gating(ux, svx, vnorm_g, ws, bs)], axis=-1)
    mc = None
    if need_ctx:
        mc = jnp.concatenate([gla_output(o_c, gc, onorm_g), spatial_gating(uc, svc, vnorm_g, ws, bs)], axis=-1)
    return mx, mc


def odd_mixer(zx, zc, w_in, qa_g, wuq, kva_g, wukv, qn_g, kn_g, rope, need_ctx):
    ftx, qax, kvax, kpex = split_cols(zx @ w_in, OD_SIZES)
    qx = mla_queries(qax, qa_g, wuq, qn_g, rope)
    kx, vx = mla_keys_values(kvax, kpex, kva_g, wukv, kn_g, rope)
    if need_ctx:
        ftc, qac, kvac, kpec = split_cols(zc @ w_in, OD_SIZES)
    else:
        kvac, kpec = split_cols(zc @ w_in[:, OD_KV_COL0:], OD_SIZES[2:])
    kc, vc = mla_keys_values(kvac, kpec, kva_g, wukv, kn_g, None)
    att_x = block_attention(qx, jnp.concatenate([kc, kx], axis=1), jnp.concatenate([vc, vx], axis=1))
    mx = jnp.concatenate([fourier_mix(ftx), att_x], axis=-1)
    mc = None
    if need_ctx:
        qc = mla_queries(qac, qa_g, wuq, qn_g, None)
        mc = jnp.concatenate([fourier_mix(ftc), block_attention(qc, kc, vc)], axis=-1)
    return mx, mc


def setup_inputs(seed: int = 0) -> dict:
    key = jax.random.key(seed)
    keys = jax.random.split(key, 32)
    counter = iter(range(32))
    f32 = jnp.float32

    def nrm(shape, s):
        return jax.random.normal(keys[next(counter)], shape, f32) * s

    def gain(shape):
        return 1.0 + nrm(shape, 0.02)

    D = D_MODEL
    NE = (DEPTH + 1) // 2
    NO = DEPTH // 2
    return {
        "x": nrm((BATCH, SEQ, D), 1.0),
        "c": nrm((BATCH, D), 1.0),
        "ctx": nrm((BATCH, CTX_LEN, D), 1.0),
        "c_ctx": nrm((D,), 1.0),
        "ada_w": nrm((DEPTH, D, 6 * D), 0.5 * D ** -0.5),
        "ada_b": nrm((DEPTH, 6 * D), 0.02),
        "norm_mix_g": gain((DEPTH, D)),
        "norm_ffn_g": gain((DEPTH, D)),
        "w_mix_out": nrm((DEPTH, MIX_W, D), MIX_W ** -0.5),
        "ffn_w_in": nrm((DEPTH, D, 2 * FFN_HIDDEN), D ** -0.5),
        "ffn_w_out": nrm((DEPTH, FFN_HIDDEN, D), FFN_HIDDEN ** -0.5),
        "ev_w_in": nrm((NE, D, EV_IN_W), D ** -0.5),
        "gla_wa_f": nrm((NE, GLA_GATE_RANK, GLA_HEADS * GLA_DK), GLA_GATE_RANK ** -0.5),
        "gla_ba_f": nrm((NE, GLA_HEADS * GLA_DK), 0.1),
        "gla_wa_b": nrm((NE, GLA_GATE_RANK, GLA_HEADS * GLA_DK), GLA_GATE_RANK ** -0.5),
        "gla_ba_b": nrm((NE, GLA_HEADS * GLA_DK), 0.1),
        "gla_onorm_g": gain((NE, GLA_DV)),
        "sg_vnorm_g": gain((NE, SG_GROUPS, SG_DIM)),
        "sg_ws": nrm((NE, SG_GROUPS, SG_CHUNK, SG_CHUNK), SG_CHUNK ** -0.5),
        "sg_bs": gain((NE, SG_GROUPS, SG_CHUNK)),
        "od_w_in": nrm((NO, D, OD_IN_W), D ** -0.5),
        "mla_qa_g": gain((NO, MLA_Q_RANK)),
        "mla_wuq": nrm((NO, MLA_Q_RANK, MLA_HEADS * (MLA_NOPE + MLA_ROPE)), MLA_Q_RANK ** -0.5),
        "mla_kva_g": gain((NO, MLA_KV_RANK)),
        "mla_wukv": nrm((NO, MLA_KV_RANK, MLA_HEADS * (MLA_NOPE + MLA_V)), MLA_KV_RANK ** -0.5),
        "mla_qn_g": gain((NO, MLA_NOPE + MLA_ROPE)),
        "mla_kn_g": gain((NO, MLA_NOPE + MLA_ROPE)),
    }


def reference(x, c, ctx, c_ctx, ada_w, ada_b, norm_mix_g, norm_ffn_g, w_mix_out, ffn_w_in, ffn_w_out,
              ev_w_in, gla_wa_f, gla_ba_f, gla_wa_b, gla_ba_b, gla_onorm_g, sg_vnorm_g, sg_ws, sg_bs,
              od_w_in, mla_qa_g, mla_wuq, mla_kva_g, mla_wukv, mla_qn_g, mla_kn_g):
    rope = axial_rope(x.shape[1])
    silu_c = jax.nn.silu(c)
    silu_cc = jax.nn.silu(c_ctx)[None, :]
    h = ctx
    for l in range(DEPTH):
        need_ctx = l < DEPTH - 1
        sm_x, cm_x, gm_x, sf_x, cf_x, gf_x = split_mod(silu_c @ ada_w[l] + ada_b[l])
        sm_c, cm_c, gm_c, sf_c, cf_c, gf_c = split_mod(silu_cc @ ada_w[l] + ada_b[l])
        zx = modulate(rms_norm(x, norm_mix_g[l]), sm_x, cm_x)
        zc = modulate(rms_norm(h, norm_mix_g[l]), sm_c, cm_c)
        i = l // 2
        if l % 2 == 0:
            mx, mc = even_mixer(zx, zc, ev_w_in[i], gla_wa_f[i], gla_ba_f[i], gla_wa_b[i], gla_ba_b[i],
                                gla_onorm_g[i], sg_vnorm_g[i], sg_ws[i], sg_bs[i], need_ctx)
        else:
            mx, mc = odd_mixer(zx, zc, od_w_in[i], mla_qa_g[i], mla_wuq[i], mla_kva_g[i], mla_wukv[i],
                               mla_qn_g[i], mla_kn_g[i], rope, need_ctx)
        x = x + gm_x * (mx @ w_mix_out[l])
        x = x + gf_x * swiglu(modulate(rms_norm(x, norm_ffn_g[l]), sf_x, cf_x), ffn_w_in[l], ffn_w_out[l])
        if need_ctx:
            h = h + gm_c * (mc @ w_mix_out[l])
            h = h + gf_c * swiglu(modulate(rms_norm(h, norm_ffn_g[l]), sf_c, cf_c), ffn_w_in[l], ffn_w_out[l])
    return x
```

```python
import functools

import numpy as np
import jax
import jax.numpy as jnp
from jax import lax
from jax.experimental import pallas as pl
from jax.experimental.pallas import tpu as pltpu

D_MODEL = 1024
BATCH = 8
SEQ = 2048
DEPTH = 4
CTX_LEN = 256
GRID_W = 64
RMS_EPS = 1e-6
GLA_HEADS = 4
GLA_DK = 64
GLA_DV = 128
GLA_GATE_RANK = 16
GLA_GATE_NORM = 16.0
GLA_CHUNK = 64
SG_GROUPS = 4
SG_DIM = 128
SG_CHUNK = 128
FT_GROUPS = 4
FT_DIM = 64
MLA_HEADS = 6
MLA_NOPE = 128
MLA_ROPE = 64
MLA_V = 128
MLA_Q_RANK = 384
MLA_KV_RANK = 256
ROPE_BASE = 10000.0
FFN_HIDDEN = 2816

TOK = CTX_LEN + SEQ
NTOK = BATCH * TOK
ROW_TILE = 256
TILES_PER_BATCH = TOK // ROW_TILE
CTX_TILES = CTX_LEN // ROW_TILE
MOD_ROWS = 16
CTX_MOD_ROW = BATCH
LANE = 128
EV_W = 2688
OD_W = 1024
MLA_QK = MLA_NOPE + MLA_ROPE
VMEM_LIMIT = 56 * 1024 * 1024

F32 = jnp.float32
BF16 = jnp.bfloat16


def _dot(a, b):
    return jnp.dot(a, b, preferred_element_type=F32)


def _dot_nt(a, b):
    return lax.dot_general(a, b, (((1,), (1,)), ((), ())), preferred_element_type=F32)


def _dot_tn(a, b):
    return lax.dot_general(a, b, (((0,), (0,)), ((), ())), preferred_element_type=F32)


def _rms(x, g):
    return x * lax.rsqrt(jnp.mean(x * x, axis=-1, keepdims=True) + RMS_EPS) * g


def _params(sem):
    return pltpu.CompilerParams(dimension_semantics=sem, vmem_limit_bytes=VMEM_LIMIT)


def _resident(shape):
    zeros = (0,) * len(shape)
    return pl.BlockSpec(shape, lambda *_: zeros, pipeline_mode=pl.Buffered(1))


def _mod_row(i):
    return jnp.where(i % TILES_PER_BATCH < CTX_TILES, CTX_MOD_ROW, i // TILES_PER_BATCH)


def _mod_spec(layer, piece, tile_of_step):
    def index_map(i):
        return ((layer * MOD_ROWS + _mod_row(tile_of_step(i))) * 6 + piece, 0, 0)
    return pl.BlockSpec((1, 1, D_MODEL), index_map)


def _ada_kernel(c_ref, w_ref, b_ref, o_ref):
    c = c_ref[...]
    s = c * jax.nn.sigmoid(c)
    o_ref[0] = jnp.dot(s, w_ref[0], precision=lax.Precision.HIGHEST, preferred_element_type=F32) + b_ref[0]


def _ada_table(cvec, ada_w, ada_b):
    tn = 1536
    n = 6 * D_MODEL
    out = pl.pallas_call(
        _ada_kernel,
        out_shape=jax.ShapeDtypeStruct((DEPTH, MOD_ROWS, n), F32),
        grid=(DEPTH, n // tn),
        in_specs=[pl.BlockSpec((MOD_ROWS, D_MODEL), lambda l, j: (0, 0)),
                  pl.BlockSpec((1, D_MODEL, tn), lambda l, j: (l, 0, j)),
                  pl.BlockSpec((1, 1, tn), lambda l, j: (l, 0, j))],
        out_specs=pl.BlockSpec((1, MOD_ROWS, tn), lambda l, j: (l, 0, j)),
        compiler_params=_params(("arbitrary", "arbitrary")),
        name="ada_table",
    )(cvec, ada_w, ada_b.reshape(DEPTH, 1, n))
    return out.reshape(DEPTH * MOD_ROWS * 6, 1, D_MODEL)


def _inproj_kernel(x_ref, g_ref, shift_ref, scale_ref, w_ref, o_ref):
    z = _rms(x_ref[...], g_ref[...]) * (1.0 + scale_ref[0]) + shift_ref[0]
    o_ref[...] = _dot(z.astype(BF16), w_ref[...]).astype(o_ref.dtype)


def _inproj(xa, mod, layer, norm_g, w):
    width = w.shape[1]
    ident = lambda i: i
    return pl.pallas_call(
        _inproj_kernel,
        out_shape=jax.ShapeDtypeStruct((NTOK, width), BF16),
        grid=(NTOK // ROW_TILE,),
        in_specs=[pl.BlockSpec((ROW_TILE, D_MODEL), lambda i: (i, 0)),
                  _resident((1, D_MODEL)),
                  _mod_spec(layer, 0, ident),
                  _mod_spec(layer, 1, ident),
                  _resident((D_MODEL, width))],
        out_specs=pl.BlockSpec((ROW_TILE, width), lambda i: (i, 0)),
        compiler_params=_params(("parallel",)),
        name="inproj",
    )(xa, norm_g.reshape(1, D_MODEL), mod, mod, w)


def _ffn_kernel(x_ref, ma_ref, mb_ref, gm_ref, sf_ref, cf_ref, gf_ref, g_ref,
                wmix_ref, win_ref, wout_ref, o_ref):
    m = jnp.concatenate([ma_ref[...], mb_ref[...]], axis=1)
    x1 = x_ref[...] + gm_ref[0] * _dot(m, wmix_ref[...])
    h = (_rms(x1, g_ref[...]) * (1.0 + cf_ref[0]) + sf_ref[0]).astype(BF16)
    gu = _dot(h, win_ref[...])
    gate = gu[:, :FFN_HIDDEN]
    act = (gate * jax.nn.sigmoid(gate) * gu[:, FFN_HIDDEN:]).astype(BF16)
    o_ref[...] = x1 + gf_ref[0] * _dot(act, wout_ref[...])


def _ffn(xa, ma, mb, mod, layer, norm_g, wmix, win, wout, latent_only):
    if latent_only:
        x_tiles = SEQ // ROW_TILE
        n_steps = BATCH * x_tiles
        tile = lambda i: (i // x_tiles) * TILES_PER_BATCH + CTX_TILES + i % x_tiles
        out_rows = BATCH * SEQ
    else:
        n_steps = NTOK // ROW_TILE
        tile = lambda i: i
        out_rows = NTOK
    row_spec = lambda width: pl.BlockSpec((ROW_TILE, width), lambda i: (tile(i), 0))
    return pl.pallas_call(
        _ffn_kernel,
        out_shape=jax.ShapeDtypeStruct((out_rows, D_MODEL), F32),
        grid=(n_steps,),
        in_specs=[row_spec(D_MODEL), row_spec(ma.shape[1]), row_spec(mb.shape[1]),
                  _mod_spec(layer, 2, tile), _mod_spec(layer, 3, tile),
                  _mod_spec(layer, 4, tile), _mod_spec(layer, 5, tile),
                  _resident((1, D_MODEL)),
                  _resident((D_MODEL, D_MODEL)),
                  _resident((D_MODEL, 2 * FFN_HIDDEN)),
                  _resident((FFN_HIDDEN, D_MODEL))],
        out_specs=pl.BlockSpec((ROW_TILE, D_MODEL), lambda i: (i, 0)),
        compiler_params=_params(("parallel",)),
        name="ffn",
    )(xa, ma, mb, mod, mod, mod, mod, norm_g.reshape(1, D_MODEL), wmix, win, wout)


GLA_QK_W = GLA_HEADS * GLA_DK
GLA_V_W = GLA_HEADS * GLA_DV
GLA_MAIN_W = 2 * GLA_QK_W + 2 * GLA_V_W
GLA_NCHUNK = TOK // GLA_CHUNK


GLA_BLOCK = 256
GLA_NBLOCK = TOK // GLA_BLOCK
GLA_CTX_BLOCKS = CTX_LEN // GLA_BLOCK


def _log_sigmoid(y):
    return jnp.minimum(y, 0.0) - jnp.log(1.0 + jnp.exp(-jnp.abs(y)))


def _dot_exact(a, b):
    return jnp.dot(a, b, precision=lax.Precision.HIGHEST, preferred_element_type=F32)


def _gla_kernel(main_ref, gate_ref, waf_ref, baf_ref, wab_ref, bab_ref, og_ref, o_ref,
                of_scr, qd_scr, kd_scr, qe_scr, kl_scr, dec_scr, s_scr):
    C, BLK = GLA_CHUNK, GLA_BLOCK
    cpb = BLK // C
    rb = lax.broadcasted_iota(jnp.int32, (BLK, BLK), 0)
    cb = lax.broadcasted_iota(jnp.int32, (BLK, BLK), 1)
    same_chunk = (rb // C) == (cb // C)
    tri_f = jnp.where(same_chunk & (rb >= cb), 1.0, 0.0).astype(BF16)
    tri_b = jnp.where(same_chunk & (rb <= cb), 1.0, 0.0).astype(BF16)
    r = lax.broadcasted_iota(jnp.int32, (C, C), 0)
    cc = lax.broadcasted_iota(jnp.int32, (C, C), 1)
    lower, upper = r >= cc, r <= cc
    sr = lax.broadcasted_iota(jnp.int32, (BLK, cpb * LANE), 0)
    sc = lax.broadcasted_iota(jnp.int32, (BLK, cpb * LANE), 1)
    chunk_sum = jnp.where(sr // C == sc // LANE, 1.0, 0.0).astype(BF16)

    def split(x):
        hi = x.astype(BF16)
        return hi, (x - hi.astype(F32)).astype(BF16)

    def dot_01(m, x):
        hi, lo = split(x)
        return _dot(jnp.concatenate([m, m], axis=1), jnp.concatenate([hi, lo], axis=0))

    def dot_10(x, m):
        hi, lo = split(x)
        return _dot(jnp.concatenate([hi, lo], axis=1), jnp.concatenate([m, m], axis=0))

    def block_rows(blk):
        return pl.ds(pl.multiple_of(blk * BLK, BLK), BLK)

    def chunk_rows(blk, j):
        return pl.ds(pl.multiple_of(blk * BLK + j * C, C), C)

    def dec_rows(blk, j):
        return pl.ds(pl.multiple_of((blk * cpb + j) * GLA_QK_W, GLA_QK_W), GLA_QK_W)

    def decay_terms(a, q, k, w_ref, bias_ref, tri, end_row):
        g = _log_sigmoid(_dot(a, w_ref[...]) + bias_ref[...]) * (1.0 / GLA_GATE_NORM)
        b = dot_01(tri, g)
        dec_all = jnp.exp(dot_10(g.T, chunk_sum))
        terms = []
        for j in range(cpb):
            sl = slice(j * C, (j + 1) * C)
            bc = b[sl]
            b_mid = bc[C // 2:C // 2 + 1]
            b_end = bc[end_row:end_row + 1]
            d = bc - b_mid
            qd = q[sl] * jnp.exp(d)
            kd = k[sl] * jnp.exp(-d)
            qe = qd * jnp.exp(b_mid)
            kl = kd * jnp.exp(b_end - b_mid)
            dec = dec_all[:, j * LANE:(j + 1) * LANE]
            terms.append((qd.astype(BF16), kd.T.astype(BF16), qe.astype(BF16), kl.T.astype(BF16), dec))
        return terms

    def step(qd, kd_t, qe, kl_t, dec, v, keep):
        outs = []
        for h in range(GLA_HEADS):
            ks = slice(h * GLA_DK, (h + 1) * GLA_DK)
            vh = v[:, h * GLA_DV:(h + 1) * GLA_DV]
            s_h = s_scr[ks, :]
            att = jnp.where(keep, _dot(qd[:, ks], kd_t[ks, :]), 0.0).astype(BF16)
            outs.append(_dot(att, vh) + _dot(qe[:, ks], s_h.astype(BF16)))
            s_scr[ks, :] = s_h * dec[ks, :] + _dot(kl_t[ks, :], vh)
        return outs

    s_scr[...] = jnp.zeros(s_scr.shape, F32)

    def loop1(blk, carry):
        rows = block_rows(blk)
        a = gate_ref[rows, :]
        q = main_ref[rows, 0:GLA_QK_W].astype(F32) * (GLA_DK ** -0.5)
        k = main_ref[rows, GLA_QK_W:2 * GLA_QK_W].astype(F32)
        v = main_ref[rows, 2 * GLA_QK_W:2 * GLA_QK_W + GLA_V_W]
        fwd = decay_terms(a, q, k, waf_ref, baf_ref, tri_f, C - 1)
        bwd = decay_terms(a, q, k, wab_ref, bab_ref, tri_b, 0)
        for j in range(cpb):
            cr = chunk_rows(blk, j)
            qd, kd, qe, kl, dec = bwd[j]
            qd_scr[cr, :] = qd
            kd_scr[dec_rows(blk, j), :] = kd
            qe_scr[cr, :] = qe
            kl_scr[dec_rows(blk, j), :] = kl
            dec_scr[dec_rows(blk, j), :] = dec
            qd, kd, qe, kl, dec = fwd[j]
            outs = step(qd, kd, qe, kl, dec, v[j * C:(j + 1) * C], lower)
            of_scr[cr, :] = jnp.concatenate(outs, axis=1)
        return carry

    lax.fori_loop(0, GLA_NBLOCK, loop1, 0)

    s_scr[...] = jnp.zeros(s_scr.shape, F32)

    def loop2(i, carry):
        blk = jnp.where(i < GLA_CTX_BLOCKS, GLA_CTX_BLOCKS - 1 - i, GLA_NBLOCK - 1 - (i - GLA_CTX_BLOCKS))
        for j in reversed(range(cpb)):
            cr = chunk_rows(blk, j)
            v = main_ref[cr, 2 * GLA_QK_W:2 * GLA_QK_W + GLA_V_W]
            kr = dec_rows(blk, j)
            outs = step(qd_scr[cr, :], kd_scr[kr, :], qe_scr[cr, :], kl_scr[kr, :], dec_scr[kr, :], v, upper)
            gout = main_ref[cr, 2 * GLA_QK_W + GLA_V_W:GLA_MAIN_W].astype(F32)
            res = []
            for h in range(GLA_HEADS):
                vs = slice(h * GLA_DV, (h + 1) * GLA_DV)
                o = outs[h] + of_scr[cr, vs]
                gh = gout[:, vs]
                res.append(_rms(o, og_ref[...]) * (gh * jax.nn.sigmoid(gh)))
            o_ref[cr, :] = jnp.concatenate(res, axis=1).astype(o_ref.dtype)
        return carry

    lax.fori_loop(0, GLA_NBLOCK, loop2, 0)


def _gla(p, waf, baf, wab, bab, onorm_g):
    gate_col = (EV_W - LANE) // LANE
    qk_scr = pltpu.VMEM((TOK, GLA_QK_W), BF16)
    qk_t_scr = pltpu.VMEM((GLA_NCHUNK * GLA_QK_W, GLA_CHUNK), BF16)
    return pl.pallas_call(
        _gla_kernel,
        out_shape=jax.ShapeDtypeStruct((NTOK, GLA_V_W), BF16),
        grid=(BATCH,),
        in_specs=[pl.BlockSpec((TOK, GLA_MAIN_W), lambda b: (b, 0)),
                  pl.BlockSpec((TOK, LANE), lambda b: (b, gate_col)),
                  _resident((LANE, GLA_QK_W)), _resident((1, GLA_QK_W)),
                  _resident((LANE, GLA_QK_W)), _resident((1, GLA_QK_W)),
                  _resident((1, GLA_DV))],
        out_specs=pl.BlockSpec((TOK, GLA_V_W), lambda b: (b, 0)),
        scratch_shapes=[pltpu.VMEM((TOK, GLA_V_W), F32), qk_scr, qk_t_scr, qk_scr, qk_t_scr,
                        pltpu.VMEM((GLA_NCHUNK * GLA_QK_W, LANE), F32),
                        pltpu.VMEM((GLA_QK_W, GLA_DV), F32)],
        compiler_params=_params(("parallel",)),
        name="gla",
    )(p, p, waf, baf, wab, bab, onorm_g.reshape(1, GLA_DV))


def _gelu(x):
    return 0.5 * x * (1.0 + lax.erf(x * (2.0 ** -0.5)))


def _sg_kernel(u_ref, v_ref, vg_ref, ws_ref, bs_ref, o_ref):
    for c in range(ROW_TILE // SG_CHUNK):
        rows = slice(c * SG_CHUNK, (c + 1) * SG_CHUNK)
        for g in range(SG_GROUPS):
            cols = slice(g * SG_DIM, (g + 1) * SG_DIM)
            u = _gelu(u_ref[rows, cols].astype(F32))
            vn = _rms(_gelu(v_ref[rows, cols].astype(F32)), vg_ref[g:g + 1, :])
            mixed = _dot(ws_ref[g], vn.astype(BF16)) + bs_ref[:, g:g + 1]
            o_ref[rows, cols] = (u * mixed).astype(o_ref.dtype)


def _spatial_gating(p, vnorm_g, ws, bs_t):
    width = SG_GROUPS * SG_DIM
    u_col = GLA_MAIN_W // width
    return pl.pallas_call(
        _sg_kernel,
        out_shape=jax.ShapeDtypeStruct((NTOK, width), BF16),
        grid=(NTOK // ROW_TILE,),
        in_specs=[pl.BlockSpec((ROW_TILE, width), lambda i: (i, u_col)),
                  pl.BlockSpec((ROW_TILE, width), lambda i: (i, u_col + 1)),
                  _resident((SG_GROUPS, SG_DIM)),
                  _resident((SG_GROUPS, SG_CHUNK, SG_CHUNK)),
                  _resident((SG_CHUNK, SG_GROUPS))],
        out_specs=pl.BlockSpec((ROW_TILE, width), lambda i: (i, 0)),
        compiler_params=_params(("parallel",)),
        name="spatial_gating",
    )(p, p, vnorm_g, ws, bs_t)


FT_W = FT_GROUPS * FT_DIM


def _dft_cos_sin(n):
    jk = np.outer(np.arange(n), np.arange(n)) % n
    ang = 2.0 * np.pi * jk.astype(np.float64) / n
    return np.cos(ang), np.sin(ang)


def _fourier_kernel(h_ref, cs_ref, cl_ref, sl_ref, clc_ref, slc_ref, o_ref, *, need_ctx):
    ab = _dot(h_ref[...], cs_ref[...]).astype(BF16)
    a, b = ab[:, :FT_W], ab[:, FT_W:]
    scale_x = (SEQ * FT_DIM) ** -0.5
    out_x = (_dot(cl_ref[...], a[CTX_LEN:]) - _dot(sl_ref[...], b[CTX_LEN:])) * scale_x
    o_ref[CTX_LEN:, :] = out_x.astype(o_ref.dtype)
    if need_ctx:
        scale_c = (CTX_LEN * FT_DIM) ** -0.5
        out_c = (_dot(clc_ref[...], a[:CTX_LEN]) - _dot(slc_ref[...], b[:CTX_LEN])) * scale_c
        o_ref[:CTX_LEN, :] = out_c.astype(o_ref.dtype)
    else:
        o_ref[:CTX_LEN, :] = jnp.zeros((CTX_LEN, FT_W), o_ref.dtype)


def _fourier(p, need_ctx):
    cc, sc = _dft_cos_sin(FT_DIM)
    eye = np.eye(FT_GROUPS)
    cs = jnp.asarray(np.concatenate([np.kron(eye, cc), np.kron(eye, sc)], axis=1), F32).astype(BF16)
    cl, sl = (jnp.asarray(m, F32).astype(BF16) for m in _dft_cos_sin(SEQ))
    clc, slc = (jnp.asarray(m, F32).astype(BF16) for m in _dft_cos_sin(CTX_LEN))
    return pl.pallas_call(
        functools.partial(_fourier_kernel, need_ctx=need_ctx),
        out_shape=jax.ShapeDtypeStruct((NTOK, FT_W), BF16),
        grid=(BATCH,),
        in_specs=[pl.BlockSpec((TOK, FT_W), lambda b: (b, 0)),
                  _resident((FT_W, 2 * FT_W)),
                  _resident((SEQ, SEQ)), _resident((SEQ, SEQ)),
                  _resident((CTX_LEN, CTX_LEN)), _resident((CTX_LEN, CTX_LEN))],
        out_specs=pl.BlockSpec((TOK, FT_W), lambda b: (b, 0)),
        compiler_params=_params(("parallel",)),
        name="fourier",
    )(p, cs, cl, sl, clc, slc)


MLA_NOPE_W = MLA_HEADS * MLA_NOPE
MLA_ROPE_W = MLA_HEADS * MLA_ROPE
OD_QA0 = FT_W
OD_KVA0 = OD_QA0 + MLA_Q_RANK
OD_KPE0 = OD_KVA0 + MLA_KV_RANK
OD_KPE_ROT0 = OD_KPE0 + MLA_ROPE


def _rot_cols(w):
    q = MLA_ROPE // 4
    return jnp.concatenate([-w[..., q:2 * q], w[..., 0:q], -w[..., 3 * q:4 * q], w[..., 2 * q:3 * q]], axis=-1)


def _rot_gain(g):
    q = MLA_ROPE // 4
    return jnp.concatenate([g[..., q:2 * q], g[..., 0:q], g[..., 3 * q:4 * q], g[..., 2 * q:3 * q]], axis=-1)


def _rope_tables():
    rows = SEQ // GRID_W
    row_id = np.repeat(np.arange(rows, dtype=np.float32), GRID_W)
    col_id = np.tile(np.arange(GRID_W, dtype=np.float32), rows)
    axis_dim = MLA_ROPE // 2
    inv_freq = (np.float32(ROPE_BASE) ** (-np.arange(0, axis_dim, 2, dtype=np.float32) / np.float32(axis_dim))).astype(np.float32)
    ang_r = row_id[:, None] * inv_freq
    ang_c = col_id[:, None] * inv_freq
    ang = np.concatenate([ang_r, ang_r, ang_c, ang_c], axis=-1).astype(np.float32)
    cos = np.concatenate([np.ones((CTX_LEN, MLA_ROPE), np.float32), np.cos(ang)], axis=0)
    sin = np.concatenate([np.zeros((CTX_LEN, MLA_ROPE), np.float32), np.sin(ang)], axis=0)
    return jnp.asarray(cos, F32), jnp.asarray(sin, F32)


def _mla_prep_kernel(p_ref, cos_ref, sin_ref, qag_ref, wuq_ref, kvag_ref, wukv_ref,
                     qng_ref, kng_ref, q_ref, k_ref, v_ref):
    cos, sin = cos_ref[...], sin_ref[...]
    qng, kng = qng_ref[...], kng_ref[...]
    g_nope = slice(0, MLA_NOPE)
    g_rope = slice(MLA_NOPE, MLA_QK)
    g_rot = slice(MLA_QK, MLA_QK + MLA_ROPE)

    qa = _rms(p_ref[:, OD_QA0:OD_KVA0].astype(F32), qag_ref[...]).astype(BF16)
    qf = _dot(qa, wuq_ref[...])
    kva = _rms(p_ref[:, OD_KVA0:OD_KPE0].astype(F32), kvag_ref[...]).astype(BF16)
    kv = _dot(kva, wukv_ref[...])
    kpe = p_ref[:, OD_KPE0:OD_KPE_ROT0].astype(F32)
    kpe_rot = p_ref[:, OD_KPE_ROT0:OD_W].astype(F32)
    kpe_ss = jnp.sum(kpe * kpe, axis=-1, keepdims=True)

    for h in range(MLA_HEADS):
        nope = slice(h * MLA_NOPE, (h + 1) * MLA_NOPE)
        rope = slice(MLA_NOPE_W + h * MLA_ROPE, MLA_NOPE_W + (h + 1) * MLA_ROPE)
        rot = slice(MLA_NOPE_W + MLA_ROPE_W + h * MLA_ROPE, MLA_NOPE_W + MLA_ROPE_W + (h + 1) * MLA_ROPE)
        qn, qr, qrr = qf[:, nope], qf[:, rope], qf[:, rot]
        ss = jnp.sum(qn * qn, axis=-1, keepdims=True) + jnp.sum(qr * qr, axis=-1, keepdims=True)
        inv = lax.rsqrt(ss * (1.0 / MLA_QK) + RMS_EPS) * (MLA_QK ** -0.5)
        q_rope = (qr * qng[:, g_rope]) * cos + (qrr * qng[:, g_rot]) * sin
        q_ref[h] = (jnp.concatenate([qn * qng[:, g_nope], q_rope], axis=1) * inv).astype(q_ref.dtype)

        kn = kv[:, nope]
        ss = jnp.sum(kn * kn, axis=-1, keepdims=True) + kpe_ss
        inv = lax.rsqrt(ss * (1.0 / MLA_QK) + RMS_EPS)
        k_rope = (kpe * kng[:, g_rope]) * cos + (kpe_rot * kng[:, g_rot]) * sin
        k_ref[h] = jnp.concatenate([(kn * kng[:, g_nope] * inv).T, (k_rope * inv).T], axis=0).astype(k_ref.dtype)
        v_ref[h] = kv[:, MLA_NOPE_W + h * MLA_V:MLA_NOPE_W + (h + 1) * MLA_V].astype(v_ref.dtype)


def _mla_prep(p, cos, sin, qa_g, wuq, kva_g, wukv, qn_g, kn_g):
    head_spec = lambda d: pl.BlockSpec((MLA_HEADS, ROW_TILE, d), lambda i: (0, i, 0))
    rope_spec = pl.BlockSpec((ROW_TILE, MLA_ROPE), lambda i: (i % TILES_PER_BATCH, 0))
    return pl.pallas_call(
        _mla_prep_kernel,
        out_shape=(jax.ShapeDtypeStruct((MLA_HEADS, NTOK, MLA_QK), BF16),
                   jax.ShapeDtypeStruct((MLA_HEADS, MLA_QK, NTOK), BF16),
                   jax.ShapeDtypeStruct((MLA_HEADS, NTOK, MLA_V), BF16)),
        grid=(NTOK // ROW_TILE,),
        in_specs=[pl.BlockSpec((ROW_TILE, OD_W), lambda i: (i, 0)),
                  rope_spec, rope_spec,
                  _resident((1, MLA_Q_RANK)), _resident(wuq.shape),
                  _resident((1, MLA_KV_RANK)), _resident(wukv.shape),
                  _resident((1, MLA_QK + MLA_ROPE)), _resident((1, MLA_QK + MLA_ROPE))],
        out_specs=(head_spec(MLA_QK),
                   pl.BlockSpec((MLA_HEADS, MLA_QK, ROW_TILE), lambda i: (0, 0, i)),
                   head_spec(MLA_V)),
        compiler_params=_params(("parallel",)),
        name="mla_prep",
    )(p, cos, sin, qa_g, wuq, kva_g, wukv, qn_g, kn_g)


ATT_Q_TILE = 512


def _softmax_av(q, k_t, v):
    s = _dot(q, k_t)
    p = jnp.exp(s - jnp.max(s, axis=-1, keepdims=True))
    return _dot(p.astype(BF16), v) / jnp.sum(p, axis=-1, keepdims=True)


def _attn_kernel(q_ref, kt_ref, v_ref, o_ref, *, need_ctx):
    k_t, v = kt_ref[0], v_ref[0]
    for j in range(SEQ // ATT_Q_TILE):
        rows = slice(CTX_LEN + j * ATT_Q_TILE, CTX_LEN + (j + 1) * ATT_Q_TILE)
        o_ref[rows, :] = _softmax_av(q_ref[0, rows, :], k_t, v).astype(o_ref.dtype)
    if need_ctx:
        o_ref[:CTX_LEN, :] = _softmax_av(q_ref[0, :CTX_LEN, :], k_t[:, :CTX_LEN], v[:CTX_LEN]).astype(o_ref.dtype)
    else:
        o_ref[:CTX_LEN, :] = jnp.zeros((CTX_LEN, MLA_V), o_ref.dtype)


def _attention(q, k_t, v, need_ctx):
    head_spec = lambda d: pl.BlockSpec((1, TOK, d), lambda b, h: (h, b, 0))
    return pl.pallas_call(
        functools.partial(_attn_kernel, need_ctx=need_ctx),
        out_shape=jax.ShapeDtypeStruct((NTOK, MLA_HEADS * MLA_V), BF16),
        grid=(BATCH, MLA_HEADS),
        in_specs=[head_spec(MLA_QK),
                  pl.BlockSpec((1, MLA_QK, TOK), lambda b, h: (h, 0, b)),
                  head_spec(MLA_V)],
        out_specs=pl.BlockSpec((TOK, MLA_V), lambda b, h: (b, h)),
        compiler_params=_params(("parallel", "parallel")),
        name="attention",
    )(q, k_t, v)


def _even_w_in(w):
    gates0 = GLA_MAIN_W
    gates1 = gates0 + 2 * GLA_GATE_RANK
    pad = jnp.zeros((D_MODEL, EV_W - w.shape[1]), w.dtype)
    return jnp.concatenate([w[:, :gates0], w[:, gates1:], w[:, gates0:gates1], pad], axis=1).astype(BF16)


def _gate_w(wa, first_row):
    out = jnp.zeros((LANE, GLA_QK_W), wa.dtype)
    return lax.dynamic_update_slice(out, wa, (first_row, 0)).astype(BF16)


def _odd_w_in(w):
    return jnp.concatenate([w, _rot_cols(w[:, OD_KPE0:OD_KPE_ROT0])], axis=1).astype(BF16)


def _wuq_cols(w):
    w = w.reshape(MLA_Q_RANK, MLA_HEADS, MLA_QK)
    nope = w[:, :, :MLA_NOPE].reshape(MLA_Q_RANK, MLA_NOPE_W)
    rope = w[:, :, MLA_NOPE:]
    return jnp.concatenate([nope, rope.reshape(MLA_Q_RANK, MLA_ROPE_W),
                            _rot_cols(rope).reshape(MLA_Q_RANK, MLA_ROPE_W)], axis=1).astype(BF16)


def _wukv_cols(w):
    w = w.reshape(MLA_KV_RANK, MLA_HEADS, MLA_NOPE + MLA_V)
    return jnp.concatenate([w[:, :, :MLA_NOPE].reshape(MLA_KV_RANK, MLA_NOPE_W),
                            w[:, :, MLA_NOPE:].reshape(MLA_KV_RANK, MLA_HEADS * MLA_V)], axis=1).astype(BF16)


def _head_gain(g):
    return jnp.concatenate([g, _rot_gain(g[MLA_NOPE:])]).reshape(1, MLA_QK + MLA_ROPE)


def kernel(x, c, ctx, c_ctx, ada_w, ada_b, norm_mix_g, norm_ffn_g, w_mix_out, ffn_w_in, ffn_w_out, ev_w_in, gla_wa_f, gla_ba_f, gla_wa_b, gla_ba_b, gla_onorm_g, sg_vnorm_g, sg_ws, sg_bs, od_w_in, mla_qa_g, mla_wuq, mla_kva_g, mla_wukv, mla_qn_g, mla_kn_g):
    cvec = jnp.zeros((MOD_ROWS, D_MODEL), F32).at[:BATCH].set(c).at[CTX_MOD_ROW].set(c_ctx)
    mod = _ada_table(cvec, ada_w, ada_b)
    xa = jnp.concatenate([ctx, x], axis=1).reshape(NTOK, D_MODEL)
    cos, sin = _rope_tables()

    for l in range(DEPTH):
        last = l == DEPTH - 1
        i = l // 2
        if l % 2 == 0:
            p = _inproj(xa, mod, l, norm_mix_g[l], _even_w_in(ev_w_in[i]))
            ma = _gla(p, _gate_w(gla_wa_f[i], 0), gla_ba_f[i].reshape(1, GLA_QK_W),
                      _gate_w(gla_wa_b[i], GLA_GATE_RANK), gla_ba_b[i].reshape(1, GLA_QK_W), gla_onorm_g[i])
            mb = _spatial_gating(p, sg_vnorm_g[i], sg_ws[i].astype(BF16), sg_bs[i].T)
        else:
            p = _inproj(xa, mod, l, norm_mix_g[l], _odd_w_in(od_w_in[i]))
            q, k, v = _mla_prep(p, cos, sin, mla_qa_g[i].reshape(1, MLA_Q_RANK), _wuq_cols(mla_wuq[i]),
                                mla_kva_g[i].reshape(1, MLA_KV_RANK), _wukv_cols(mla_wukv[i]),
                                _head_gain(mla_qn_g[i]), _head_gain(mla_kn_g[i]))
            ma = _fourier(p, not last)
            mb = _attention(q, k, v, not last)
        xa = _ffn(xa, ma, mb, mod, l, norm_ffn_g[l], w_mix_out[l].astype(BF16),
                  ffn_w_in[l].astype(BF16), ffn_w_out[l].astype(BF16), last)
    return xa.reshape(BATCH, SEQ, D_MODEL)
```

```python
import functools

import numpy as np
import jax
import jax.numpy as jnp
from jax import lax
from jax.experimental import pallas as pl
from jax.experimental.pallas import tpu as pltpu

D_MODEL = 1024
BATCH = 8
SEQ = 2048
DEPTH = 4
CTX_LEN = 256
GRID_W = 64
RMS_EPS = 1e-6
GLA_HEADS = 4
GLA_DK = 64
GLA_DV = 128
GLA_GATE_RANK = 16
GLA_GATE_NORM = 16.0
GLA_CHUNK = 64
SG_GROUPS = 4
SG_DIM = 128
SG_CHUNK = 128
FT_GROUPS = 4
FT_DIM = 64
MLA_HEADS = 6
MLA_NOPE = 128
MLA_ROPE = 64
MLA_V = 128
MLA_Q_RANK = 384
MLA_KV_RANK = 256
ROPE_BASE = 10000.0
FFN_HIDDEN = 2816

TOK = CTX_LEN + SEQ
NTOK = BATCH * TOK
ROW_TILE = 256
TILES_PER_BATCH = TOK // ROW_TILE
CTX_TILES = CTX_LEN // ROW_TILE
MOD_ROWS = 16
CTX_MOD_ROW = BATCH
LANE = 128
EV_W = 2688
OD_W = 1024
MLA_QK = MLA_NOPE + MLA_ROPE
VMEM_LIMIT = 56 * 1024 * 1024

F32 = jnp.float32
BF16 = jnp.bfloat16


def _dot(a, b):
    return jnp.dot(a, b, preferred_element_type=F32)


def _dot_nt(a, b):
    return lax.dot_general(a, b, (((1,), (1,)), ((), ())), preferred_element_type=F32)


def _dot_tn(a, b):
    return lax.dot_general(a, b, (((0,), (0,)), ((), ())), preferred_element_type=F32)


def _rms(x, g):
    return x * lax.rsqrt(jnp.mean(x * x, axis=-1, keepdims=True) + RMS_EPS) * g


def _params(sem):
    return pltpu.CompilerParams(dimension_semantics=sem, vmem_limit_bytes=VMEM_LIMIT)


def _resident(shape):
    zeros = (0,) * len(shape)
    return pl.BlockSpec(shape, lambda *_: zeros, pipeline_mode=pl.Buffered(1))


def _mod_row(i):
    return jnp.where(i % TILES_PER_BATCH < CTX_TILES, CTX_MOD_ROW, i // TILES_PER_BATCH)


def _mod_spec(layer, piece, tile_of_step):
    def index_map(i):
        return ((layer * MOD_ROWS + _mod_row(tile_of_step(i))) * 6 + piece, 0, 0)
    return pl.BlockSpec((1, 1, D_MODEL), index_map)


def _ada_kernel(c_ref, w_ref, b_ref, o_ref):
    c = c_ref[...]
    s = c * jax.nn.sigmoid(c)
    o_ref[0] = jnp.dot(s, w_ref[0], precision=lax.Precision.HIGHEST, preferred_element_type=F32) + b_ref[0]


def _ada_table(cvec, ada_w, ada_b):
    tn = 1536
    n = 6 * D_MODEL
    out = pl.pallas_call(
        _ada_kernel,
        out_shape=jax.ShapeDtypeStruct((DEPTH, MOD_ROWS, n), F32),
        grid=(DEPTH, n // tn),
        in_specs=[pl.BlockSpec((MOD_ROWS, D_MODEL), lambda l, j: (0, 0)),
                  pl.BlockSpec((1, D_MODEL, tn), lambda l, j: (l, 0, j)),
                  pl.BlockSpec((1, 1, tn), lambda l, j: (l, 0, j))],
        out_specs=pl.BlockSpec((1, MOD_ROWS, tn), lambda l, j: (l, 0, j)),
        compiler_params=_params(("arbitrary", "arbitrary")),
        name="ada_table",
    )(cvec, ada_w, ada_b.reshape(DEPTH, 1, n))
    return out.reshape(DEPTH * MOD_ROWS * 6, 1, D_MODEL)


def _layer_block(shape, layer):
    zeros = (0,) * len(shape)
    return pl.BlockSpec((1,) + tuple(shape), lambda *_: (layer,) + zeros, pipeline_mode=pl.Buffered(1))


def _modulated_norm(x, g, shift, scale):
    return _rms(x, g) * (1.0 + scale) + shift


def _first_inproj_kernel(x_ref, ctx_ref, g_ref, shift_ref, scale_ref, w_ref, p_ref, xa_ref):
    is_ctx = pl.program_id(0) % TILES_PER_BATCH < CTX_TILES
    x = jnp.where(is_ctx, ctx_ref[...], x_ref[...])
    xa_ref[...] = x
    z = _modulated_norm(x, g_ref[0], shift_ref[0], scale_ref[0])
    p_ref[...] = _dot(z.astype(BF16), w_ref[0]).astype(p_ref.dtype)


def _first_inproj(x, ctx, mod, norm_g, w):
    width = w.shape[-1]
    x_tiles = SEQ // ROW_TILE
    x_tile = lambda i: ((i // TILES_PER_BATCH) * x_tiles + jnp.maximum(i % TILES_PER_BATCH - CTX_TILES, 0), 0)
    ctx_tile = lambda i: ((i // TILES_PER_BATCH) * CTX_TILES + jnp.minimum(i % TILES_PER_BATCH, CTX_TILES - 1), 0)
    ident = lambda i: i
    row_out = lambda wd: pl.BlockSpec((ROW_TILE, wd), lambda i: (i, 0))
    return pl.pallas_call(
        _first_inproj_kernel,
        out_shape=(jax.ShapeDtypeStruct((NTOK, width), BF16), jax.ShapeDtypeStruct((NTOK, D_MODEL), F32)),
        grid=(NTOK // ROW_TILE,),
        in_specs=[pl.BlockSpec((ROW_TILE, D_MODEL), x_tile),
                  pl.BlockSpec((ROW_TILE, D_MODEL), ctx_tile),
                  _layer_block((1, D_MODEL), 0),
                  _mod_spec(0, 0, ident), _mod_spec(0, 1, ident),
                  _layer_block((D_MODEL, width), 0)],
        out_specs=(row_out(width), row_out(D_MODEL)),
        compiler_params=_params(("parallel",)),
        name="first_inproj",
    )(x.reshape(BATCH * SEQ, D_MODEL), ctx.reshape(BATCH * CTX_LEN, D_MODEL), norm_g, mod, mod, w)


def _tail_kernel(x_ref, ma_ref, mb_ref, gm_ref, sf_ref, cf_ref, gf_ref, g_ref,
                 wmix_ref, win_ref, wout_ref, *rest, has_next):
    m = jnp.concatenate([ma_ref[...], mb_ref[...]], axis=1)
    x1 = x_ref[...] + gm_ref[0] * _dot(m, wmix_ref[0])
    h = _modulated_norm(x1, g_ref[0], sf_ref[0], cf_ref[0]).astype(BF16)
    gu = _dot(h, win_ref[0])
    gate = gu[:, :FFN_HIDDEN]
    act = (gate * jax.nn.sigmoid(gate) * gu[:, FFN_HIDDEN:]).astype(BF16)
    x2 = x1 + gf_ref[0] * _dot(act, wout_ref[0])
    if has_next:
        shift_ref, scale_ref, gn_ref, wn_ref, o_ref, p_ref = rest
        z = _modulated_norm(x2, gn_ref[0], shift_ref[0], scale_ref[0])
        p_ref[...] = _dot(z.astype(BF16), wn_ref[0]).astype(p_ref.dtype)
    else:
        (o_ref,) = rest
    o_ref[...] = x2


def _layer_tail(xa, ma, mb, mod, layer, norm_ffn_g, wmix, win, wout, nxt):
    if nxt is None:
        x_tiles = SEQ // ROW_TILE
        n_steps = BATCH * x_tiles
        tile = lambda i: (i // x_tiles) * TILES_PER_BATCH + CTX_TILES + i % x_tiles
        out_rows = BATCH * SEQ
    else:
        n_steps = NTOK // ROW_TILE
        tile = lambda i: i
        out_rows = NTOK
    row_in = lambda width: pl.BlockSpec((ROW_TILE, width), lambda i: (tile(i), 0))
    row_out = lambda width: pl.BlockSpec((ROW_TILE, width), lambda i: (i, 0))
    in_specs = [row_in(D_MODEL), row_in(ma.shape[1]), row_in(mb.shape[1]),
                _mod_spec(layer, 2, tile), _mod_spec(layer, 3, tile),
                _mod_spec(layer, 4, tile), _mod_spec(layer, 5, tile),
                _layer_block((1, D_MODEL), layer),
                _layer_block((D_MODEL, D_MODEL), layer),
                _layer_block((D_MODEL, 2 * FFN_HIDDEN), layer),
                _layer_block((FFN_HIDDEN, D_MODEL), layer)]
    args = [xa, ma, mb, mod, mod, mod, mod, norm_ffn_g, wmix, win, wout]
    out_shape = [jax.ShapeDtypeStruct((out_rows, D_MODEL), F32)]
    out_specs = [row_out(D_MODEL)]
    if nxt is not None:
        norm_mix_g, w_next, idx = nxt
        width = w_next.shape[-1]
        in_specs += [_mod_spec(layer + 1, 0, tile), _mod_spec(layer + 1, 1, tile),
                     _layer_block((1, D_MODEL), layer + 1), _layer_block((D_MODEL, width), idx)]
        args += [mod, mod, norm_mix_g, w_next]
        out_shape.append(jax.ShapeDtypeStruct((out_rows, width), BF16))
        out_specs.append(row_out(width))
    return pl.pallas_call(
        functools.partial(_tail_kernel, has_next=nxt is not None),
        out_shape=tuple(out_shape),
        grid=(n_steps,),
        in_specs=in_specs,
        out_specs=tuple(out_specs),
        compiler_params=_params(("parallel",)),
        name="layer_tail",
    )(*args)


GLA_QK_W = GLA_HEADS * GLA_DK
GLA_V_W = GLA_HEADS * GLA_DV
GLA_MAIN_W = 2 * GLA_QK_W + 2 * GLA_V_W
GLA_NCHUNK = TOK // GLA_CHUNK


GLA_BLOCK = 256
GLA_NBLOCK = TOK // GLA_BLOCK
GLA_CTX_BLOCKS = CTX_LEN // GLA_BLOCK


def _log_sigmoid(y):
    return jnp.minimum(y, 0.0) - jnp.log(1.0 + jnp.exp(-jnp.abs(y)))


def _dot_exact(a, b):
    return jnp.dot(a, b, precision=lax.Precision.HIGHEST, preferred_element_type=F32)


def _gla_kernel(main_ref, gate_ref, waf_ref, baf_ref, wab_ref, bab_ref, og_ref, o_ref,
                of_scr, qd_scr, kd_scr, qe_scr, kl_scr, dec_scr, s_scr):
    C, BLK = GLA_CHUNK, GLA_BLOCK
    cpb = BLK // C
    rb = lax.broadcasted_iota(jnp.int32, (BLK, BLK), 0)
    cb = lax.broadcasted_iota(jnp.int32, (BLK, BLK), 1)
    same_chunk = (rb // C) == (cb // C)
    tri_f = jnp.where(same_chunk & (rb >= cb), 1.0, 0.0).astype(BF16)
    tri_b = jnp.where(same_chunk & (rb <= cb), 1.0, 0.0).astype(BF16)
    r = lax.broadcasted_iota(jnp.int32, (C, C), 0)
    cc = lax.broadcasted_iota(jnp.int32, (C, C), 1)
    lower, upper = r >= cc, r <= cc
    sr = lax.broadcasted_iota(jnp.int32, (BLK, cpb * LANE), 0)
    sc = lax.broadcasted_iota(jnp.int32, (BLK, cpb * LANE), 1)
    chunk_sum = jnp.where(sr // C == sc // LANE, 1.0, 0.0).astype(BF16)

    def split(x):
        hi = x.astype(BF16)
        return hi, (x - hi.astype(F32)).astype(BF16)

    def dot_01(m, x):
        hi, lo = split(x)
        return _dot(jnp.concatenate([m, m], axis=1), jnp.concatenate([hi, lo], axis=0))

    def dot_10(x, m):
        hi, lo = split(x)
        return _dot(jnp.concatenate([hi, lo], axis=1), jnp.concatenate([m, m], axis=0))

    def block_rows(blk):
        return pl.ds(pl.multiple_of(blk * BLK, BLK), BLK)

    def chunk_rows(blk, j):
        return pl.ds(pl.multiple_of(blk * BLK + j * C, C), C)

    def dec_rows(blk, j):
        return pl.ds(pl.multiple_of((blk * cpb + j) * GLA_QK_W, GLA_QK_W), GLA_QK_W)

    def decay_terms(a, q, k, w_ref, bias_ref, tri, end_row):
        g = _log_sigmoid(_dot(a, w_ref[...]) + bias_ref[...]) * (1.0 / GLA_GATE_NORM)
        b = dot_01(tri, g)
        dec_all = jnp.exp(dot_10(g.T, chunk_sum))
        terms = []
        for j in range(cpb):
            sl = slice(j * C, (j + 1) * C)
            bc = b[sl]
            b_mid = bc[C // 2:C // 2 + 1]
            b_end = bc[end_row:end_row + 1]
            d = bc - b_mid
            qd = q[sl] * jnp.exp(d)
            kd = k[sl] * jnp.exp(-d)
            qe = qd * jnp.exp(b_mid)
            kl = kd * jnp.exp(b_end - b_mid)
            dec = dec_all[:, j * LANE:(j + 1) * LANE]
            terms.append((qd.astype(BF16), kd.T.astype(BF16), qe.astype(BF16), kl.T.astype(BF16), dec))
        return terms

    def step(qd, kd_t, qe, kl_t, dec, v, keep):
        outs = []
        for h in range(GLA_HEADS):
            ks = slice(h * GLA_DK, (h + 1) * GLA_DK)
            vh = v[:, h * GLA_DV:(h + 1) * GLA_DV]
            s_h = s_scr[ks, :]
            att = jnp.where(keep, _dot(qd[:, ks], kd_t[ks, :]), 0.0).astype(BF16)
            outs.append(_dot(att, vh) + _dot(qe[:, ks], s_h.astype(BF16)))
            s_scr[ks, :] = s_h * dec[ks, :] + _dot(kl_t[ks, :], vh)
        return outs

    s_scr[...] = jnp.zeros(s_scr.shape, F32)

    def loop1(blk, carry):
        rows = block_rows(blk)
        a = gate_ref[rows, :]
        q = main_ref[rows, 0:GLA_QK_W].astype(F32) * (GLA_DK ** -0.5)
        k = main_ref[rows, GLA_QK_W:2 * GLA_QK_W].astype(F32)
        v = main_ref[rows, 2 * GLA_QK_W:2 * GLA_QK_W + GLA_V_W]
        fwd = decay_terms(a, q, k, waf_ref, baf_ref, tri_f, C - 1)
        bwd = decay_terms(a, q, k, wab_ref, bab_ref, tri_b, 0)
        for j in range(cpb):
            cr = chunk_rows(blk, j)
            qd, kd, qe, kl, dec = bwd[j]
            qd_scr[cr, :] = qd
            kd_scr[dec_rows(blk, j), :] = kd
            qe_scr[cr, :] = qe
            kl_scr[dec_rows(blk, j), :] = kl
            dec_scr[dec_rows(blk, j), :] = dec
            qd, kd, qe, kl, dec = fwd[j]
            outs = step(qd, kd, qe, kl, dec, v[j * C:(j + 1) * C], lower)
            of_scr[cr, :] = jnp.concatenate(outs, axis=1)
        return carry

    lax.fori_loop(0, GLA_NBLOCK, loop1, 0)

    s_scr[...] = jnp.zeros(s_scr.shape, F32)

    def loop2(i, carry):
        blk = jnp.where(i < GLA_CTX_BLOCKS, GLA_CTX_BLOCKS - 1 - i, GLA_NBLOCK - 1 - (i - GLA_CTX_BLOCKS))
        for j in reversed(range(cpb)):
            cr = chunk_rows(blk, j)
            v = main_ref[cr, 2 * GLA_QK_W:2 * GLA_QK_W + GLA_V_W]
            kr = dec_rows(blk, j)
            outs = step(qd_scr[cr, :], kd_scr[kr, :], qe_scr[cr, :], kl_scr[kr, :], dec_scr[kr, :], v, upper)
            gout = main_ref[cr, 2 * GLA_QK_W + GLA_V_W:GLA_MAIN_W].astype(F32)
            res = []
            for h in range(GLA_HEADS):
                vs = slice(h * GLA_DV, (h + 1) * GLA_DV)
                o = outs[h] + of_scr[cr, vs]
                gh = gout[:, vs]
                res.append(_rms(o, og_ref[...]) * (gh * jax.nn.sigmoid(gh)))
            o_ref[cr, :] = jnp.concatenate(res, axis=1).astype(o_ref.dtype)
        return carry

    lax.fori_loop(0, GLA_NBLOCK, loop2, 0)


def _gla(p, waf, baf, wab, bab, onorm_g):
    gate_col = (EV_W - LANE) // LANE
    qk_scr = pltpu.VMEM((TOK, GLA_QK_W), BF16)
    qk_t_scr = pltpu.VMEM((GLA_NCHUNK * GLA_QK_W, GLA_CHUNK), BF16)
    return pl.pallas_call(
        _gla_kernel,
        out_shape=jax.ShapeDtypeStruct((NTOK, GLA_V_W), BF16),
        grid=(BATCH,),
        in_specs=[pl.BlockSpec((TOK, GLA_MAIN_W), lambda b: (b, 0)),
                  pl.BlockSpec((TOK, LANE), lambda b: (b, gate_col)),
                  _resident((LANE, GLA_QK_W)), _resident((1, GLA_QK_W)),
                  _resident((LANE, GLA_QK_W)), _resident((1, GLA_QK_W)),
                  _resident((1, GLA_DV))],
        out_specs=pl.BlockSpec((TOK, GLA_V_W), lambda b: (b, 0)),
        scratch_shapes=[pltpu.VMEM((TOK, GLA_V_W), F32), qk_scr, qk_t_scr, qk_scr, qk_t_scr,
                        pltpu.VMEM((GLA_NCHUNK * GLA_QK_W, LANE), F32),
                        pltpu.VMEM((GLA_QK_W, GLA_DV), F32)],
        compiler_params=_params(("parallel",)),
        name="gla",
    )(p, p, waf, baf, wab, bab, onorm_g.reshape(1, GLA_DV))


def _gelu(x):
    return 0.5 * x * (1.0 + lax.erf(x * (2.0 ** -0.5)))


def _sg_kernel(u_ref, v_ref, vg_ref, ws_ref, bs_ref, o_ref):
    for c in range(ROW_TILE // SG_CHUNK):
        rows = slice(c * SG_CHUNK, (c + 1) * SG_CHUNK)
        for g in range(SG_GROUPS):
            cols = slice(g * SG_DIM, (g + 1) * SG_DIM)
            u = _gelu(u_ref[rows, cols].astype(F32))
            vn = _rms(_gelu(v_ref[rows, cols].astype(F32)), vg_ref[g:g + 1, :])
            mixed = _dot(ws_ref[g], vn.astype(BF16)) + bs_ref[:, g:g + 1]
            o_ref[rows, cols] = (u * mixed).astype(o_ref.dtype)


def _spatial_gating(p, vnorm_g, ws, bs_t):
    width = SG_GROUPS * SG_DIM
    u_col = GLA_MAIN_W // width
    return pl.pallas_call(
        _sg_kernel,
        out_shape=jax.ShapeDtypeStruct((NTOK, width), BF16),
        grid=(NTOK // ROW_TILE,),
        in_specs=[pl.BlockSpec((ROW_TILE, width), lambda i: (i, u_col)),
                  pl.BlockSpec((ROW_TILE, width), lambda i: (i, u_col + 1)),
                  _resident((SG_GROUPS, SG_DIM)),
                  _resident((SG_GROUPS, SG_CHUNK, SG_CHUNK)),
                  _resident((SG_CHUNK, SG_GROUPS))],
        out_specs=pl.BlockSpec((ROW_TILE, width), lambda i: (i, 0)),
        compiler_params=_params(("parallel",)),
        name="spatial_gating",
    )(p, p, vnorm_g, ws, bs_t)


FT_W = FT_GROUPS * FT_DIM


def _dft_cos_sin(n):
    jk = np.outer(np.arange(n), np.arange(n)) % n
    ang = 2.0 * np.pi * jk.astype(np.float64) / n
    return np.cos(ang), np.sin(ang)


def _fourier_kernel(h_ref, cs_ref, cl_ref, sl_ref, clc_ref, slc_ref, o_ref, *, need_ctx):
    ab = _dot(h_ref[...], cs_ref[...]).astype(BF16)
    a, b = ab[:, :FT_W], ab[:, FT_W:]
    scale_x = (SEQ * FT_DIM) ** -0.5
    out_x = (_dot(cl_ref[...], a[CTX_LEN:]) - _dot(sl_ref[...], b[CTX_LEN:])) * scale_x
    o_ref[CTX_LEN:, :] = out_x.astype(o_ref.dtype)
    if need_ctx:
        scale_c = (CTX_LEN * FT_DIM) ** -0.5
        out_c = (_dot(clc_ref[...], a[:CTX_LEN]) - _dot(slc_ref[...], b[:CTX_LEN])) * scale_c
        o_ref[:CTX_LEN, :] = out_c.astype(o_ref.dtype)
    else:
        o_ref[:CTX_LEN, :] = jnp.zeros((CTX_LEN, FT_W), o_ref.dtype)


def _fourier(p, need_ctx):
    cc, sc = _dft_cos_sin(FT_DIM)
    eye = np.eye(FT_GROUPS)
    cs = jnp.asarray(np.concatenate([np.kron(eye, cc), np.kron(eye, sc)], axis=1), F32).astype(BF16)
    cl, sl = (jnp.asarray(m, F32).astype(BF16) for m in _dft_cos_sin(SEQ))
    clc, slc = (jnp.asarray(m, F32).astype(BF16) for m in _dft_cos_sin(CTX_LEN))
    return pl.pallas_call(
        functools.partial(_fourier_kernel, need_ctx=need_ctx),
        out_shape=jax.ShapeDtypeStruct((NTOK, FT_W), BF16),
        grid=(BATCH,),
        in_specs=[pl.BlockSpec((TOK, FT_W), lambda b: (b, 0)),
                  _resident((FT_W, 2 * FT_W)),
                  _resident((SEQ, SEQ)), _resident((SEQ, SEQ)),
                  _resident((CTX_LEN, CTX_LEN)), _resident((CTX_LEN, CTX_LEN))],
        out_specs=pl.BlockSpec((TOK, FT_W), lambda b: (b, 0)),
        compiler_params=_params(("parallel",)),
        name="fourier",
    )(p, cs, cl, sl, clc, slc)


MLA_NOPE_W = MLA_HEADS * MLA_NOPE
MLA_ROPE_W = MLA_HEADS * MLA_ROPE
OD_QA0 = FT_W
OD_KVA0 = OD_QA0 + MLA_Q_RANK
OD_KPE0 = OD_KVA0 + MLA_KV_RANK
OD_KPE_ROT0 = OD_KPE0 + MLA_ROPE


def _rot_cols(w):
    q = MLA_ROPE // 4
    return jnp.concatenate([-w[..., q:2 * q], w[..., 0:q], -w[..., 3 * q:4 * q], w[..., 2 * q:3 * q]], axis=-1)


def _rot_gain(g):
    q = MLA_ROPE // 4
    return jnp.concatenate([g[..., q:2 * q], g[..., 0:q], g[..., 3 * q:4 * q], g[..., 2 * q:3 * q]], axis=-1)


def _rope_tables():
    rows = SEQ // GRID_W
    row_id = np.repeat(np.arange(rows, dtype=np.float32), GRID_W)
    col_id = np.tile(np.arange(GRID_W, dtype=np.float32), rows)
    axis_dim = MLA_ROPE // 2
    inv_freq = (np.float32(ROPE_BASE) ** (-np.arange(0, axis_dim, 2, dtype=np.float32) / np.float32(axis_dim))).astype(np.float32)
    ang_r = row_id[:, None] * inv_freq
    ang_c = col_id[:, None] * inv_freq
    ang = np.concatenate([ang_r, ang_r, ang_c, ang_c], axis=-1).astype(np.float32)
    cos = np.concatenate([np.ones((CTX_LEN, MLA_ROPE), np.float32), np.cos(ang)], axis=0)
    sin = np.concatenate([np.zeros((CTX_LEN, MLA_ROPE), np.float32), np.sin(ang)], axis=0)
    return tuple(jnp.asarray(t, F32) for t in (np.tile(cos, (1, MLA_HEADS)), np.tile(sin, (1, MLA_HEADS)),
                                               cos.T, sin.T))


def _lane_rms(x, g):
    width = x.shape[1]
    ss = _dot((x * x).astype(BF16), jnp.ones((width, LANE), BF16))
    inv = lax.rsqrt(ss * (1.0 / width) + RMS_EPS)
    return x * jnp.concatenate([inv] * (width // LANE), axis=1) * g


def _mla_prep_kernel(p_ref, cos_ref, sin_ref, cos_t_ref, sin_t_ref, qag_ref, wuq_ref, kvag_ref,
                     wukv_kt_ref, wukv_v_ref, qgain_ref, head_sum_ref,
                     kg_nope_ref, kg_rope_ref, kg_rot_ref, q_ref, k_ref, v_ref):
    qa = _lane_rms(p_ref[:, OD_QA0:OD_KVA0].astype(F32), qag_ref[...])
    qf = _dot(qa.astype(BF16), wuq_ref[...])
    sq = qf[:, :MLA_NOPE_W + MLA_ROPE_W]
    ss = _dot((sq * sq).astype(BF16), head_sum_ref[...])
    inv = lax.rsqrt(ss * (1.0 / MLA_QK) + RMS_EPS) * (MLA_QK ** -0.5)
    y = qf * qgain_ref[...]
    q_rope = (y[:, MLA_NOPE_W:MLA_NOPE_W + MLA_ROPE_W] * cos_ref[...]
              + y[:, MLA_NOPE_W + MLA_ROPE_W:] * sin_ref[...])
    for h in range(MLA_HEADS):
        q_h = jnp.concatenate([y[:, h * MLA_NOPE:(h + 1) * MLA_NOPE],
                               q_rope[:, h * MLA_ROPE:(h + 1) * MLA_ROPE]], axis=1)
        q_ref[h] = (q_h * inv[:, h:h + 1]).astype(q_ref.dtype)

    kva = _lane_rms(p_ref[:, OD_KVA0:OD_KPE0].astype(F32), kvag_ref[...])
    val = _dot(kva.astype(BF16), wukv_v_ref[...])
    for h in range(MLA_HEADS):
        v_ref[h] = val[:, h * MLA_V:(h + 1) * MLA_V].astype(v_ref.dtype)
    kn_t = _dot(wukv_kt_ref[...], kva.T.astype(BF16))
    pe_t = p_ref[:, OD_KPE0:OD_W].astype(F32).T
    kpe_t, rot_t = pe_t[:MLA_ROPE], pe_t[MLA_ROPE:]
    pe_ss = jnp.sum(kpe_t * kpe_t, axis=0, keepdims=True)
    k_rope = kpe_t * kg_rope_ref[...] * cos_t_ref[...] + rot_t * kg_rot_ref[...] * sin_t_ref[...]
    for h in range(MLA_HEADS):
        kn = kn_t[h * MLA_NOPE:(h + 1) * MLA_NOPE]
        ss = jnp.sum(kn * kn, axis=0, keepdims=True) + pe_ss
        inv = lax.rsqrt(ss * (1.0 / MLA_QK) + RMS_EPS)
        k_ref[h] = jnp.concatenate([kn * kg_nope_ref[...] * inv, k_rope * inv], axis=0).astype(k_ref.dtype)


def _head_sum_matrix():
    lane_head = np.concatenate([np.repeat(np.arange(MLA_HEADS), MLA_NOPE),
                                np.repeat(np.arange(MLA_HEADS), MLA_ROPE)])
    return jnp.asarray(lane_head[:, None] == np.arange(LANE)[None, :], BF16)


def _mla_prep(p, rope_tabs, qa_g, wuq, kva_g, wukv, qn_g, kn_g):
    cos, sin, cos_t, sin_t = rope_tabs
    head_sum = _head_sum_matrix()
    wukv = _wukv_cols(wukv)
    wukv_kt, wukv_v = wukv[:, :MLA_NOPE_W].T, wukv[:, MLA_NOPE_W:]
    wuq = _wuq_cols(wuq)
    tile6 = lambda g: jnp.tile(g, MLA_HEADS)
    qgain = jnp.concatenate([tile6(qn_g[:MLA_NOPE]), tile6(qn_g[MLA_NOPE:]),
                             tile6(_rot_gain(qn_g[MLA_NOPE:]))]).reshape(1, -1)
    col = lambda g: jnp.broadcast_to(g[:, None], (g.shape[0], ROW_TILE))
    head_spec = lambda d: pl.BlockSpec((MLA_HEADS, ROW_TILE, d), lambda i: (0, i, 0))
    rope_spec = pl.BlockSpec((ROW_TILE, MLA_ROPE_W), lambda i: (i % TILES_PER_BATCH, 0))
    rope_t_spec = pl.BlockSpec((MLA_ROPE, ROW_TILE), lambda i: (0, i % TILES_PER_BATCH))
    return pl.pallas_call(
        _mla_prep_kernel,
        out_shape=(jax.ShapeDtypeStruct((MLA_HEADS, NTOK, MLA_QK), BF16),
                   jax.ShapeDtypeStruct((MLA_HEADS, MLA_QK, NTOK), BF16),
                   jax.ShapeDtypeStruct((MLA_HEADS, NTOK, MLA_V), BF16)),
        grid=(NTOK // ROW_TILE,),
        in_specs=[pl.BlockSpec((ROW_TILE, OD_W), lambda i: (i, 0)),
                  rope_spec, rope_spec, rope_t_spec, rope_t_spec,
                  _resident((1, MLA_Q_RANK)), _resident(wuq.shape),
                  _resident((1, MLA_KV_RANK)), _resident(wukv_kt.shape), _resident(wukv_v.shape),
                  _resident(qgain.shape), _resident(head_sum.shape),
                  _resident((MLA_NOPE, ROW_TILE)), _resident((MLA_ROPE, ROW_TILE)), _resident((MLA_ROPE, ROW_TILE))],
        out_specs=(head_spec(MLA_QK),
                   pl.BlockSpec((MLA_HEADS, MLA_QK, ROW_TILE), lambda i: (0, 0, i)),
                   head_spec(MLA_V)),
        compiler_params=_params(("parallel",)),
        name="mla_prep",
    )(p, cos, sin, cos_t, sin_t, qa_g.reshape(1, -1), wuq, kva_g.reshape(1, -1), wukv_kt, wukv_v,
      qgain, head_sum,
      col(kn_g[:MLA_NOPE]), col(kn_g[MLA_NOPE:]), col(_rot_gain(kn_g[MLA_NOPE:])))


ATT_Q_TILE = 512


def _softmax_av(q, k_t, v):
    s = _dot(q, k_t)
    p = jnp.exp(s - jnp.max(s, axis=-1, keepdims=True))
    return _dot(p.astype(BF16), v) / jnp.sum(p, axis=-1, keepdims=True)


def _attn_kernel(q_ref, kt_ref, v_ref, o_ref, *, need_ctx):
    k_t, v = kt_ref[0], v_ref[0]
    for j in range(SEQ // ATT_Q_TILE):
        rows = slice(CTX_LEN + j * ATT_Q_TILE, CTX_LEN + (j + 1) * ATT_Q_TILE)
        o_ref[rows, :] = _softmax_av(q_ref[0, rows, :], k_t, v).astype(o_ref.dtype)
    if need_ctx:
        o_ref[:CTX_LEN, :] = _softmax_av(q_ref[0, :CTX_LEN, :], k_t[:, :CTX_LEN], v[:CTX_LEN]).astype(o_ref.dtype)
    else:
        o_ref[:CTX_LEN, :] = jnp.zeros((CTX_LEN, MLA_V), o_ref.dtype)


def _attention(q, k_t, v, need_ctx):
    head_spec = lambda d: pl.BlockSpec((1, TOK, d), lambda b, h: (h, b, 0))
    return pl.pallas_call(
        functools.partial(_attn_kernel, need_ctx=need_ctx),
        out_shape=jax.ShapeDtypeStruct((NTOK, MLA_HEADS * MLA_V), BF16),
        grid=(BATCH, MLA_HEADS),
        in_specs=[head_spec(MLA_QK),
                  pl.BlockSpec((1, MLA_QK, TOK), lambda b, h: (h, 0, b)),
                  head_spec(MLA_V)],
        out_specs=pl.BlockSpec((TOK, MLA_V), lambda b, h: (b, h)),
        compiler_params=_params(("parallel", "parallel")),
        name="attention",
    )(q, k_t, v)


def _even_w_in(w):
    gates0 = GLA_MAIN_W
    gates1 = gates0 + 2 * GLA_GATE_RANK
    pad = jnp.zeros(w.shape[:-1] + (EV_W - w.shape[-1],), w.dtype)
    return jnp.concatenate([w[..., :gates0], w[..., gates1:], w[..., gates0:gates1], pad], axis=-1).astype(BF16)


def _gate_w(wa, first_row):
    out = jnp.zeros((LANE, GLA_QK_W), wa.dtype)
    return lax.dynamic_update_slice(out, wa, (first_row, 0)).astype(BF16)


def _odd_w_in(w):
    return jnp.concatenate([w, _rot_cols(w[..., OD_KPE0:OD_KPE_ROT0])], axis=-1).astype(BF16)


def _wuq_cols(w):
    w = w.reshape(MLA_Q_RANK, MLA_HEADS, MLA_QK)
    nope = w[:, :, :MLA_NOPE].reshape(MLA_Q_RANK, MLA_NOPE_W)
    rope = w[:, :, MLA_NOPE:]
    return jnp.concatenate([nope, rope.reshape(MLA_Q_RANK, MLA_ROPE_W),
                            _rot_cols(rope).reshape(MLA_Q_RANK, MLA_ROPE_W)], axis=1).astype(BF16)


def _wukv_cols(w):
    w = w.reshape(MLA_KV_RANK, MLA_HEADS, MLA_NOPE + MLA_V)
    return jnp.concatenate([w[:, :, :MLA_NOPE].reshape(MLA_KV_RANK, MLA_NOPE_W),
                            w[:, :, MLA_NOPE:].reshape(MLA_KV_RANK, MLA_HEADS * MLA_V)], axis=1).astype(BF16)


def kernel(x, c, ctx, c_ctx, ada_w, ada_b, norm_mix_g, norm_ffn_g, w_mix_out, ffn_w_in, ffn_w_out, ev_w_in, gla_wa_f, gla_ba_f, gla_wa_b, gla_ba_b, gla_onorm_g, sg_vnorm_g, sg_ws, sg_bs, od_w_in, mla_qa_g, mla_wuq, mla_kva_g, mla_wukv, mla_qn_g, mla_kn_g):
    cvec = jnp.zeros((MOD_ROWS, D_MODEL), F32).at[:BATCH].set(c).at[CTX_MOD_ROW].set(c_ctx)
    mod = _ada_table(cvec, ada_w, ada_b)
    rope_tabs = _rope_tables()
    norm_mix_g = norm_mix_g.reshape(DEPTH, 1, D_MODEL)
    norm_ffn_g = norm_ffn_g.reshape(DEPTH, 1, D_MODEL)
    wmix, win, wout = w_mix_out.astype(BF16), ffn_w_in.astype(BF16), ffn_w_out.astype(BF16)
    w_in_stacks = (_even_w_in(ev_w_in), _odd_w_in(od_w_in))

    p, xa = _first_inproj(x, ctx, mod, norm_mix_g, w_in_stacks[0])
    for l in range(DEPTH):
        last = l == DEPTH - 1
        i = l // 2
        if l % 2 == 0:
            ma = _gla(p, _gate_w(gla_wa_f[i], 0), gla_ba_f[i].reshape(1, GLA_QK_W),
                      _gate_w(gla_wa_b[i], GLA_GATE_RANK), gla_ba_b[i].reshape(1, GLA_QK_W), gla_onorm_g[i])
            mb = _spatial_gating(p, sg_vnorm_g[i], sg_ws[i].astype(BF16), sg_bs[i].T)
        else:
            q, k, v = _mla_prep(p, rope_tabs, mla_qa_g[i], mla_wuq[i], mla_kva_g[i], mla_wukv[i],
                                mla_qn_g[i], mla_kn_g[i])
            ma = _fourier(p, not last)
            mb = _attention(q, k, v, not last)
        nxt = None if last else (norm_mix_g, w_in_stacks[(l + 1) % 2], (l + 1) // 2)
        res = _layer_tail(xa, ma, mb, mod, l, norm_ffn_g, wmix, win, wout, nxt)
        xa, p = (res[0], None) if last else res
    return xa.reshape(BATCH, SEQ, D_MODEL)
```

```python
import functools

import numpy as np
import jax
import jax.numpy as jnp
from jax import lax
from jax.experimental import pallas as pl
from jax.experimental.pallas import tpu as pltpu

D_MODEL = 1024
BATCH = 8
SEQ = 2048
DEPTH = 4
CTX_LEN = 256
GRID_W = 64
RMS_EPS = 1e-6
GLA_HEADS = 4
GLA_DK = 64
GLA_DV = 128
GLA_GATE_RANK = 16
GLA_GATE_NORM = 16.0
GLA_CHUNK = 64
SG_GROUPS = 4
SG_DIM = 128
SG_CHUNK = 128
FT_GROUPS = 4
FT_DIM = 64
MLA_HEADS = 6
MLA_NOPE = 128
MLA_ROPE = 64
MLA_V = 128
MLA_Q_RANK = 384
MLA_KV_RANK = 256
ROPE_BASE = 10000.0
FFN_HIDDEN = 2816

TOK = CTX_LEN + SEQ
NTOK = BATCH * TOK
ROW_TILE = 256
TILES_PER_BATCH = TOK // ROW_TILE
CTX_TILES = CTX_LEN // ROW_TILE
MOD_ROWS = 16
CTX_MOD_ROW = BATCH
LANE = 128
EV_W = 2688
OD_W = 1024
MLA_QK = MLA_NOPE + MLA_ROPE
VMEM_LIMIT = 56 * 1024 * 1024

F32 = jnp.float32
BF16 = jnp.bfloat16


def _dot(a, b):
    return jnp.dot(a, b, preferred_element_type=F32)


def _rms(x, g):
    return x * lax.rsqrt(jnp.mean(x * x, axis=-1, keepdims=True) + RMS_EPS) * g


def _lane_rms(x, g):
    width = x.shape[1]
    ss = _dot((x * x).astype(BF16), jnp.ones((width, LANE), BF16))
    inv = lax.rsqrt(ss * (1.0 / width) + RMS_EPS)
    return x * jnp.concatenate([inv] * (width // LANE), axis=1) * g


def _params(sem):
    return pltpu.CompilerParams(dimension_semantics=sem, vmem_limit_bytes=VMEM_LIMIT)


def _resident(shape):
    zeros = (0,) * len(shape)
    return pl.BlockSpec(shape, lambda *_: zeros, pipeline_mode=pl.Buffered(1))


def _mod_row(i):
    return jnp.where(i % TILES_PER_BATCH < CTX_TILES, CTX_MOD_ROW, i // TILES_PER_BATCH)


def _mod_spec(layer, piece, tile_of_step):
    def index_map(i):
        return ((layer * MOD_ROWS + _mod_row(tile_of_step(i))) * 6 + piece, 0, 0)
    return pl.BlockSpec((1, 1, D_MODEL), index_map)


def _ada_kernel(c_ref, w_ref, b_ref, o_ref):
    c = c_ref[...]
    s = c * jax.nn.sigmoid(c)
    o_ref[0] = jnp.dot(s, w_ref[0], precision=lax.Precision.HIGHEST, preferred_element_type=F32) + b_ref[0]


def _ada_table(cvec, ada_w, ada_b):
    tn = 1536
    n = 6 * D_MODEL
    out = pl.pallas_call(
        _ada_kernel,
        out_shape=jax.ShapeDtypeStruct((DEPTH, MOD_ROWS, n), F32),
        grid=(DEPTH, n // tn),
        in_specs=[pl.BlockSpec((MOD_ROWS, D_MODEL), lambda l, j: (0, 0)),
                  pl.BlockSpec((1, D_MODEL, tn), lambda l, j: (l, 0, j)),
                  pl.BlockSpec((1, 1, tn), lambda l, j: (l, 0, j))],
        out_specs=pl.BlockSpec((1, MOD_ROWS, tn), lambda l, j: (l, 0, j)),
        compiler_params=_params(("arbitrary", "arbitrary")),
        name="ada_table",
    )(cvec, ada_w, ada_b.reshape(DEPTH, 1, n))
    return out.reshape(DEPTH * MOD_ROWS * 6, 1, D_MODEL)


def _layer_block(shape, layer):
    zeros = (0,) * len(shape)
    return pl.BlockSpec((1,) + tuple(shape), lambda *_: (layer,) + zeros, pipeline_mode=pl.Buffered(1))


def _modulated_norm(x, g, shift, scale):
    return _rms(x, g) * (1.0 + scale) + shift


def _first_inproj_kernel(x_ref, ctx_ref, g_ref, shift_ref, scale_ref, w_ref, p_ref, xa_ref):
    is_ctx = pl.program_id(0) % TILES_PER_BATCH < CTX_TILES
    x = jnp.where(is_ctx, ctx_ref[...], x_ref[...])
    xa_ref[...] = x
    z = _modulated_norm(x, g_ref[0], shift_ref[0], scale_ref[0])
    p_ref[...] = _dot(z.astype(BF16), w_ref[0]).astype(p_ref.dtype)


def _first_inproj(x, ctx, mod, norm_g, w):
    width = w.shape[-1]
    x_tiles = SEQ // ROW_TILE
    x_tile = lambda i: ((i // TILES_PER_BATCH) * x_tiles + jnp.maximum(i % TILES_PER_BATCH - CTX_TILES, 0), 0)
    ctx_tile = lambda i: ((i // TILES_PER_BATCH) * CTX_TILES + jnp.minimum(i % TILES_PER_BATCH, CTX_TILES - 1), 0)
    ident = lambda i: i
    row_out = lambda wd: pl.BlockSpec((ROW_TILE, wd), lambda i: (i, 0))
    return pl.pallas_call(
        _first_inproj_kernel,
        out_shape=(jax.ShapeDtypeStruct((NTOK, width), BF16), jax.ShapeDtypeStruct((NTOK, D_MODEL), F32)),
        grid=(NTOK // ROW_TILE,),
        in_specs=[pl.BlockSpec((ROW_TILE, D_MODEL), x_tile),
                  pl.BlockSpec((ROW_TILE, D_MODEL), ctx_tile),
                  _layer_block((1, D_MODEL), 0),
                  _mod_spec(0, 0, ident), _mod_spec(0, 1, ident),
                  _layer_block((D_MODEL, width), 0)],
        out_specs=(row_out(width), row_out(D_MODEL)),
        compiler_params=_params(("parallel",)),
        name="first_inproj",
    )(x.reshape(BATCH * SEQ, D_MODEL), ctx.reshape(BATCH * CTX_LEN, D_MODEL), norm_g, mod, mod, w)


def _tail_kernel(x_ref, ma_ref, mb_ref, gm_ref, sf_ref, cf_ref, gf_ref, g_ref,
                 wmix_ref, win_ref, wout_ref, *rest, has_next):
    m = jnp.concatenate([ma_ref[...], mb_ref[...]], axis=1)
    x1 = x_ref[...] + gm_ref[0] * _dot(m, wmix_ref[0])
    h = _modulated_norm(x1, g_ref[0], sf_ref[0], cf_ref[0]).astype(BF16)
    gu = _dot(h, win_ref[0])
    gate = gu[:, :FFN_HIDDEN]
    act = (gate * jax.nn.sigmoid(gate) * gu[:, FFN_HIDDEN:]).astype(BF16)
    x2 = x1 + gf_ref[0] * _dot(act, wout_ref[0])
    if has_next:
        shift_ref, scale_ref, gn_ref, wn_ref, o_ref, p_ref = rest
        z = _modulated_norm(x2, gn_ref[0], shift_ref[0], scale_ref[0])
        p_ref[...] = _dot(z.astype(BF16), wn_ref[0]).astype(p_ref.dtype)
    else:
        (o_ref,) = rest
    o_ref[...] = x2


def _layer_tail(xa, ma, mb, mod, layer, norm_ffn_g, wmix, win, wout, nxt):
    if nxt is None:
        x_tiles = SEQ // ROW_TILE
        n_steps = BATCH * x_tiles
        tile = lambda i: (i // x_tiles) * TILES_PER_BATCH + CTX_TILES + i % x_tiles
        out_rows = BATCH * SEQ
    else:
        n_steps = NTOK // ROW_TILE
        tile = lambda i: i
        out_rows = NTOK
    row_in = lambda width: pl.BlockSpec((ROW_TILE, width), lambda i: (tile(i), 0))
    row_out = lambda width: pl.BlockSpec((ROW_TILE, width), lambda i: (i, 0))
    in_specs = [row_in(D_MODEL), row_in(ma.shape[1]), row_in(mb.shape[1]),
                _mod_spec(layer, 2, tile), _mod_spec(layer, 3, tile),
                _mod_spec(layer, 4, tile), _mod_spec(layer, 5, tile),
                _layer_block((1, D_MODEL), layer),
                _layer_block((D_MODEL, D_MODEL), layer),
                _layer_block((D_MODEL, 2 * FFN_HIDDEN), layer),
                _layer_block((FFN_HIDDEN, D_MODEL), layer)]
    args = [xa, ma, mb, mod, mod, mod, mod, norm_ffn_g, wmix, win, wout]
    out_shape = [jax.ShapeDtypeStruct((out_rows, D_MODEL), F32)]
    out_specs = [row_out(D_MODEL)]
    if nxt is not None:
        norm_mix_g, w_next, idx = nxt
        width = w_next.shape[-1]
        in_specs += [_mod_spec(layer + 1, 0, tile), _mod_spec(layer + 1, 1, tile),
                     _layer_block((1, D_MODEL), layer + 1), _layer_block((D_MODEL, width), idx)]
        args += [mod, mod, norm_mix_g, w_next]
        out_shape.append(jax.ShapeDtypeStruct((out_rows, width), BF16))
        out_specs.append(row_out(width))
    return pl.pallas_call(
        functools.partial(_tail_kernel, has_next=nxt is not None),
        out_shape=tuple(out_shape),
        grid=(n_steps,),
        in_specs=in_specs,
        out_specs=tuple(out_specs),
        compiler_params=_params(("parallel",)),
        name="layer_tail",
    )(*args)


GLA_QK_W = GLA_HEADS * GLA_DK
GLA_V_W = GLA_HEADS * GLA_DV
GLA_MAIN_W = 2 * GLA_QK_W + 2 * GLA_V_W
GLA_NCHUNK = TOK // GLA_CHUNK


GLA_BLOCK = 256
GLA_NBLOCK = TOK // GLA_BLOCK
GLA_CTX_BLOCKS = CTX_LEN // GLA_BLOCK


def _log_sigmoid(y):
    return jnp.minimum(y, 0.0) - jnp.log(1.0 + jnp.exp(-jnp.abs(y)))


def _gla_kernel(main_ref, gate_ref, waf_ref, baf_ref, wab_ref, bab_ref, og_ref, o_ref,
                of_scr, qd_scr, kd_scr, qe_scr, kl_scr, dec_scr, s_scr):
    C, BLK = GLA_CHUNK, GLA_BLOCK
    cpb = BLK // C
    rb = lax.broadcasted_iota(jnp.int32, (BLK, BLK), 0)
    cb = lax.broadcasted_iota(jnp.int32, (BLK, BLK), 1)
    same_chunk = (rb // C) == (cb // C)
    tri_f = jnp.where(same_chunk & (rb >= cb), 1.0, 0.0).astype(BF16)
    tri_b = jnp.where(same_chunk & (rb <= cb), 1.0, 0.0).astype(BF16)
    r = lax.broadcasted_iota(jnp.int32, (C, C), 0)
    cc = lax.broadcasted_iota(jnp.int32, (C, C), 1)
    lower, upper = r >= cc, r <= cc
    sr = lax.broadcasted_iota(jnp.int32, (BLK, cpb * LANE), 0)
    sc = lax.broadcasted_iota(jnp.int32, (BLK, cpb * LANE), 1)
    chunk_sum = jnp.where(sr // C == sc // LANE, 1.0, 0.0).astype(BF16)

    def split(x):
        hi = x.astype(BF16)
        return hi, (x - hi.astype(F32)).astype(BF16)

    def dot_01(m, x):
        hi, lo = split(x)
        return _dot(jnp.concatenate([m, m], axis=1), jnp.concatenate([hi, lo], axis=0))

    def dot_10(x, m):
        hi, lo = split(x)
        return _dot(jnp.concatenate([hi, lo], axis=1), jnp.concatenate([m, m], axis=0))

    def block_rows(blk):
        return pl.ds(pl.multiple_of(blk * BLK, BLK), BLK)

    def chunk_rows(blk, j):
        return pl.ds(pl.multiple_of(blk * BLK + j * C, C), C)

    def dec_rows(blk, j):
        return pl.ds(pl.multiple_of((blk * cpb + j) * GLA_QK_W, GLA_QK_W), GLA_QK_W)

    def decay_terms(a, q, k, w_ref, bias_ref, tri, end_row):
        g = _log_sigmoid(_dot(a, w_ref[...]) + bias_ref[...]) * (1.0 / GLA_GATE_NORM)
        b = dot_01(tri, g)
        dec_all = jnp.exp(dot_10(g.T, chunk_sum))
        terms = []
        for j in range(cpb):
            sl = slice(j * C, (j + 1) * C)
            bc = b[sl]
            b_mid = bc[C // 2:C // 2 + 1]
            b_end = bc[end_row:end_row + 1]
            d = bc - b_mid
            qd = q[sl] * jnp.exp(d)
            kd = k[sl] * jnp.exp(-d)
            qe = qd * jnp.exp(b_mid)
            kl = kd * jnp.exp(b_end - b_mid)
            dec = dec_all[:, j * LANE:(j + 1) * LANE]
            terms.append((qd.astype(BF16), kd.T.astype(BF16), qe.astype(BF16), kl.T.astype(BF16), dec))
        return terms

    def step(qd, kd_t, qe, kl_t, dec, v, keep):
        outs = []
        for h in range(GLA_HEADS):
            ks = slice(h * GLA_DK, (h + 1) * GLA_DK)
            vh = v[:, h * GLA_DV:(h + 1) * GLA_DV]
            s_h = s_scr[ks, :]
            att = jnp.where(keep, _dot(qd[:, ks], kd_t[ks, :]), 0.0).astype(BF16)
            outs.append(_dot(att, vh) + _dot(qe[:, ks], s_h.astype(BF16)))
            s_scr[ks, :] = s_h * dec[ks, :] + _dot(kl_t[ks, :], vh)
        return outs

    s_scr[...] = jnp.zeros(s_scr.shape, F32)

    def loop1(blk, carry):
        rows = block_rows(blk)
        a = gate_ref[rows, :]
        q = main_ref[rows, 0:GLA_QK_W].astype(F32) * (GLA_DK ** -0.5)
        k = main_ref[rows, GLA_QK_W:2 * GLA_QK_W].astype(F32)
        v = main_ref[rows, 2 * GLA_QK_W:2 * GLA_QK_W + GLA_V_W]
        fwd = decay_terms(a, q, k, waf_ref, baf_ref, tri_f, C - 1)
        bwd = decay_terms(a, q, k, wab_ref, bab_ref, tri_b, 0)
        for j in range(cpb):
            cr = chunk_rows(blk, j)
            qd, kd, qe, kl, dec = bwd[j]
            qd_scr[cr, :] = qd
            kd_scr[dec_rows(blk, j), :] = kd
            qe_scr[cr, :] = qe
            kl_scr[dec_rows(blk, j), :] = kl
            dec_scr[dec_rows(blk, j), :] = dec
            qd, kd, qe, kl, dec = fwd[j]
            outs = step(qd, kd, qe, kl, dec, v[j * C:(j + 1) * C], lower)
            of_scr[cr, :] = jnp.concatenate(outs, axis=1)
        return carry

    lax.fori_loop(0, GLA_NBLOCK, loop1, 0)

    s_scr[...] = jnp.zeros(s_scr.shape, F32)

    def loop2(i, carry):
        blk = jnp.where(i < GLA_CTX_BLOCKS, GLA_CTX_BLOCKS - 1 - i, GLA_NBLOCK - 1 - (i - GLA_CTX_BLOCKS))
        for j in reversed(range(cpb)):
            cr = chunk_rows(blk, j)
            v = main_ref[cr, 2 * GLA_QK_W:2 * GLA_QK_W + GLA_V_W]
            kr = dec_rows(blk, j)
            outs = step(qd_scr[cr, :], kd_scr[kr, :], qe_scr[cr, :], kl_scr[kr, :], dec_scr[kr, :], v, upper)
            gout = main_ref[cr, 2 * GLA_QK_W + GLA_V_W:GLA_MAIN_W].astype(F32)
            res = []
            for h in range(GLA_HEADS):
                vs = slice(h * GLA_DV, (h + 1) * GLA_DV)
                o = outs[h] + of_scr[cr, vs]
                gh = gout[:, vs]
                res.append(_rms(o, og_ref[...]) * (gh * jax.nn.sigmoid(gh)))
            o_ref[cr, :] = jnp.concatenate(res, axis=1).astype(o_ref.dtype)
        return carry

    lax.fori_loop(0, GLA_NBLOCK, loop2, 0)


def _gla(p, waf, baf, wab, bab, onorm_g):
    gate_col = (EV_W - LANE) // LANE
    qk_scr = pltpu.VMEM((TOK, GLA_QK_W), BF16)
    qk_t_scr = pltpu.VMEM((GLA_NCHUNK * GLA_QK_W, GLA_CHUNK), BF16)
    return pl.pallas_call(
        _gla_kernel,
        out_shape=jax.ShapeDtypeStruct((NTOK, GLA_V_W), BF16),
        grid=(BATCH,),
        in_specs=[pl.BlockSpec((TOK, GLA_MAIN_W), lambda b: (b, 0)),
                  pl.BlockSpec((TOK, LANE), lambda b: (b, gate_col)),
                  _resident((LANE, GLA_QK_W)), _resident((1, GLA_QK_W)),
                  _resident((LANE, GLA_QK_W)), _resident((1, GLA_QK_W)),
                  _resident((1, GLA_DV))],
        out_specs=pl.BlockSpec((TOK, GLA_V_W), lambda b: (b, 0)),
        scratch_shapes=[pltpu.VMEM((TOK, GLA_V_W), F32), qk_scr, qk_t_scr, qk_scr, qk_t_scr,
                        pltpu.VMEM((GLA_NCHUNK * GLA_QK_W, LANE), F32),
                        pltpu.VMEM((GLA_QK_W, GLA_DV), F32)],
        compiler_params=_params(("parallel",)),
        name="gla",
    )(p, p, waf, baf, wab, bab, onorm_g.reshape(1, GLA_DV))


def _gelu(x):
    return 0.5 * x * (1.0 + lax.erf(x * (2.0 ** -0.5)))


SG_TILE = 768


def _sg_kernel(u_ref, v_ref, vg_ref, ws_ref, bs_ref, o_ref):
    for c in range(SG_TILE // SG_CHUNK):
        rows = slice(c * SG_CHUNK, (c + 1) * SG_CHUNK)
        for g in range(SG_GROUPS):
            cols = slice(g * SG_DIM, (g + 1) * SG_DIM)
            u = _gelu(u_ref[rows, cols].astype(F32))
            vn = _lane_rms(_gelu(v_ref[rows, cols].astype(F32)), vg_ref[g:g + 1, :])
            mixed = _dot(ws_ref[g], vn.astype(BF16)) + bs_ref[g]
            o_ref[rows, cols] = (u * mixed).astype(o_ref.dtype)


def _spatial_gating(p, vnorm_g, ws, bs):
    width = SG_GROUPS * SG_DIM
    u_col = GLA_MAIN_W // width
    bias = jnp.broadcast_to(bs[:, :, None], (SG_GROUPS, SG_CHUNK, SG_DIM))
    return pl.pallas_call(
        _sg_kernel,
        out_shape=jax.ShapeDtypeStruct((NTOK, width), BF16),
        grid=(NTOK // SG_TILE,),
        in_specs=[pl.BlockSpec((SG_TILE, width), lambda i: (i, u_col)),
                  pl.BlockSpec((SG_TILE, width), lambda i: (i, u_col + 1)),
                  _resident((SG_GROUPS, SG_DIM)),
                  _resident((SG_GROUPS, SG_CHUNK, SG_CHUNK)),
                  _resident((SG_GROUPS, SG_CHUNK, SG_DIM))],
        out_specs=pl.BlockSpec((SG_TILE, width), lambda i: (i, 0)),
        compiler_params=_params(("parallel",)),
        name="spatial_gating",
    )(p, p, vnorm_g, ws, bias)


FT_W = FT_GROUPS * FT_DIM


def _dft_cos_sin(n):
    jk = np.outer(np.arange(n), np.arange(n)) % n
    ang = 2.0 * np.pi * jk.astype(np.float64) / n
    return np.cos(ang), np.sin(ang)


def _fourier_kernel(h_ref, cs_ref, cl_ref, sl_ref, clc_ref, slc_ref, o_ref, *, need_ctx):
    ab = _dot(h_ref[...], cs_ref[...]).astype(BF16)
    a, b = ab[:, :FT_W], ab[:, FT_W:]
    scale_x = (SEQ * FT_DIM) ** -0.5
    out_x = (_dot(cl_ref[...], a[CTX_LEN:]) - _dot(sl_ref[...], b[CTX_LEN:])) * scale_x
    o_ref[CTX_LEN:, :] = out_x.astype(o_ref.dtype)
    if need_ctx:
        scale_c = (CTX_LEN * FT_DIM) ** -0.5
        out_c = (_dot(clc_ref[...], a[:CTX_LEN]) - _dot(slc_ref[...], b[:CTX_LEN])) * scale_c
        o_ref[:CTX_LEN, :] = out_c.astype(o_ref.dtype)
    else:
        o_ref[:CTX_LEN, :] = jnp.zeros((CTX_LEN, FT_W), o_ref.dtype)


def _fourier(p, need_ctx):
    cc, sc = _dft_cos_sin(FT_DIM)
    eye = np.eye(FT_GROUPS)
    cs = jnp.asarray(np.concatenate([np.kron(eye, cc), np.kron(eye, sc)], axis=1), F32).astype(BF16)
    cl, sl = (jnp.asarray(m, F32).astype(BF16) for m in _dft_cos_sin(SEQ))
    clc, slc = (jnp.asarray(m, F32).astype(BF16) for m in _dft_cos_sin(CTX_LEN))
    return pl.pallas_call(
        functools.partial(_fourier_kernel, need_ctx=need_ctx),
        out_shape=jax.ShapeDtypeStruct((NTOK, FT_W), BF16),
        grid=(BATCH,),
        in_specs=[pl.BlockSpec((TOK, FT_W), lambda b: (b, 0)),
                  _resident((FT_W, 2 * FT_W)),
                  _resident((SEQ, SEQ)), _resident((SEQ, SEQ)),
                  _resident((CTX_LEN, CTX_LEN)), _resident((CTX_LEN, CTX_LEN))],
        out_specs=pl.BlockSpec((TOK, FT_W), lambda b: (b, 0)),
        compiler_params=_params(("parallel",)),
        name="fourier",
    )(p, cs, cl, sl, clc, slc)


MLA_NOPE_W = MLA_HEADS * MLA_NOPE
MLA_ROPE_W = MLA_HEADS * MLA_ROPE
OD_QA0 = FT_W
OD_KVA0 = OD_QA0 + MLA_Q_RANK
OD_KPE0 = OD_KVA0 + MLA_KV_RANK
OD_KPE_ROT0 = OD_KPE0 + MLA_ROPE


def _rot_cols(w):
    q = MLA_ROPE // 4
    return jnp.concatenate([-w[..., q:2 * q], w[..., 0:q], -w[..., 3 * q:4 * q], w[..., 2 * q:3 * q]], axis=-1)


def _rot_gain(g):
    q = MLA_ROPE // 4
    return jnp.concatenate([g[..., q:2 * q], g[..., 0:q], g[..., 3 * q:4 * q], g[..., 2 * q:3 * q]], axis=-1)


def _rope_tables():
    rows = SEQ // GRID_W
    row_id = np.repeat(np.arange(rows, dtype=np.float32), GRID_W)
    col_id = np.tile(np.arange(GRID_W, dtype=np.float32), rows)
    axis_dim = MLA_ROPE // 2
    inv_freq = (np.float32(ROPE_BASE) ** (-np.arange(0, axis_dim, 2, dtype=np.float32) / np.float32(axis_dim))).astype(np.float32)
    ang_r = row_id[:, None] * inv_freq
    ang_c = col_id[:, None] * inv_freq
    ang = np.concatenate([ang_r, ang_r, ang_c, ang_c], axis=-1).astype(np.float32)
    cos = np.concatenate([np.ones((CTX_LEN, MLA_ROPE), np.float32), np.cos(ang)], axis=0)
    sin = np.concatenate([np.zeros((CTX_LEN, MLA_ROPE), np.float32), np.sin(ang)], axis=0)
    return tuple(jnp.asarray(t, F32) for t in (cos, sin, cos.T, sin.T))


MLA_TILE = 768
MLA_VT_ROWS = MLA_V + 16


def _mla_prep_kernel(p_ref, cos_ref, sin_ref, cos_t_ref, sin_t_ref, qag_ref, wuq_t_ref, kvag_ref,
                     wukv_k_ref, wukv_vt_ref, head_sum_ref, kgain_ref,
                     qg_nope_ref, qg_rope_ref, qg_rot_ref, qt_ref, k_ref, vt_ref):
    qa = _lane_rms(p_ref[:, OD_QA0:OD_KVA0].astype(F32), qag_ref[...])
    q_t = _dot(wuq_t_ref[...], qa.T.astype(BF16))
    cos_t, sin_t = cos_t_ref[...], sin_t_ref[...]
    for h in range(MLA_HEADS):
        qn = q_t[h * MLA_NOPE:(h + 1) * MLA_NOPE]
        qr = q_t[MLA_NOPE_W + h * MLA_ROPE:MLA_NOPE_W + (h + 1) * MLA_ROPE]
        qrr = q_t[MLA_NOPE_W + MLA_ROPE_W + h * MLA_ROPE:MLA_NOPE_W + MLA_ROPE_W + (h + 1) * MLA_ROPE]
        ss = jnp.sum(qn * qn, axis=0, keepdims=True) + jnp.sum(qr * qr, axis=0, keepdims=True)
        inv = lax.rsqrt(ss * (1.0 / MLA_QK) + RMS_EPS) * (MLA_QK ** -0.5)
        q_rope = qr * qg_rope_ref[...] * cos_t + qrr * qg_rot_ref[...] * sin_t
        qt_ref[h] = jnp.concatenate([qn * qg_nope_ref[...] * inv, q_rope * inv], axis=0).astype(qt_ref.dtype)

    kva = _lane_rms(p_ref[:, OD_KVA0:OD_KPE0].astype(F32), kvag_ref[...])
    v_t = _dot(wukv_vt_ref[...], kva.T.astype(BF16))
    ones = jnp.ones((MLA_VT_ROWS - MLA_V, v_t.shape[1]), F32)
    for h in range(MLA_HEADS):
        vt_ref[h] = jnp.concatenate([v_t[h * MLA_V:(h + 1) * MLA_V], ones], axis=0).astype(vt_ref.dtype)
    kn = _dot(kva.astype(BF16), wukv_k_ref[...])
    kpe = p_ref[:, OD_KPE0:OD_KPE_ROT0].astype(F32)
    kpe_rot = p_ref[:, OD_KPE_ROT0:OD_W].astype(F32)
    sq = jnp.concatenate([kn * kn, kpe * kpe], axis=1).astype(BF16)
    ss = _dot(sq, head_sum_ref[...])
    inv = lax.rsqrt(ss * (1.0 / MLA_QK) + RMS_EPS)
    kg = kgain_ref[...]
    k_rope = (kpe * kg[:, MLA_NOPE:MLA_QK] * cos_ref[...]
              + kpe_rot * kg[:, MLA_QK:] * sin_ref[...])
    for h in range(MLA_HEADS):
        k_h = jnp.concatenate([kn[:, h * MLA_NOPE:(h + 1) * MLA_NOPE] * kg[:, :MLA_NOPE], k_rope], axis=1)
        k_ref[h] = (k_h * inv[:, h:h + 1]).astype(k_ref.dtype)


def _key_head_sum_matrix():
    head = np.arange(LANE)[None, :]
    nope = np.repeat(np.arange(MLA_HEADS), MLA_NOPE)[:, None] == head
    rope = np.broadcast_to(head < MLA_HEADS, (MLA_ROPE, LANE))
    return jnp.asarray(np.concatenate([nope, rope], axis=0), BF16)


def _mla_prep(p, rope_tabs, qa_g, wuq, kva_g, wukv, qn_g, kn_g):
    cos, sin, cos_t, sin_t = rope_tabs
    head_sum = _key_head_sum_matrix()
    wukv = _wukv_cols(wukv)
    wukv_k, wukv_vt = wukv[:, :MLA_NOPE_W], wukv[:, MLA_NOPE_W:].T
    wuq_t = _wuq_cols(wuq).T
    kgain = jnp.concatenate([kn_g, _rot_gain(kn_g[MLA_NOPE:])]).reshape(1, -1)
    col = lambda g: jnp.broadcast_to(g[:, None], (g.shape[0], MLA_TILE))
    tiles_per_batch = TOK // MLA_TILE
    rows = lambda d: pl.BlockSpec((MLA_HEADS, MLA_TILE, d), lambda i: (0, i, 0))
    cols = lambda d: pl.BlockSpec((MLA_HEADS, d, MLA_TILE), lambda i: (0, 0, i))
    rope_spec = pl.BlockSpec((MLA_TILE, MLA_ROPE), lambda i: (i % tiles_per_batch, 0))
    rope_t_spec = pl.BlockSpec((MLA_ROPE, MLA_TILE), lambda i: (0, i % tiles_per_batch))
    return pl.pallas_call(
        _mla_prep_kernel,
        out_shape=(jax.ShapeDtypeStruct((MLA_HEADS, MLA_QK, NTOK), BF16),
                   jax.ShapeDtypeStruct((MLA_HEADS, NTOK, MLA_QK), BF16),
                   jax.ShapeDtypeStruct((MLA_HEADS, MLA_VT_ROWS, NTOK), BF16)),
        grid=(NTOK // MLA_TILE,),
        in_specs=[pl.BlockSpec((MLA_TILE, OD_W), lambda i: (i, 0)),
                  rope_spec, rope_spec, rope_t_spec, rope_t_spec,
                  _resident((1, MLA_Q_RANK)), _resident(wuq_t.shape),
                  _resident((1, MLA_KV_RANK)), _resident(wukv_k.shape), _resident(wukv_vt.shape),
                  _resident(head_sum.shape), _resident(kgain.shape),
                  _resident((MLA_NOPE, MLA_TILE)), _resident((MLA_ROPE, MLA_TILE)), _resident((MLA_ROPE, MLA_TILE))],
        out_specs=(cols(MLA_QK), rows(MLA_QK), cols(MLA_VT_ROWS)),
        compiler_params=_params(("parallel",)),
        name="mla_prep",
    )(p, cos, sin, cos_t, sin_t, qa_g.reshape(1, -1), wuq_t, kva_g.reshape(1, -1), wukv_k, wukv_vt,
      head_sum, kgain,
      col(qn_g[:MLA_NOPE]), col(qn_g[MLA_NOPE:]), col(_rot_gain(qn_g[MLA_NOPE:])))


ATT_Q_TILE = 512


ATT_KEY_CHUNK = TOK


def _weighted_values(acc):
    return (acc[:MLA_V] * (1.0 / acc[MLA_V:MLA_V + 1])).T


def _attn_kernel(qt_ref, k_ref, vt_ref, o_ref, s_scr, *, need_ctx):
    n_q = SEQ // ATT_Q_TILE
    chunks = [slice(c * ATT_KEY_CHUNK, (c + 1) * ATT_KEY_CHUNK) for c in range(TOK // ATT_KEY_CHUNK)]
    q_cols = lambda j: slice(CTX_LEN + j * ATT_Q_TILE, CTX_LEN + (j + 1) * ATT_Q_TILE)

    def scores(j, rows, col_max):
        s = _dot(k_ref[0, rows, :], qt_ref[0, :, q_cols(j)])
        s_scr[j % 2, rows, :] = s
        m = jnp.max(s, axis=0, keepdims=True)
        return m if col_max is None else jnp.maximum(col_max, m)

    col_max = None
    for rows in chunks:
        col_max = scores(0, rows, col_max)
    for j in range(n_q):
        acc, next_max = None, None
        for rows in chunks:
            if j + 1 < n_q:
                next_max = scores(j + 1, rows, next_max)
            p = jnp.exp(s_scr[j % 2, rows, :] - col_max).astype(BF16)
            part = _dot(vt_ref[0, :, rows], p)
            acc = part if acc is None else acc + part
        o_ref[q_cols(j), :] = _weighted_values(acc).astype(o_ref.dtype)
        col_max = next_max

    if need_ctx:
        s = _dot(k_ref[0, :CTX_LEN, :], qt_ref[0, :, :CTX_LEN])
        p = jnp.exp(s - jnp.max(s, axis=0, keepdims=True)).astype(BF16)
        o_ref[:CTX_LEN, :] = _weighted_values(_dot(vt_ref[0, :, :CTX_LEN], p)).astype(o_ref.dtype)
    else:
        o_ref[:CTX_LEN, :] = jnp.zeros((CTX_LEN, MLA_V), o_ref.dtype)


def _attention(q_t, k, v_t, need_ctx):
    transposed = lambda d: pl.BlockSpec((1, d, TOK), lambda b, h: (h, 0, b))
    return pl.pallas_call(
        functools.partial(_attn_kernel, need_ctx=need_ctx),
        out_shape=jax.ShapeDtypeStruct((NTOK, MLA_HEADS * MLA_V), BF16),
        grid=(BATCH, MLA_HEADS),
        in_specs=[transposed(MLA_QK),
                  pl.BlockSpec((1, TOK, MLA_QK), lambda b, h: (h, b, 0)),
                  transposed(MLA_VT_ROWS)],
        out_specs=pl.BlockSpec((TOK, MLA_V), lambda b, h: (b, h)),
        scratch_shapes=[pltpu.VMEM((2, TOK, ATT_Q_TILE), F32)],
        compiler_params=_params(("parallel", "parallel")),
        name="attention",
    )(q_t, k, v_t)


def _even_w_in(w):
    gates0 = GLA_MAIN_W
    gates1 = gates0 + 2 * GLA_GATE_RANK
    pad = jnp.zeros(w.shape[:-1] + (EV_W - w.shape[-1],), w.dtype)
    return jnp.concatenate([w[..., :gates0], w[..., gates1:], w[..., gates0:gates1], pad], axis=-1).astype(BF16)


def _gate_w(wa, first_row):
    out = jnp.zeros((LANE, GLA_QK_W), wa.dtype)
    return lax.dynamic_update_slice(out, wa, (first_row, 0)).astype(BF16)


def _odd_w_in(w):
    return jnp.concatenate([w, _rot_cols(w[..., OD_KPE0:OD_KPE_ROT0])], axis=-1).astype(BF16)


def _wuq_cols(w):
    w = w.reshape(MLA_Q_RANK, MLA_HEADS, MLA_QK)
    nope = w[:, :, :MLA_NOPE].reshape(MLA_Q_RANK, MLA_NOPE_W)
    rope = w[:, :, MLA_NOPE:]
    return jnp.concatenate([nope, rope.reshape(MLA_Q_RANK, MLA_ROPE_W),
                            _rot_cols(rope).reshape(MLA_Q_RANK, MLA_ROPE_W)], axis=1).astype(BF16)


def _wukv_cols(w):
    w = w.reshape(MLA_KV_RANK, MLA_HEADS, MLA_NOPE + MLA_V)
    return jnp.concatenate([w[:, :, :MLA_NOPE].reshape(MLA_KV_RANK, MLA_NOPE_W),
                            w[:, :, MLA_NOPE:].reshape(MLA_KV_RANK, MLA_HEADS * MLA_V)], axis=1).astype(BF16)


def kernel(x, c, ctx, c_ctx, ada_w, ada_b, norm_mix_g, norm_ffn_g, w_mix_out, ffn_w_in, ffn_w_out, ev_w_in, gla_wa_f, gla_ba_f, gla_wa_b, gla_ba_b, gla_onorm_g, sg_vnorm_g, sg_ws, sg_bs, od_w_in, mla_qa_g, mla_wuq, mla_kva_g, mla_wukv, mla_qn_g, mla_kn_g):
    cvec = jnp.zeros((MOD_ROWS, D_MODEL), F32).at[:BATCH].set(c).at[CTX_MOD_ROW].set(c_ctx)
    mod = _ada_table(cvec, ada_w, ada_b)
    rope_tabs = _rope_tables()
    norm_mix_g = norm_mix_g.reshape(DEPTH, 1, D_MODEL)
    norm_ffn_g = norm_ffn_g.reshape(DEPTH, 1, D_MODEL)
    wmix, win, wout = w_mix_out.astype(BF16), ffn_w_in.astype(BF16), ffn_w_out.astype(BF16)
    w_in_stacks = (_even_w_in(ev_w_in), _odd_w_in(od_w_in))

    p, xa = _first_inproj(x, ctx, mod, norm_mix_g, w_in_stacks[0])
    for l in range(DEPTH):
        last = l == DEPTH - 1
        i = l // 2
        if l % 2 == 0:
            ma = _gla(p, _gate_w(gla_wa_f[i], 0), gla_ba_f[i].reshape(1, GLA_QK_W),
                      _gate_w(gla_wa_b[i], GLA_GATE_RANK), gla_ba_b[i].reshape(1, GLA_QK_W), gla_onorm_g[i])
            mb = _spatial_gating(p, sg_vnorm_g[i], sg_ws[i].astype(BF16), sg_bs[i])
        else:
            q, k, v = _mla_prep(p, rope_tabs, mla_qa_g[i], mla_wuq[i], mla_kva_g[i], mla_wukv[i],
                                mla_qn_g[i], mla_kn_g[i])
            ma = _fourier(p, not last)
            mb = _attention(q, k, v, not last)
        nxt = None if last else (norm_mix_g, w_in_stacks[(l + 1) % 2], (l + 1) // 2)
        res = _layer_tail(xa, ma, mb, mod, l, norm_ffn_g, wmix, win, wout, nxt)
        xa, p = (res[0], None) if last else res
    return xa.reshape(BATCH, SEQ, D_MODEL)
```

```python
import functools

import numpy as np
import jax
import jax.numpy as jnp
from jax import lax
from jax.experimental import pallas as pl
from jax.experimental.pallas import tpu as pltpu

D_MODEL = 1024
BATCH = 8
SEQ = 2048
DEPTH = 4
CTX_LEN = 256
GRID_W = 64
RMS_EPS = 1e-6
GLA_HEADS = 4
GLA_DK = 64
GLA_DV = 128
GLA_GATE_RANK = 16
GLA_GATE_NORM = 16.0
GLA_CHUNK = 64
SG_GROUPS = 4
SG_DIM = 128
SG_CHUNK = 128
FT_GROUPS = 4
FT_DIM = 64
MLA_HEADS = 6
MLA_NOPE = 128
MLA_ROPE = 64
MLA_V = 128
MLA_Q_RANK = 384
MLA_KV_RANK = 256
ROPE_BASE = 10000.0
FFN_HIDDEN = 2816

TOK = CTX_LEN + SEQ
NTOK = BATCH * TOK
ROW_TILE = 256
TILES_PER_BATCH = TOK // ROW_TILE
CTX_TILES = CTX_LEN // ROW_TILE
MOD_ROWS = 16
CTX_MOD_ROW = BATCH
LANE = 128
EV_W = 2688
OD_W = 1024
MLA_QK = MLA_NOPE + MLA_ROPE
VMEM_LIMIT = 56 * 1024 * 1024

F32 = jnp.float32
BF16 = jnp.bfloat16


def _dot(a, b):
    return jnp.dot(a, b, preferred_element_type=F32)


def _rms(x, g):
    return x * lax.rsqrt(jnp.mean(x * x, axis=-1, keepdims=True) + RMS_EPS) * g


def _lane_rms(x, g):
    width = x.shape[1]
    ss = _dot((x * x).astype(BF16), jnp.ones((width, LANE), BF16))
    inv = lax.rsqrt(ss * (1.0 / width) + RMS_EPS)
    return x * jnp.concatenate([inv] * (width // LANE), axis=1) * g


def _params(sem):
    return pltpu.CompilerParams(dimension_semantics=sem, vmem_limit_bytes=VMEM_LIMIT)


def _resident(shape):
    zeros = (0,) * len(shape)
    return pl.BlockSpec(shape, lambda *_: zeros, pipeline_mode=pl.Buffered(1))


def _mod_row(i):
    return jnp.where(i % TILES_PER_BATCH < CTX_TILES, CTX_MOD_ROW, i // TILES_PER_BATCH)


MOD_PIECES = 6


def _mod_spec(layer, tile_of_step):
    def index_map(i):
        return (layer * MOD_ROWS + _mod_row(tile_of_step(i)), 0, 0)
    return pl.BlockSpec((1, MOD_PIECES, D_MODEL), index_map)


def _ada_kernel(c_ref, w_ref, b_ref, o_ref):
    c = c_ref[...]
    s = c * jax.nn.sigmoid(c)
    w = w_ref[0]
    w_hi = w.astype(BF16)
    w_lo = (w - w_hi.astype(F32)).astype(BF16)
    s_hi = s.astype(BF16)
    s_lo = (s - s_hi.astype(F32)).astype(BF16)
    o_ref[0] = _dot(s_hi, w_hi) + (_dot(s_hi, w_lo) + _dot(s_lo, w_hi)) + b_ref[0]


def _ada_table(cvec, ada_w, ada_b):
    tn = 3072
    n = MOD_PIECES * D_MODEL
    out = pl.pallas_call(
        _ada_kernel,
        out_shape=jax.ShapeDtypeStruct((DEPTH, MOD_ROWS, n), F32),
        grid=(DEPTH, n // tn),
        in_specs=[pl.BlockSpec((MOD_ROWS, D_MODEL), lambda l, j: (0, 0)),
                  pl.BlockSpec((1, D_MODEL, tn), lambda l, j: (l, 0, j)),
                  pl.BlockSpec((1, 1, tn), lambda l, j: (l, 0, j))],
        out_specs=pl.BlockSpec((1, MOD_ROWS, tn), lambda l, j: (l, 0, j)),
        compiler_params=_params(("arbitrary", "arbitrary")),
        name="ada_table",
    )(cvec, ada_w, ada_b.reshape(DEPTH, 1, n))
    return out.reshape(DEPTH * MOD_ROWS, MOD_PIECES, D_MODEL)


def _layer_block(shape, layer):
    zeros = (0,) * len(shape)
    return pl.BlockSpec((1,) + tuple(shape), lambda *_: (layer,) + zeros, pipeline_mode=pl.Buffered(1))


def _modulated_norm(x, g, shift, scale):
    return _rms(x, g) * (1.0 + scale) + shift


def _first_inproj_kernel(x_ref, ctx_ref, g_ref, mod_ref, w_ref, p_ref, xa_ref):
    is_ctx = pl.program_id(0) % TILES_PER_BATCH < CTX_TILES
    x = jnp.where(is_ctx, ctx_ref[...], x_ref[...])
    xa_ref[...] = x
    mod = mod_ref[0]
    z = _modulated_norm(x, g_ref[0], mod[0:1], mod[1:2])
    p_ref[...] = _dot(z.astype(BF16), w_ref[0]).astype(p_ref.dtype)


def _first_inproj(x, ctx, mod, norm_g, w):
    width = w.shape[-1]
    x_tiles = SEQ // ROW_TILE
    x_tile = lambda i: ((i // TILES_PER_BATCH) * x_tiles + jnp.maximum(i % TILES_PER_BATCH - CTX_TILES, 0), 0)
    ctx_tile = lambda i: ((i // TILES_PER_BATCH) * CTX_TILES + jnp.minimum(i % TILES_PER_BATCH, CTX_TILES - 1), 0)
    ident = lambda i: i
    row_out = lambda wd: pl.BlockSpec((ROW_TILE, wd), lambda i: (i, 0))
    return pl.pallas_call(
        _first_inproj_kernel,
        out_shape=(jax.ShapeDtypeStruct((NTOK, width), BF16), jax.ShapeDtypeStruct((NTOK, D_MODEL), F32)),
        grid=(NTOK // ROW_TILE,),
        in_specs=[pl.BlockSpec((ROW_TILE, D_MODEL), x_tile),
                  pl.BlockSpec((ROW_TILE, D_MODEL), ctx_tile),
                  _layer_block((1, D_MODEL), 0),
                  _mod_spec(0, ident),
                  _layer_block((D_MODEL, width), 0)],
        out_specs=(row_out(width), row_out(D_MODEL)),
        compiler_params=_params(("parallel",)),
        name="first_inproj",
    )(x.reshape(BATCH * SEQ, D_MODEL), ctx.reshape(BATCH * CTX_LEN, D_MODEL), norm_g, mod, w)


def _tail_kernel(x_ref, ma_ref, mb_ref, mod_ref, g_ref, wmix_ref, win_ref, wout_ref, *rest, has_next):
    mod = mod_ref[0]
    m = jnp.concatenate([ma_ref[...], mb_ref[...]], axis=1)
    x1 = x_ref[...] + mod[2:3] * _dot(m, wmix_ref[0])
    h = _modulated_norm(x1, g_ref[0], mod[3:4], mod[4:5]).astype(BF16)
    gu = _dot(h, win_ref[0])
    gate = gu[:, :FFN_HIDDEN]
    act = (gate * jax.nn.sigmoid(gate) * gu[:, FFN_HIDDEN:]).astype(BF16)
    x2 = x1 + mod[5:6] * _dot(act, wout_ref[0])
    if has_next:
        next_mod_ref, gn_ref, wn_ref, o_ref, p_ref = rest
        next_mod = next_mod_ref[0]
        z = _modulated_norm(x2, gn_ref[0], next_mod[0:1], next_mod[1:2])
        p_ref[...] = _dot(z.astype(BF16), wn_ref[0]).astype(p_ref.dtype)
    else:
        (o_ref,) = rest
    o_ref[...] = x2


def _layer_tail(xa, ma, mb, mod, layer, norm_ffn_g, wmix, win, wout, nxt):
    if nxt is None:
        x_tiles = SEQ // ROW_TILE
        n_steps = BATCH * x_tiles
        tile = lambda i: (i // x_tiles) * TILES_PER_BATCH + CTX_TILES + i % x_tiles
        out_rows = BATCH * SEQ
    else:
        n_steps = NTOK // ROW_TILE
        tile = lambda i: i
        out_rows = NTOK
    row_in = lambda width: pl.BlockSpec((ROW_TILE, width), lambda i: (tile(i), 0))
    row_out = lambda width: pl.BlockSpec((ROW_TILE, width), lambda i: (i, 0))
    in_specs = [row_in(D_MODEL), row_in(ma.shape[1]), row_in(mb.shape[1]),
                _mod_spec(layer, tile),
                _layer_block((1, D_MODEL), layer),
                _layer_block((D_MODEL, D_MODEL), layer),
                _layer_block((D_MODEL, 2 * FFN_HIDDEN), layer),
                _layer_block((FFN_HIDDEN, D_MODEL), layer)]
    args = [xa, ma, mb, mod, norm_ffn_g, wmix, win, wout]
    out_shape = [jax.ShapeDtypeStruct((out_rows, D_MODEL), F32)]
    out_specs = [row_out(D_MODEL)]
    if nxt is not None:
        norm_mix_g, w_next, idx = nxt
        width = w_next.shape[-1]
        in_specs += [_mod_spec(layer + 1, tile),
                     _layer_block((1, D_MODEL), layer + 1), _layer_block((D_MODEL, width), idx)]
        args += [mod, norm_mix_g, w_next]
        out_shape.append(jax.ShapeDtypeStruct((out_rows, width), BF16))
        out_specs.append(row_out(width))
    return pl.pallas_call(
        functools.partial(_tail_kernel, has_next=nxt is not None),
        out_shape=tuple(out_shape),
        grid=(n_steps,),
        in_specs=in_specs,
        out_specs=tuple(out_specs),
        compiler_params=_params(("parallel",)),
        name="layer_tail",
    )(*args)


GLA_QK_W = GLA_HEADS * GLA_DK
GLA_V_W = GLA_HEADS * GLA_DV
GLA_MAIN_W = 2 * GLA_QK_W + 2 * GLA_V_W
GLA_NCHUNK = TOK // GLA_CHUNK


GLA_BLOCK = 256
GLA_NBLOCK = TOK // GLA_BLOCK
GLA_CTX_BLOCKS = CTX_LEN // GLA_BLOCK
GLA_UNROLL = 3


def _log_sigmoid(y):
    return jnp.minimum(y, 0.0) - jnp.log(1.0 + jnp.exp(-jnp.abs(y)))


def _gla_kernel(main_ref, gate_ref, waf_ref, baf_ref, wab_ref, bab_ref, og_ref, o_ref,
                of_scr, qd_scr, kd_scr, qe_scr, kl_scr, dec_scr, s_scr):
    C, BLK = GLA_CHUNK, GLA_BLOCK
    cpb = BLK // C
    rb = lax.broadcasted_iota(jnp.int32, (BLK, BLK), 0)
    cb = lax.broadcasted_iota(jnp.int32, (BLK, BLK), 1)
    same_chunk = (rb // C) == (cb // C)
    tri_f = jnp.where(same_chunk & (rb >= cb), 1.0, 0.0).astype(BF16)
    tri_b = jnp.where(same_chunk & (rb <= cb), 1.0, 0.0).astype(BF16)
    r = lax.broadcasted_iota(jnp.int32, (C, C), 0)
    cc = lax.broadcasted_iota(jnp.int32, (C, C), 1)
    lower, upper = r >= cc, r <= cc
    sr = lax.broadcasted_iota(jnp.int32, (BLK, cpb * LANE), 0)
    sc = lax.broadcasted_iota(jnp.int32, (BLK, cpb * LANE), 1)
    chunk_sum = jnp.where(sr // C == sc // LANE, 1.0, 0.0).astype(BF16)

    def split(x):
        hi = x.astype(BF16)
        return hi, (x - hi.astype(F32)).astype(BF16)

    def dot_01(m, x):
        hi, lo = split(x)
        return _dot(jnp.concatenate([m, m], axis=1), jnp.concatenate([hi, lo], axis=0))

    def dot_10(x, m):
        hi, lo = split(x)
        return _dot(jnp.concatenate([hi, lo], axis=1), jnp.concatenate([m, m], axis=0))

    def block_rows(blk):
        return pl.ds(pl.multiple_of(blk * BLK, BLK), BLK)

    def chunk_rows(blk, j):
        return pl.ds(pl.multiple_of(blk * BLK + j * C, C), C)

    def dec_rows(blk, j):
        return pl.ds(pl.multiple_of((blk * cpb + j) * GLA_QK_W, GLA_QK_W), GLA_QK_W)

    def decay_terms(a, q, k, w_ref, bias_ref, tri, end_row):
        g = _log_sigmoid(_dot(a, w_ref[...]) + bias_ref[...]) * (1.0 / GLA_GATE_NORM)
        b = dot_01(tri, g)
        dec_all = jnp.exp(dot_10(g.T, chunk_sum))
        terms = []
        for j in range(cpb):
            sl = slice(j * C, (j + 1) * C)
            bc = b[sl]
            b_mid = bc[C // 2:C // 2 + 1]
            b_end = bc[end_row:end_row + 1]
            d = bc - b_mid
            qd = q[sl] * jnp.exp(d)
            kd = k[sl] * jnp.exp(-d)
            qe = qd * jnp.exp(b_mid)
            kl = kd * jnp.exp(b_end - b_mid)
            dec = dec_all[:, j * LANE:(j + 1) * LANE]
            terms.append((qd.astype(BF16), kd.T.astype(BF16), qe.astype(BF16), kl.T.astype(BF16), dec))
        return terms

    def step(qd, kd_t, qe, kl_t, dec, v, keep):
        outs = []
        for h in range(GLA_HEADS):
            ks = slice(h * GLA_DK, (h + 1) * GLA_DK)
            vh = v[:, h * GLA_DV:(h + 1) * GLA_DV]
            s_h = s_scr[ks, :]
            att = jnp.where(keep, _dot(qd[:, ks], kd_t[ks, :]), 0.0).astype(BF16)
            outs.append(_dot(att, vh) + _dot(qe[:, ks], s_h.astype(BF16)))
            s_scr[ks, :] = s_h * dec[ks, :] + _dot(kl_t[ks, :], vh)
        return outs

    s_scr[...] = jnp.zeros(s_scr.shape, F32)

    def loop1(blk, carry):
        rows = block_rows(blk)
        a = gate_ref[rows, :]
        q = main_ref[rows, 0:GLA_QK_W].astype(F32) * (GLA_DK ** -0.5)
        k = main_ref[rows, GLA_QK_W:2 * GLA_QK_W].astype(F32)
        v = main_ref[rows, 2 * GLA_QK_W:2 * GLA_QK_W + GLA_V_W]
        fwd = decay_terms(a, q, k, waf_ref, baf_ref, tri_f, C - 1)
        bwd = decay_terms(a, q, k, wab_ref, bab_ref, tri_b, 0)
        for j in range(cpb):
            cr = chunk_rows(blk, j)
            qd, kd, qe, kl, dec = bwd[j]
            qd_scr[cr, :] = qd
            kd_scr[dec_rows(blk, j), :] = kd
            qe_scr[cr, :] = qe
            kl_scr[dec_rows(blk, j), :] = kl
            dec_scr[dec_rows(blk, j), :] = dec
            qd, kd, qe, kl, dec = fwd[j]
            outs = step(qd, kd, qe, kl, dec, v[j * C:(j + 1) * C], lower)
            of_scr[cr, :] = jnp.concatenate(outs, axis=1)
        return carry

    lax.fori_loop(0, GLA_NBLOCK, loop1, 0, unroll=GLA_UNROLL)

    s_scr[...] = jnp.zeros(s_scr.shape, F32)

    def loop2(i, carry):
        blk = jnp.where(i < GLA_CTX_BLOCKS, GLA_CTX_BLOCKS - 1 - i, GLA_NBLOCK - 1 - (i - GLA_CTX_BLOCKS))
        for j in reversed(range(cpb)):
            cr = chunk_rows(blk, j)
            v = main_ref[cr, 2 * GLA_QK_W:2 * GLA_QK_W + GLA_V_W]
            kr = dec_rows(blk, j)
            outs = step(qd_scr[cr, :], kd_scr[kr, :], qe_scr[cr, :], kl_scr[kr, :], dec_scr[kr, :], v, upper)
            gout = main_ref[cr, 2 * GLA_QK_W + GLA_V_W:GLA_MAIN_W].astype(F32)
            res = []
            for h in range(GLA_HEADS):
                vs = slice(h * GLA_DV, (h + 1) * GLA_DV)
                o = outs[h] + of_scr[cr, vs]
                gh = gout[:, vs]
                res.append(_rms(o, og_ref[...]) * (gh * jax.nn.sigmoid(gh)))
            o_ref[cr, :] = jnp.concatenate(res, axis=1).astype(o_ref.dtype)
        return carry

    lax.fori_loop(0, GLA_NBLOCK, loop2, 0, unroll=GLA_UNROLL)


def _gla(p, waf, baf, wab, bab, onorm_g):
    gate_col = (EV_W - LANE) // LANE
    qk_scr = pltpu.VMEM((TOK, GLA_QK_W), BF16)
    qk_t_scr = pltpu.VMEM((GLA_NCHUNK * GLA_QK_W, GLA_CHUNK), BF16)
    return pl.pallas_call(
        _gla_kernel,
        out_shape=jax.ShapeDtypeStruct((NTOK, GLA_V_W), BF16),
        grid=(BATCH,),
        in_specs=[pl.BlockSpec((TOK, GLA_MAIN_W), lambda b: (b, 0)),
                  pl.BlockSpec((TOK, LANE), lambda b: (b, gate_col)),
                  _resident((LANE, GLA_QK_W)), _resident((1, GLA_QK_W)),
                  _resident((LANE, GLA_QK_W)), _resident((1, GLA_QK_W)),
                  _resident((1, GLA_DV))],
        out_specs=pl.BlockSpec((TOK, GLA_V_W), lambda b: (b, 0)),
        scratch_shapes=[pltpu.VMEM((TOK, GLA_V_W), F32), qk_scr, qk_t_scr, qk_scr, qk_t_scr,
                        pltpu.VMEM((GLA_NCHUNK * GLA_QK_W, LANE), F32),
                        pltpu.VMEM((GLA_QK_W, GLA_DV), F32)],
        compiler_params=_params(("parallel",)),
        name="gla",
    )(p, p, waf, baf, wab, bab, onorm_g.reshape(1, GLA_DV))


def _gelu(x):
    return 0.5 * x * (1.0 + lax.erf(x * (2.0 ** -0.5)))


SG_TILE = 768


def _sg_kernel(u_ref, v_ref, vg_ref, ws_ref, bs_ref, o_ref):
    for c in range(SG_TILE // SG_CHUNK):
        rows = slice(c * SG_CHUNK, (c + 1) * SG_CHUNK)
        for g in range(SG_GROUPS):
            cols = slice(g * SG_DIM, (g + 1) * SG_DIM)
            u = _gelu(u_ref[rows, cols].astype(F32))
            vn = _lane_rms(_gelu(v_ref[rows, cols].astype(F32)), vg_ref[g:g + 1, :])
            mixed = _dot(ws_ref[g], vn.astype(BF16)) + bs_ref[g]
            o_ref[rows, cols] = (u * mixed).astype(o_ref.dtype)


def _spatial_gating(p, vnorm_g, ws, bs):
    width = SG_GROUPS * SG_DIM
    u_col = GLA_MAIN_W // width
    bias = jnp.broadcast_to(bs[:, :, None], (SG_GROUPS, SG_CHUNK, SG_DIM))
    return pl.pallas_call(
        _sg_kernel,
        out_shape=jax.ShapeDtypeStruct((NTOK, width), BF16),
        grid=(NTOK // SG_TILE,),
        in_specs=[pl.BlockSpec((SG_TILE, width), lambda i: (i, u_col)),
                  pl.BlockSpec((SG_TILE, width), lambda i: (i, u_col + 1)),
                  _resident((SG_GROUPS, SG_DIM)),
                  _resident((SG_GROUPS, SG_CHUNK, SG_CHUNK)),
                  _resident((SG_GROUPS, SG_CHUNK, SG_DIM))],
        out_specs=pl.BlockSpec((SG_TILE, width), lambda i: (i, 0)),
        compiler_params=_params(("parallel",)),
        name="spatial_gating",
    )(p, p, vnorm_g, ws, bias)


FT_W = FT_GROUPS * FT_DIM


def _dft_cos_sin(n):
    jk = np.outer(np.arange(n), np.arange(n)) % n
    ang = 2.0 * np.pi * jk.astype(np.float64) / n
    return np.cos(ang), np.sin(ang)


def _parity_dft(n):
    p = np.arange(n // 2)[:, None]
    mats = []
    for first in (0, 1):
        t = 2 * np.arange(n // 2)[None, :] + first
        ang = 2.0 * np.pi * ((p * t) % n).astype(np.float64) / n
        mats.append(np.concatenate([np.cos(ang), -np.sin(ang)], axis=1))
    return jnp.asarray(np.stack(mats), F32).astype(BF16)


def _fourier_kernel(h_ref, cs_ref, dft_x_ref, dft_c_ref, o_ref, ab_scr, *, need_ctx):
    ab = _dot(h_ref[...], cs_ref[...])
    n_slab = 2 * FT_W // LANE
    for c in range(n_slab):
        ab_scr[c] = ab[:, c * LANE:(c + 1) * LANE]

    def position_dft(first_row, length, dft_ref):
        half = length // 2
        parts = []
        for parity in (0, 1):
            slabs = [ab_scr[c, pl.ds(first_row + parity, half, stride=2), :].astype(BF16) for c in range(n_slab)]
            a = jnp.concatenate(slabs[:n_slab // 2], axis=1)
            b = jnp.concatenate(slabs[n_slab // 2:], axis=1)
            parts.append(_dot(dft_ref[parity], jnp.concatenate([a, b], axis=0)))
        scale = (length * FT_DIM) ** -0.5
        o_ref[first_row:first_row + half, :] = ((parts[0] + parts[1]) * scale).astype(o_ref.dtype)
        o_ref[first_row + half:first_row + length, :] = ((parts[0] - parts[1]) * scale).astype(o_ref.dtype)

    position_dft(CTX_LEN, SEQ, dft_x_ref)
    if need_ctx:
        position_dft(0, CTX_LEN, dft_c_ref)
    else:
        o_ref[:CTX_LEN, :] = jnp.zeros((CTX_LEN, FT_W), o_ref.dtype)


def _fourier(p, need_ctx):
    cc, sc = _dft_cos_sin(FT_DIM)
    eye = np.eye(FT_GROUPS)
    cs = jnp.asarray(np.concatenate([np.kron(eye, cc), np.kron(eye, sc)], axis=1), F32).astype(BF16)
    return pl.pallas_call(
        functools.partial(_fourier_kernel, need_ctx=need_ctx),
        out_shape=jax.ShapeDtypeStruct((NTOK, FT_W), BF16),
        grid=(BATCH,),
        in_specs=[pl.BlockSpec((TOK, FT_W), lambda b: (b, 0)),
                  _resident((FT_W, 2 * FT_W)),
                  _resident((2, SEQ // 2, SEQ)), _resident((2, CTX_LEN // 2, CTX_LEN))],
        out_specs=pl.BlockSpec((TOK, FT_W), lambda b: (b, 0)),
        scratch_shapes=[pltpu.VMEM((2 * FT_W // LANE, TOK, LANE), F32)],
        compiler_params=_params(("parallel",)),
        name="fourier",
    )(p, cs, _parity_dft(SEQ), _parity_dft(CTX_LEN))


MLA_NOPE_W = MLA_HEADS * MLA_NOPE
MLA_ROPE_W = MLA_HEADS * MLA_ROPE
OD_QA0 = FT_W
OD_KVA0 = OD_QA0 + MLA_Q_RANK
OD_KPE0 = OD_KVA0 + MLA_KV_RANK
OD_KPE_ROT0 = OD_KPE0 + MLA_ROPE


def _rot_cols(w):
    q = MLA_ROPE // 4
    return jnp.concatenate([-w[..., q:2 * q], w[..., 0:q], -w[..., 3 * q:4 * q], w[..., 2 * q:3 * q]], axis=-1)


def _rot_gain(g):
    q = MLA_ROPE // 4
    return jnp.concatenate([g[..., q:2 * q], g[..., 0:q], g[..., 3 * q:4 * q], g[..., 2 * q:3 * q]], axis=-1)


def _rope_tables():
    rows = SEQ // GRID_W
    row_id = np.repeat(np.arange(rows, dtype=np.float32), GRID_W)
    col_id = np.tile(np.arange(GRID_W, dtype=np.float32), rows)
    axis_dim = MLA_ROPE // 2
    inv_freq = (np.float32(ROPE_BASE) ** (-np.arange(0, axis_dim, 2, dtype=np.float32) / np.float32(axis_dim))).astype(np.float32)
    ang_r = row_id[:, None] * inv_freq
    ang_c = col_id[:, None] * inv_freq
    ang = np.concatenate([ang_r, ang_r, ang_c, ang_c], axis=-1).astype(np.float32)
    cos = np.concatenate([np.ones((CTX_LEN, MLA_ROPE), np.float32), np.cos(ang)], axis=0)
    sin = np.concatenate([np.zeros((CTX_LEN, MLA_ROPE), np.float32), np.sin(ang)], axis=0)
    return tuple(jnp.asarray(t, F32) for t in (cos, sin, cos.T, sin.T))


MLA_TILE = 768
MLA_VT_ROWS = MLA_V + 16


def _mla_prep_kernel(p_ref, cos_ref, sin_ref, cos_t_ref, sin_t_ref, qag_ref, wuq_t_ref, kvag_ref,
                     wukv_k_ref, wukv_vt_ref, head_sum_ref, kgain_ref,
                     qg_nope_ref, qg_rope_ref, qg_rot_ref, qt_ref, k_ref, vt_ref):
    qa = _lane_rms(p_ref[:, OD_QA0:OD_KVA0].astype(F32), qag_ref[...])
    q_t = _dot(wuq_t_ref[...], qa.T.astype(BF16))
    cos_t, sin_t = cos_t_ref[...], sin_t_ref[...]
    for h in range(MLA_HEADS):
        qn = q_t[h * MLA_NOPE:(h + 1) * MLA_NOPE]
        qr = q_t[MLA_NOPE_W + h * MLA_ROPE:MLA_NOPE_W + (h + 1) * MLA_ROPE]
        qrr = q_t[MLA_NOPE_W + MLA_ROPE_W + h * MLA_ROPE:MLA_NOPE_W + MLA_ROPE_W + (h + 1) * MLA_ROPE]
        ss = jnp.sum(qn * qn, axis=0, keepdims=True) + jnp.sum(qr * qr, axis=0, keepdims=True)
        inv = lax.rsqrt(ss * (1.0 / MLA_QK) + RMS_EPS) * (MLA_QK ** -0.5)
        q_rope = qr * qg_rope_ref[...] * cos_t + qrr * qg_rot_ref[...] * sin_t
        qt_ref[h] = jnp.concatenate([qn * qg_nope_ref[...] * inv, q_rope * inv], axis=0).astype(qt_ref.dtype)

    kva = _lane_rms(p_ref[:, OD_KVA0:OD_KPE0].astype(F32), kvag_ref[...])
    v_t = _dot(wukv_vt_ref[...], kva.T.astype(BF16))
    ones = jnp.ones((MLA_VT_ROWS - MLA_V, v_t.shape[1]), F32)
    for h in range(MLA_HEADS):
        vt_ref[h] = jnp.concatenate([v_t[h * MLA_V:(h + 1) * MLA_V], ones], axis=0).astype(vt_ref.dtype)
    kn = _dot(kva.astype(BF16), wukv_k_ref[...])
    kpe = p_ref[:, OD_KPE0:OD_KPE_ROT0].astype(F32)
    kpe_rot = p_ref[:, OD_KPE_ROT0:OD_W].astype(F32)
    sq = jnp.concatenate([kn * kn, kpe * kpe], axis=1).astype(BF16)
    ss = _dot(sq, head_sum_ref[...])
    inv = lax.rsqrt(ss * (1.0 / MLA_QK) + RMS_EPS)
    kg = kgain_ref[...]
    k_rope = (kpe * kg[:, MLA_NOPE:MLA_QK] * cos_ref[...]
              + kpe_rot * kg[:, MLA_QK:] * sin_ref[...])
    for h in range(MLA_HEADS):
        k_h = jnp.concatenate([kn[:, h * MLA_NOPE:(h + 1) * MLA_NOPE] * kg[:, :MLA_NOPE], k_rope], axis=1)
        k_ref[h] = (k_h * inv[:, h:h + 1]).astype(k_ref.dtype)


def _key_head_sum_matrix():
    head = np.arange(LANE)[None, :]
    nope = np.repeat(np.arange(MLA_HEADS), MLA_NOPE)[:, None] == head
    rope = np.broadcast_to(head < MLA_HEADS, (MLA_ROPE, LANE))
    return jnp.asarray(np.concatenate([nope, rope], axis=0), BF16)


def _mla_prep(p, rope_tabs, qa_g, wuq, kva_g, wukv, qn_g, kn_g):
    cos, sin, cos_t, sin_t = rope_tabs
    head_sum = _key_head_sum_matrix()
    wukv = _wukv_cols(wukv)
    wukv_k, wukv_vt = wukv[:, :MLA_NOPE_W], wukv[:, MLA_NOPE_W:].T
    wuq_t = _wuq_cols(wuq).T
    kgain = jnp.concatenate([kn_g, _rot_gain(kn_g[MLA_NOPE:])]).reshape(1, -1)
    col = lambda g: jnp.broadcast_to(g[:, None], (g.shape[0], MLA_TILE))
    tiles_per_batch = TOK // MLA_TILE
    rows = lambda d: pl.BlockSpec((MLA_HEADS, MLA_TILE, d), lambda i: (0, i, 0))
    cols = lambda d: pl.BlockSpec((MLA_HEADS, d, MLA_TILE), lambda i: (0, 0, i))
    rope_spec = pl.BlockSpec((MLA_TILE, MLA_ROPE), lambda i: (i % tiles_per_batch, 0))
    rope_t_spec = pl.BlockSpec((MLA_ROPE, MLA_TILE), lambda i: (0, i % tiles_per_batch))
    return pl.pallas_call(
        _mla_prep_kernel,
        out_shape=(jax.ShapeDtypeStruct((MLA_HEADS, MLA_QK, NTOK), BF16),
                   jax.ShapeDtypeStruct((MLA_HEADS, NTOK, MLA_QK), BF16),
                   jax.ShapeDtypeStruct((MLA_HEADS, MLA_VT_ROWS, NTOK), BF16)),
        grid=(NTOK // MLA_TILE,),
        in_specs=[pl.BlockSpec((MLA_TILE, OD_W), lambda i: (i, 0)),
                  rope_spec, rope_spec, rope_t_spec, rope_t_spec,
                  _resident((1, MLA_Q_RANK)), _resident(wuq_t.shape),
                  _resident((1, MLA_KV_RANK)), _resident(wukv_k.shape), _resident(wukv_vt.shape),
                  _resident(head_sum.shape), _resident(kgain.shape),
                  _resident((MLA_NOPE, MLA_TILE)), _resident((MLA_ROPE, MLA_TILE)), _resident((MLA_ROPE, MLA_TILE))],
        out_specs=(cols(MLA_QK), rows(MLA_QK), cols(MLA_VT_ROWS)),
        compiler_params=_params(("parallel",)),
        name="mla_prep",
    )(p, cos, sin, cos_t, sin_t, qa_g.reshape(1, -1), wuq_t, kva_g.reshape(1, -1), wukv_k, wukv_vt,
      head_sum, kgain,
      col(qn_g[:MLA_NOPE]), col(qn_g[MLA_NOPE:]), col(_rot_gain(qn_g[MLA_NOPE:])))


ATT_Q_TILE = 512


ATT_KEY_CHUNK = TOK


def _weighted_values(acc):
    return (acc[:MLA_V] * (1.0 / acc[MLA_V:MLA_V + 1])).T


def _attn_kernel(qt_ref, k_ref, vt_ref, o_ref, s_scr, *, need_ctx):
    n_q = SEQ // ATT_Q_TILE
    chunks = [slice(c * ATT_KEY_CHUNK, (c + 1) * ATT_KEY_CHUNK) for c in range(TOK // ATT_KEY_CHUNK)]
    q_cols = lambda j: slice(CTX_LEN + j * ATT_Q_TILE, CTX_LEN + (j + 1) * ATT_Q_TILE)

    def scores(j, rows, col_max):
        s = _dot(k_ref[0, rows, :], qt_ref[0, :, q_cols(j)])
        s_scr[j % 2, rows, :] = s
        m = jnp.max(s, axis=0, keepdims=True)
        return m if col_max is None else jnp.maximum(col_max, m)

    col_max = None
    for rows in chunks:
        col_max = scores(0, rows, col_max)
    for j in range(n_q):
        acc, next_max = None, None
        for rows in chunks:
            if j + 1 < n_q:
                next_max = scores(j + 1, rows, next_max)
            p = jnp.exp(s_scr[j % 2, rows, :] - col_max).astype(BF16)
            part = _dot(vt_ref[0, :, rows], p)
            acc = part if acc is None else acc + part
        o_ref[q_cols(j), :] = _weighted_values(acc).astype(o_ref.dtype)
        col_max = next_max

    if need_ctx:
        s = _dot(k_ref[0, :CTX_LEN, :], qt_ref[0, :, :CTX_LEN])
        p = jnp.exp(s - jnp.max(s, axis=0, keepdims=True)).astype(BF16)
        o_ref[:CTX_LEN, :] = _weighted_values(_dot(vt_ref[0, :, :CTX_LEN], p)).astype(o_ref.dtype)
    else:
        o_ref[:CTX_LEN, :] = jnp.zeros((CTX_LEN, MLA_V), o_ref.dtype)


def _attention(q_t, k, v_t, need_ctx):
    transposed = lambda d: pl.BlockSpec((1, d, TOK), lambda b, h: (h, 0, b))
    return pl.pallas_call(
        functools.partial(_attn_kernel, need_ctx=need_ctx),
        out_shape=jax.ShapeDtypeStruct((NTOK, MLA_HEADS * MLA_V), BF16),
        grid=(BATCH, MLA_HEADS),
        in_specs=[transposed(MLA_QK),
                  pl.BlockSpec((1, TOK, MLA_QK), lambda b, h: (h, b, 0)),
                  transposed(MLA_VT_ROWS)],
        out_specs=pl.BlockSpec((TOK, MLA_V), lambda b, h: (b, h)),
        scratch_shapes=[pltpu.VMEM((2, TOK, ATT_Q_TILE), F32)],
        compiler_params=_params(("parallel", "parallel")),
        name="attention",
    )(q_t, k, v_t)


def _even_w_in(w):
    gates0 = GLA_MAIN_W
    gates1 = gates0 + 2 * GLA_GATE_RANK
    pad = jnp.zeros(w.shape[:-1] + (EV_W - w.shape[-1],), w.dtype)
    return jnp.concatenate([w[..., :gates0], w[..., gates1:], w[..., gates0:gates1], pad], axis=-1).astype(BF16)


def _gate_w(wa, first_row):
    out = jnp.zeros((LANE, GLA_QK_W), wa.dtype)
    return lax.dynamic_update_slice(out, wa, (first_row, 0)).astype(BF16)


def _odd_w_in(w):
    return jnp.concatenate([w, _rot_cols(w[..., OD_KPE0:OD_KPE_ROT0])], axis=-1).astype(BF16)


def _wuq_cols(w):
    w = w.reshape(MLA_Q_RANK, MLA_HEADS, MLA_QK)
    nope = w[:, :, :MLA_NOPE].reshape(MLA_Q_RANK, MLA_NOPE_W)
    rope = w[:, :, MLA_NOPE:]
    return jnp.concatenate([nope, rope.reshape(MLA_Q_RANK, MLA_ROPE_W),
                            _rot_cols(rope).reshape(MLA_Q_RANK, MLA_ROPE_W)], axis=1).astype(BF16)


def _wukv_cols(w):
    w = w.reshape(MLA_KV_RANK, MLA_HEADS, MLA_NOPE + MLA_V)
    return jnp.concatenate([w[:, :, :MLA_NOPE].reshape(MLA_KV_RANK, MLA_NOPE_W),
                            w[:, :, MLA_NOPE:].reshape(MLA_KV_RANK, MLA_HEADS * MLA_V)], axis=1).astype(BF16)


def kernel(x, c, ctx, c_ctx, ada_w, ada_b, norm_mix_g, norm_ffn_g, w_mix_out, ffn_w_in, ffn_w_out, ev_w_in, gla_wa_f, gla_ba_f, gla_wa_b, gla_ba_b, gla_onorm_g, sg_vnorm_g, sg_ws, sg_bs, od_w_in, mla_qa_g, mla_wuq, mla_kva_g, mla_wukv, mla_qn_g, mla_kn_g):
    cvec = jnp.zeros((MOD_ROWS, D_MODEL), F32).at[:BATCH].set(c).at[CTX_MOD_ROW].set(c_ctx)
    mod = _ada_table(cvec, ada_w, ada_b)
    rope_tabs = _rope_tables()
    norm_mix_g = norm_mix_g.reshape(DEPTH, 1, D_MODEL)
    norm_ffn_g = norm_ffn_g.reshape(DEPTH, 1, D_MODEL)
    wmix, win, wout = w_mix_out.astype(BF16), ffn_w_in.astype(BF16), ffn_w_out.astype(BF16)
    w_in_stacks = (_even_w_in(ev_w_in), _odd_w_in(od_w_in))

    p, xa = _first_inproj(x, ctx, mod, norm_mix_g, w_in_stacks[0])
    for l in range(DEPTH):
        last = l == DEPTH - 1
        i = l // 2
        if l % 2 == 0:
            ma = _gla(p, _gate_w(gla_wa_f[i], 0), gla_ba_f[i].reshape(1, GLA_QK_W),
                      _gate_w(gla_wa_b[i], GLA_GATE_RANK), gla_ba_b[i].reshape(1, GLA_QK_W), gla_onorm_g[i])
            mb = _spatial_gating(p, sg_vnorm_g[i], sg_ws[i].astype(BF16), sg_bs[i])
        else:
            q, k, v = _mla_prep(p, rope_tabs, mla_qa_g[i], mla_wuq[i], mla_kva_g[i], mla_wukv[i],
                                mla_qn_g[i], mla_kn_g[i])
            ma = _fourier(p, not last)
            mb = _attention(q, k, v, not last)
        nxt = None if last else (norm_mix_g, w_in_stacks[(l + 1) % 2], (l + 1) // 2)
        res = _layer_tail(xa, ma, mb, mod, l, norm_ffn_g, wmix, win, wout, nxt)
        xa, p = (res[0], None) if last else res
    return xa.reshape(BATCH, SEQ, D_MODEL)
```

```python
import functools

import numpy as np
import jax
import jax.numpy as jnp
from jax import lax
from jax.experimental import pallas as pl
from jax.experimental.pallas import tpu as pltpu

D_MODEL = 1024
BATCH = 8
SEQ = 2048
DEPTH = 4
CTX_LEN = 256
GRID_W = 64
RMS_EPS = 1e-6
GLA_HEADS = 4
GLA_DK = 64
GLA_DV = 128
GLA_GATE_RANK = 16
GLA_GATE_NORM = 16.0
GLA_CHUNK = 64
SG_GROUPS = 4
SG_DIM = 128
SG_CHUNK = 128
FT_GROUPS = 4
FT_DIM = 64
MLA_HEADS = 6
MLA_NOPE = 128
MLA_ROPE = 64
MLA_V = 128
MLA_Q_RANK = 384
MLA_KV_RANK = 256
ROPE_BASE = 10000.0
FFN_HIDDEN = 2816

TOK = CTX_LEN + SEQ
NTOK = BATCH * TOK
ROW_TILE = 256
TILES_PER_BATCH = TOK // ROW_TILE
CTX_TILES = CTX_LEN // ROW_TILE
MOD_ROWS = 16
CTX_MOD_ROW = BATCH
LANE = 128
EV_W = 2688
OD_W = 1024
MLA_QK = MLA_NOPE + MLA_ROPE
VMEM_LIMIT = 56 * 1024 * 1024

F32 = jnp.float32
BF16 = jnp.bfloat16


def _dot(a, b):
    return jnp.dot(a, b, preferred_element_type=F32)


def _rms(x, g):
    return x * lax.rsqrt(jnp.mean(x * x, axis=-1, keepdims=True) + RMS_EPS) * g


def _lane_rms(x, g):
    width = x.shape[1]
    ss = _dot((x * x).astype(BF16), jnp.ones((width, LANE), BF16))
    inv = lax.rsqrt(ss * (1.0 / width) + RMS_EPS)
    return x * jnp.concatenate([inv] * (width // LANE), axis=1) * g


def _params(sem):
    return pltpu.CompilerParams(dimension_semantics=sem, vmem_limit_bytes=VMEM_LIMIT)


def _resident(shape):
    zeros = (0,) * len(shape)
    return pl.BlockSpec(shape, lambda *_: zeros, pipeline_mode=pl.Buffered(1))


def _mod_row(i):
    return jnp.where(i % TILES_PER_BATCH < CTX_TILES, CTX_MOD_ROW, i // TILES_PER_BATCH)


MOD_PIECES = 6


def _mod_spec(layer, tile_of_step):
    def index_map(i):
        return (layer * MOD_ROWS + _mod_row(tile_of_step(i)), 0, 0)
    return pl.BlockSpec((1, MOD_PIECES, D_MODEL), index_map)


def _ada_kernel(c_ref, w_ref, b_ref, o_ref):
    c = c_ref[...]
    s = c * jax.nn.sigmoid(c)
    w = w_ref[0]
    w_hi = w.astype(BF16)
    w_lo = (w - w_hi.astype(F32)).astype(BF16)
    s_hi = s.astype(BF16)
    s_lo = (s - s_hi.astype(F32)).astype(BF16)
    o_ref[0] = _dot(s_hi, w_hi) + (_dot(s_hi, w_lo) + _dot(s_lo, w_hi)) + b_ref[0]


def _ada_table(cvec, ada_w, ada_b):
    tn = 3072
    n = MOD_PIECES * D_MODEL
    out = pl.pallas_call(
        _ada_kernel,
        out_shape=jax.ShapeDtypeStruct((DEPTH, MOD_ROWS, n), F32),
        grid=(DEPTH, n // tn),
        in_specs=[pl.BlockSpec((MOD_ROWS, D_MODEL), lambda l, j: (0, 0)),
                  pl.BlockSpec((1, D_MODEL, tn), lambda l, j: (l, 0, j)),
                  pl.BlockSpec((1, 1, tn), lambda l, j: (l, 0, j))],
        out_specs=pl.BlockSpec((1, MOD_ROWS, tn), lambda l, j: (l, 0, j)),
        compiler_params=_params(("arbitrary", "arbitrary")),
        name="ada_table",
    )(cvec, ada_w, ada_b.reshape(DEPTH, 1, n))
    return out.reshape(DEPTH * MOD_ROWS, MOD_PIECES, D_MODEL)


def _layer_block(shape, layer):
    zeros = (0,) * len(shape)
    return pl.BlockSpec((1,) + tuple(shape), lambda *_: (layer,) + zeros, pipeline_mode=pl.Buffered(1))


DENSE_VIEW_ROWS = D_MODEL


def _copy_plumbing(stacks, layer, n_steps):
    n_slices = 1 << (n_steps.bit_length() - 1)
    rows = DENSE_VIEW_ROWS // n_slices
    views, in_specs, shapes, out_specs = [], [], [], []
    for w in stacks:
        cols = w.shape[1] * w.shape[2] // DENSE_VIEW_ROWS
        views.append(w.reshape(w.shape[0] * DENSE_VIEW_ROWS, cols))
        in_specs.append(pl.BlockSpec((rows, cols), lambda i: (layer * n_slices + jnp.minimum(i, n_slices - 1), 0)))
        shapes.append(jax.ShapeDtypeStruct((DENSE_VIEW_ROWS, cols), BF16))
        out_specs.append(pl.BlockSpec((rows, cols), lambda i: (jnp.minimum(i, n_slices - 1), 0)))
    return views, in_specs, shapes, out_specs


def _as_layer_stacks(copies, stacks):
    return [c.reshape(1, w.shape[1], w.shape[2]) for c, w in zip(copies, stacks)]


def _modulated_norm(x, g, shift, scale):
    return _rms(x, g) * (1.0 + scale) + shift


def _first_inproj_kernel(x_ref, ctx_ref, g_ref, mod_ref, w_ref, p_ref, xa_ref):
    is_ctx = pl.program_id(0) % TILES_PER_BATCH < CTX_TILES
    x = jnp.where(is_ctx, ctx_ref[...], x_ref[...])
    xa_ref[...] = x
    mod = mod_ref[0]
    z = _modulated_norm(x, g_ref[0], mod[0:1], mod[1:2])
    p_ref[...] = _dot(z.astype(BF16), w_ref[0]).astype(p_ref.dtype)


def _first_inproj(x, ctx, mod, norm_g, w):
    width = w.shape[-1]
    x_tiles = SEQ // ROW_TILE
    x_tile = lambda i: ((i // TILES_PER_BATCH) * x_tiles + jnp.maximum(i % TILES_PER_BATCH - CTX_TILES, 0), 0)
    ctx_tile = lambda i: ((i // TILES_PER_BATCH) * CTX_TILES + jnp.minimum(i % TILES_PER_BATCH, CTX_TILES - 1), 0)
    ident = lambda i: i
    row_out = lambda wd: pl.BlockSpec((ROW_TILE, wd), lambda i: (i, 0))
    return pl.pallas_call(
        _first_inproj_kernel,
        out_shape=(jax.ShapeDtypeStruct((NTOK, width), BF16), jax.ShapeDtypeStruct((NTOK, D_MODEL), F32)),
        grid=(NTOK // ROW_TILE,),
        in_specs=[pl.BlockSpec((ROW_TILE, D_MODEL), x_tile),
                  pl.BlockSpec((ROW_TILE, D_MODEL), ctx_tile),
                  _layer_block((1, D_MODEL), 0),
                  _mod_spec(0, ident),
                  _layer_block((D_MODEL, width), 0)],
        out_specs=(row_out(width), row_out(D_MODEL)),
        compiler_params=_params(("parallel",)),
        name="first_inproj",
    )(x.reshape(BATCH * SEQ, D_MODEL), ctx.reshape(BATCH * CTX_LEN, D_MODEL), norm_g, mod, w)


def _tail_kernel(x_ref, ma_ref, mb_ref, mod_ref, g_ref, wmix_ref, win_ref, wout_ref, *rest, has_next):
    mod = mod_ref[0]
    m = jnp.concatenate([ma_ref[...], mb_ref[...]], axis=1)
    x1 = x_ref[...] + mod[2:3] * _dot(m, wmix_ref[0])
    h = _modulated_norm(x1, g_ref[0], mod[3:4], mod[4:5]).astype(BF16)
    gu = _dot(h, win_ref[0])
    gate = gu[:, :FFN_HIDDEN]
    act = (gate * jax.nn.sigmoid(gate) * gu[:, FFN_HIDDEN:]).astype(BF16)
    x2 = x1 + mod[5:6] * _dot(act, wout_ref[0])
    if has_next:
        next_mod_ref, gn_ref, wn_ref, *copy_refs, o_ref, p_ref, mix_copy, in_copy, out_copy = rest
        next_mod = next_mod_ref[0]
        z = _modulated_norm(x2, gn_ref[0], next_mod[0:1], next_mod[1:2])
        p_ref[...] = _dot(z.astype(BF16), wn_ref[0]).astype(p_ref.dtype)
        for src, dst in zip(copy_refs, (mix_copy, in_copy, out_copy)):
            dst[...] = src[...].astype(BF16)
    else:
        (o_ref,) = rest
    o_ref[...] = x2


def _layer_tail(xa, ma, mb, mod, layer, norm_ffn_g, dense, nxt):
    wmix, win, wout = dense
    if nxt is None:
        x_tiles = SEQ // ROW_TILE
        n_steps = BATCH * x_tiles
        tile = lambda i: (i // x_tiles) * TILES_PER_BATCH + CTX_TILES + i % x_tiles
        out_rows = BATCH * SEQ
    else:
        n_steps = NTOK // ROW_TILE
        tile = lambda i: i
        out_rows = NTOK
    row_in = lambda width: pl.BlockSpec((ROW_TILE, width), lambda i: (tile(i), 0))
    row_out = lambda width: pl.BlockSpec((ROW_TILE, width), lambda i: (i, 0))
    in_specs = [row_in(D_MODEL), row_in(ma.shape[1]), row_in(mb.shape[1]),
                _mod_spec(layer, tile),
                _layer_block((1, D_MODEL), layer),
                _layer_block((D_MODEL, D_MODEL), 0),
                _layer_block((D_MODEL, 2 * FFN_HIDDEN), 0),
                _layer_block((FFN_HIDDEN, D_MODEL), 0)]
    args = [xa, ma, mb, mod, norm_ffn_g, wmix, win, wout]
    out_shape = [jax.ShapeDtypeStruct((out_rows, D_MODEL), F32)]
    out_specs = [row_out(D_MODEL)]
    if nxt is not None:
        norm_mix_g, w_next, idx, dense_f32 = nxt
        width = w_next.shape[-1]
        views, copy_in, copy_shapes, copy_out = _copy_plumbing(dense_f32, layer + 1, n_steps)
        in_specs += [_mod_spec(layer + 1, tile),
                     _layer_block((1, D_MODEL), layer + 1), _layer_block((D_MODEL, width), idx)] + copy_in
        args += [mod, norm_mix_g, w_next] + views
        out_shape += [jax.ShapeDtypeStruct((out_rows, width), BF16)] + copy_shapes
        out_specs += [row_out(width)] + copy_out
    res = pl.pallas_call(
        functools.partial(_tail_kernel, has_next=nxt is not None),
        out_shape=tuple(out_shape),
        grid=(n_steps,),
        in_specs=in_specs,
        out_specs=tuple(out_specs),
        compiler_params=_params(("arbitrary",)),
        name="layer_tail",
    )(*args)
    if nxt is None:
        return res[0], None, None
    return res[0], res[1], _as_layer_stacks(res[2:], dense_f32)


GLA_QK_W = GLA_HEADS * GLA_DK
GLA_V_W = GLA_HEADS * GLA_DV
GLA_MAIN_W = 2 * GLA_QK_W + 2 * GLA_V_W
GLA_NCHUNK = TOK // GLA_CHUNK


GLA_BLOCK = 256
GLA_NBLOCK = TOK // GLA_BLOCK
GLA_CTX_BLOCKS = CTX_LEN // GLA_BLOCK
GLA_UNROLL = 3


def _log_sigmoid(y):
    return jnp.minimum(y, 0.0) - jnp.log(1.0 + jnp.exp(-jnp.abs(y)))


def _gla_kernel(main_ref, gate_ref, waf_ref, baf_ref, wab_ref, bab_ref, og_ref, o_ref,
                of_scr, qd_scr, kd_scr, qe_scr, kl_scr, dec_scr, s_scr):
    C, BLK = GLA_CHUNK, GLA_BLOCK
    cpb = BLK // C
    rb = lax.broadcasted_iota(jnp.int32, (BLK, BLK), 0)
    cb = lax.broadcasted_iota(jnp.int32, (BLK, BLK), 1)
    same_chunk = (rb // C) == (cb // C)
    tri_f = jnp.where(same_chunk & (rb >= cb), 1.0, 0.0).astype(BF16)
    tri_b = jnp.where(same_chunk & (rb <= cb), 1.0, 0.0).astype(BF16)
    r = lax.broadcasted_iota(jnp.int32, (C, C), 0)
    cc = lax.broadcasted_iota(jnp.int32, (C, C), 1)
    lower, upper = r >= cc, r <= cc
    sr = lax.broadcasted_iota(jnp.int32, (BLK, cpb * LANE), 0)
    sc = lax.broadcasted_iota(jnp.int32, (BLK, cpb * LANE), 1)
    chunk_sum = jnp.where(sr // C == sc // LANE, 1.0, 0.0).astype(BF16)

    def split(x):
        hi = x.astype(BF16)
        return hi, (x - hi.astype(F32)).astype(BF16)

    def dot_01(m, x):
        hi, lo = split(x)
        return _dot(jnp.concatenate([m, m], axis=1), jnp.concatenate([hi, lo], axis=0))

    def dot_10(x, m):
        hi, lo = split(x)
        return _dot(jnp.concatenate([hi, lo], axis=1), jnp.concatenate([m, m], axis=0))

    def block_rows(blk):
        return pl.ds(pl.multiple_of(blk * BLK, BLK), BLK)

    def chunk_rows(blk, j):
        return pl.ds(pl.multiple_of(blk * BLK + j * C, C), C)

    def dec_rows(blk, j):
        return pl.ds(pl.multiple_of((blk * cpb + j) * GLA_QK_W, GLA_QK_W), GLA_QK_W)

    def decay_terms(a, q, k, w_ref, bias_ref, tri, end_row):
        g = _log_sigmoid(_dot(a, w_ref[...]) + bias_ref[...]) * (1.0 / GLA_GATE_NORM)
        b = dot_01(tri, g)
        dec_all = jnp.exp(dot_10(g.T, chunk_sum))
        terms = []
        for j in range(cpb):
            sl = slice(j * C, (j + 1) * C)
            bc = b[sl]
            b_mid = bc[C // 2:C // 2 + 1]
            b_end = bc[end_row:end_row + 1]
            d = bc - b_mid
            qd = q[sl] * jnp.exp(d)
            kd = k[sl] * jnp.exp(-d)
            qe = qd * jnp.exp(b_mid)
            kl = kd * jnp.exp(b_end - b_mid)
            dec = dec_all[:, j * LANE:(j + 1) * LANE]
            terms.append((qd.astype(BF16), kd.T.astype(BF16), qe.astype(BF16), kl.T.astype(BF16), dec))
        return terms

    def step(qd, kd_t, qe, kl_t, dec, v, keep):
        outs = []
        for h in range(GLA_HEADS):
            ks = slice(h * GLA_DK, (h + 1) * GLA_DK)
            vh = v[:, h * GLA_DV:(h + 1) * GLA_DV]
            s_h = s_scr[ks, :]
            att = jnp.where(keep, _dot(qd[:, ks], kd_t[ks, :]), 0.0).astype(BF16)
            outs.append(_dot(att, vh) + _dot(qe[:, ks], s_h.astype(BF16)))
            s_scr[ks, :] = s_h * dec[ks, :] + _dot(kl_t[ks, :], vh)
        return outs

    s_scr[...] = jnp.zeros(s_scr.shape, F32)

    def loop1(blk, carry):
        rows = block_rows(blk)
        a = gate_ref[rows, :]
        q = main_ref[rows, 0:GLA_QK_W].astype(F32) * (GLA_DK ** -0.5)
        k = main_ref[rows, GLA_QK_W:2 * GLA_QK_W].astype(F32)
        v = main_ref[rows, 2 * GLA_QK_W:2 * GLA_QK_W + GLA_V_W]
        fwd = decay_terms(a, q, k, waf_ref, baf_ref, tri_f, C - 1)
        bwd = decay_terms(a, q, k, wab_ref, bab_ref, tri_b, 0)
        for j in range(cpb):
            cr = chunk_rows(blk, j)
            qd, kd, qe, kl, dec = bwd[j]
            qd_scr[cr, :] = qd
            kd_scr[dec_rows(blk, j), :] = kd
            qe_scr[cr, :] = qe
            kl_scr[dec_rows(blk, j), :] = kl
            dec_scr[dec_rows(blk, j), :] = dec
            qd, kd, qe, kl, dec = fwd[j]
            outs = step(qd, kd, qe, kl, dec, v[j * C:(j + 1) * C], lower)
            of_scr[cr, :] = jnp.concatenate(outs, axis=1)
        return carry

    lax.fori_loop(0, GLA_NBLOCK, loop1, 0, unroll=GLA_UNROLL)

    s_scr[...] = jnp.zeros(s_scr.shape, F32)

    def loop2(i, carry):
        blk = jnp.where(i < GLA_CTX_BLOCKS, GLA_CTX_BLOCKS - 1 - i, GLA_NBLOCK - 1 - (i - GLA_CTX_BLOCKS))
        for j in reversed(range(cpb)):
            cr = chunk_rows(blk, j)
            v = main_ref[cr, 2 * GLA_QK_W:2 * GLA_QK_W + GLA_V_W]
            kr = dec_rows(blk, j)
            outs = step(qd_scr[cr, :], kd_scr[kr, :], qe_scr[cr, :], kl_scr[kr, :], dec_scr[kr, :], v, upper)
            gout = main_ref[cr, 2 * GLA_QK_W + GLA_V_W:GLA_MAIN_W].astype(F32)
            res = []
            for h in range(GLA_HEADS):
                vs = slice(h * GLA_DV, (h + 1) * GLA_DV)
                o = outs[h] + of_scr[cr, vs]
                gh = gout[:, vs]
                res.append(_rms(o, og_ref[...]) * (gh * jax.nn.sigmoid(gh)))
            o_ref[cr, :] = jnp.concatenate(res, axis=1).astype(o_ref.dtype)
        return carry

    lax.fori_loop(0, GLA_NBLOCK, loop2, 0, unroll=GLA_UNROLL)


def _gla(p, waf, baf, wab, bab, onorm_g):
    gate_col = (EV_W - LANE) // LANE
    qk_scr = pltpu.VMEM((TOK, GLA_QK_W), BF16)
    qk_t_scr = pltpu.VMEM((GLA_NCHUNK * GLA_QK_W, GLA_CHUNK), BF16)
    return pl.pallas_call(
        _gla_kernel,
        out_shape=jax.ShapeDtypeStruct((NTOK, GLA_V_W), BF16),
        grid=(BATCH,),
        in_specs=[pl.BlockSpec((TOK, GLA_MAIN_W), lambda b: (b, 0)),
                  pl.BlockSpec((TOK, LANE), lambda b: (b, gate_col)),
                  _resident((LANE, GLA_QK_W)), _resident((1, GLA_QK_W)),
                  _resident((LANE, GLA_QK_W)), _resident((1, GLA_QK_W)),
                  _resident((1, GLA_DV))],
        out_specs=pl.BlockSpec((TOK, GLA_V_W), lambda b: (b, 0)),
        scratch_shapes=[pltpu.VMEM((TOK, GLA_V_W), F32), qk_scr, qk_t_scr, qk_scr, qk_t_scr,
                        pltpu.VMEM((GLA_NCHUNK * GLA_QK_W, LANE), F32),
                        pltpu.VMEM((GLA_QK_W, GLA_DV), F32)],
        compiler_params=_params(("parallel",)),
        name="gla",
    )(p, p, waf, baf, wab, bab, onorm_g.reshape(1, GLA_DV))


def _gelu(x):
    return 0.5 * x * (1.0 + lax.erf(x * (2.0 ** -0.5)))


SG_TILE = 1152
SG_STEPS = NTOK // SG_TILE


def _sg_kernel(u_ref, v_ref, vg_ref, ws_ref, bs_ref, *rest):
    o_ref = rest[len(rest) // 2]
    for c in range(SG_TILE // SG_CHUNK):
        rows = slice(c * SG_CHUNK, (c + 1) * SG_CHUNK)
        for g in range(SG_GROUPS):
            cols = slice(g * SG_DIM, (g + 1) * SG_DIM)
            u = _gelu(u_ref[rows, cols].astype(F32))
            vn = _lane_rms(_gelu(v_ref[rows, cols].astype(F32)), vg_ref[g:g + 1, :])
            mixed = _dot(ws_ref[g], vn.astype(BF16)) + bs_ref[g]
            o_ref[rows, cols] = (u * mixed).astype(o_ref.dtype)
    n_copy = len(rest) // 2
    for src, dst in zip(rest[:n_copy], rest[n_copy + 1:]):
        dst[...] = src[...].astype(BF16)


def _spatial_gating(p, vnorm_g, ws, bs, dense_f32=()):
    width = SG_GROUPS * SG_DIM
    u_col = GLA_MAIN_W // width
    bias = jnp.broadcast_to(bs[:, :, None], (SG_GROUPS, SG_CHUNK, SG_DIM))
    views, copy_in, copy_shapes, copy_out = _copy_plumbing(dense_f32, 0, SG_STEPS)
    out, *copies = pl.pallas_call(
        _sg_kernel,
        out_shape=(jax.ShapeDtypeStruct((NTOK, width), BF16), *copy_shapes),
        grid=(SG_STEPS,),
        in_specs=[pl.BlockSpec((SG_TILE, width), lambda i: (i, u_col)),
                  pl.BlockSpec((SG_TILE, width), lambda i: (i, u_col + 1)),
                  _resident((SG_GROUPS, SG_DIM)),
                  _resident((SG_GROUPS, SG_CHUNK, SG_CHUNK)),
                  _resident((SG_GROUPS, SG_CHUNK, SG_DIM)),
                  *copy_in],
        out_specs=(pl.BlockSpec((SG_TILE, width), lambda i: (i, 0)), *copy_out),
        compiler_params=_params(("parallel",)),
        name="spatial_gating",
    )(p, p, vnorm_g, ws, bias, *views)
    return out, _as_layer_stacks(copies, dense_f32)


FT_W = FT_GROUPS * FT_DIM


def _dft_cos_sin(n):
    jk = np.outer(np.arange(n), np.arange(n)) % n
    ang = 2.0 * np.pi * jk.astype(np.float64) / n
    return np.cos(ang), np.sin(ang)


def _parity_dft(n):
    p = np.arange(n // 2)[:, None]
    mats = []
    for first in (0, 1):
        t = 2 * np.arange(n // 2)[None, :] + first
        ang = 2.0 * np.pi * ((p * t) % n).astype(np.float64) / n
        mats.append(np.concatenate([np.cos(ang), -np.sin(ang)], axis=1))
    return jnp.asarray(np.stack(mats), F32).astype(BF16)


def _fourier_kernel(h_ref, cs_ref, dft_x_ref, dft_c_ref, o_ref, ab_scr, *, need_ctx):
    ab = _dot(h_ref[...], cs_ref[...])
    n_slab = 2 * FT_W // LANE
    for c in range(n_slab):
        ab_scr[c] = ab[:, c * LANE:(c + 1) * LANE]

    def position_dft(first_row, length, dft_ref):
        half = length // 2
        parts = []
        for parity in (0, 1):
            slabs = [ab_scr[c, pl.ds(first_row + parity, half, stride=2), :].astype(BF16) for c in range(n_slab)]
            a = jnp.concatenate(slabs[:n_slab // 2], axis=1)
            b = jnp.concatenate(slabs[n_slab // 2:], axis=1)
            parts.append(_dot(dft_ref[parity], jnp.concatenate([a, b], axis=0)))
        scale = (length * FT_DIM) ** -0.5
        o_ref[first_row:first_row + half, :] = ((parts[0] + parts[1]) * scale).astype(o_ref.dtype)
        o_ref[first_row + half:first_row + length, :] = ((parts[0] - parts[1]) * scale).astype(o_ref.dtype)

    position_dft(CTX_LEN, SEQ, dft_x_ref)
    if need_ctx:
        position_dft(0, CTX_LEN, dft_c_ref)
    else:
        o_ref[:CTX_LEN, :] = jnp.zeros((CTX_LEN, FT_W), o_ref.dtype)


def _fourier(p, need_ctx):
    cc, sc = _dft_cos_sin(FT_DIM)
    eye = np.eye(FT_GROUPS)
    cs = jnp.asarray(np.concatenate([np.kron(eye, cc), np.kron(eye, sc)], axis=1), F32).astype(BF16)
    return pl.pallas_call(
        functools.partial(_fourier_kernel, need_ctx=need_ctx),
        out_shape=jax.ShapeDtypeStruct((NTOK, FT_W), BF16),
        grid=(BATCH,),
        in_specs=[pl.BlockSpec((TOK, FT_W), lambda b: (b, 0)),
                  _resident((FT_W, 2 * FT_W)),
                  _resident((2, SEQ // 2, SEQ)), _resident((2, CTX_LEN // 2, CTX_LEN))],
        out_specs=pl.BlockSpec((TOK, FT_W), lambda b: (b, 0)),
        scratch_shapes=[pltpu.VMEM((2 * FT_W // LANE, TOK, LANE), F32)],
        compiler_params=_params(("parallel",)),
        name="fourier",
    )(p, cs, _parity_dft(SEQ), _parity_dft(CTX_LEN))


MLA_NOPE_W = MLA_HEADS * MLA_NOPE
MLA_ROPE_W = MLA_HEADS * MLA_ROPE
OD_QA0 = FT_W
OD_KVA0 = OD_QA0 + MLA_Q_RANK
OD_KPE0 = OD_KVA0 + MLA_KV_RANK
OD_KPE_ROT0 = OD_KPE0 + MLA_ROPE


def _rot_cols(w):
    q = MLA_ROPE // 4
    return jnp.concatenate([-w[..., q:2 * q], w[..., 0:q], -w[..., 3 * q:4 * q], w[..., 2 * q:3 * q]], axis=-1)


def _rot_gain(g):
    q = MLA_ROPE // 4
    return jnp.concatenate([g[..., q:2 * q], g[..., 0:q], g[..., 3 * q:4 * q], g[..., 2 * q:3 * q]], axis=-1)


def _rope_tables():
    rows = SEQ // GRID_W
    row_id = np.repeat(np.arange(rows, dtype=np.float32), GRID_W)
    col_id = np.tile(np.arange(GRID_W, dtype=np.float32), rows)
    axis_dim = MLA_ROPE // 2
    inv_freq = (np.float32(ROPE_BASE) ** (-np.arange(0, axis_dim, 2, dtype=np.float32) / np.float32(axis_dim))).astype(np.float32)
    ang_r = row_id[:, None] * inv_freq
    ang_c = col_id[:, None] * inv_freq
    ang = np.concatenate([ang_r, ang_r, ang_c, ang_c], axis=-1).astype(np.float32)
    cos = np.concatenate([np.ones((CTX_LEN, MLA_ROPE), np.float32), np.cos(ang)], axis=0)
    sin = np.concatenate([np.zeros((CTX_LEN, MLA_ROPE), np.float32), np.sin(ang)], axis=0)
    return tuple(jnp.asarray(t, F32) for t in (cos, sin, cos.T, sin.T))


MLA_VT_ROWS = MLA_V + 16


def _mla_heads(p, cos, sin, cos_t, sin_t, qag, wuq_t, kvag, wukv_k, wukv_vt, head_sum, kg,
               qg_nope, qg_rope, qg_rot, qt_ref, k_ref, vt_ref):
    qa = _lane_rms(p[:, OD_QA0:OD_KVA0].astype(F32), qag)
    q_t = _dot(wuq_t, qa.T.astype(BF16))
    for h in range(MLA_HEADS):
        qn = q_t[h * MLA_NOPE:(h + 1) * MLA_NOPE]
        qr = q_t[MLA_NOPE_W + h * MLA_ROPE:MLA_NOPE_W + (h + 1) * MLA_ROPE]
        qrr = q_t[MLA_NOPE_W + MLA_ROPE_W + h * MLA_ROPE:MLA_NOPE_W + MLA_ROPE_W + (h + 1) * MLA_ROPE]
        ss = jnp.sum(qn * qn, axis=0, keepdims=True) + jnp.sum(qr * qr, axis=0, keepdims=True)
        inv = lax.rsqrt(ss * (1.0 / MLA_QK) + RMS_EPS) * (MLA_QK ** -0.5)
        q_rope = qr * qg_rope * cos_t + qrr * qg_rot * sin_t
        qt_ref[h] = jnp.concatenate([qn * qg_nope * inv, q_rope * inv], axis=0).astype(qt_ref.dtype)

    kva = _lane_rms(p[:, OD_KVA0:OD_KPE0].astype(F32), kvag)
    v_t = _dot(wukv_vt, kva.T.astype(BF16))
    ones = jnp.ones((MLA_VT_ROWS - MLA_V, v_t.shape[1]), F32)
    for h in range(MLA_HEADS):
        vt_ref[h] = jnp.concatenate([v_t[h * MLA_V:(h + 1) * MLA_V], ones], axis=0).astype(vt_ref.dtype)
    kn = _dot(kva.astype(BF16), wukv_k)
    kpe = p[:, OD_KPE0:OD_KPE_ROT0].astype(F32)
    kpe_rot = p[:, OD_KPE_ROT0:OD_W].astype(F32)
    sq = jnp.concatenate([kn * kn, kpe * kpe], axis=1).astype(BF16)
    ss = _dot(sq, head_sum)
    inv = lax.rsqrt(ss * (1.0 / MLA_QK) + RMS_EPS)
    k_rope = kpe * kg[:, MLA_NOPE:MLA_QK] * cos + kpe_rot * kg[:, MLA_QK:] * sin
    for h in range(MLA_HEADS):
        k_h = jnp.concatenate([kn[:, h * MLA_NOPE:(h + 1) * MLA_NOPE] * kg[:, :MLA_NOPE], k_rope], axis=1)
        k_ref[h] = (k_h * inv[:, h:h + 1]).astype(k_ref.dtype)


def _key_head_sum_matrix():
    head = np.arange(LANE)[None, :]
    nope = np.repeat(np.arange(MLA_HEADS), MLA_NOPE)[:, None] == head
    rope = np.broadcast_to(head < MLA_HEADS, (MLA_ROPE, LANE))
    return jnp.asarray(np.concatenate([nope, rope], axis=0), BF16)


MLA_TILE = 768


def _mla_prep_kernel(p_ref, *refs):
    *operand_refs, qt_ref, k_ref, vt_ref = refs
    _mla_heads(p_ref[...], *(r[...] for r in operand_refs), qt_ref, k_ref, vt_ref)


def _mla_prep(p, rope_tabs, qa_g, wuq, kva_g, wukv, qn_g, kn_g):
    wukv = _wukv_cols(wukv)
    kgain = jnp.concatenate([kn_g, _rot_gain(kn_g[MLA_NOPE:])]).reshape(1, -1)
    col = lambda g: jnp.broadcast_to(g[:, None], (g.shape[0], MLA_TILE))
    operands = [*rope_tabs, qa_g.reshape(1, -1), _wuq_cols(wuq).T, kva_g.reshape(1, -1),
                wukv[:, :MLA_NOPE_W], wukv[:, MLA_NOPE_W:].T, _key_head_sum_matrix(), kgain,
                col(qn_g[:MLA_NOPE]), col(qn_g[MLA_NOPE:]), col(_rot_gain(qn_g[MLA_NOPE:]))]
    tiles_per_batch = TOK // MLA_TILE
    rope = pl.BlockSpec((MLA_TILE, MLA_ROPE), lambda i: (i % tiles_per_batch, 0))
    rope_t = pl.BlockSpec((MLA_ROPE, MLA_TILE), lambda i: (0, i % tiles_per_batch))
    rows = lambda d: pl.BlockSpec((MLA_HEADS, MLA_TILE, d), lambda i: (0, i, 0))
    cols = lambda d: pl.BlockSpec((MLA_HEADS, d, MLA_TILE), lambda i: (0, 0, i))
    return pl.pallas_call(
        _mla_prep_kernel,
        out_shape=(jax.ShapeDtypeStruct((MLA_HEADS, MLA_QK, NTOK), BF16),
                   jax.ShapeDtypeStruct((MLA_HEADS, NTOK, MLA_QK), BF16),
                   jax.ShapeDtypeStruct((MLA_HEADS, MLA_VT_ROWS, NTOK), BF16)),
        grid=(NTOK // MLA_TILE,),
        in_specs=[pl.BlockSpec((MLA_TILE, OD_W), lambda i: (i, 0)), rope, rope, rope_t, rope_t]
                 + [_resident(a.shape) for a in operands[4:]],
        out_specs=(cols(MLA_QK), rows(MLA_QK), cols(MLA_VT_ROWS)),
        compiler_params=_params(("parallel",)),
        name="mla_prep",
    )(p, *operands)


ATT_Q_TILE = 512


ATT_KEY_CHUNK = TOK


def _weighted_values(acc):
    return (acc[:MLA_V] * (1.0 / acc[MLA_V:MLA_V + 1])).T


def _attn_kernel(qt_ref, k_ref, vt_ref, o_ref, s_scr, *, need_ctx):
    n_q = SEQ // ATT_Q_TILE
    chunks = [slice(c * ATT_KEY_CHUNK, (c + 1) * ATT_KEY_CHUNK) for c in range(TOK // ATT_KEY_CHUNK)]
    q_cols = lambda j: slice(CTX_LEN + j * ATT_Q_TILE, CTX_LEN + (j + 1) * ATT_Q_TILE)

    def scores(j, rows, col_max):
        s = _dot(k_ref[0, rows, :], qt_ref[0, :, q_cols(j)])
        s_scr[j % 2, rows, :] = s
        m = jnp.max(s, axis=0, keepdims=True)
        return m if col_max is None else jnp.maximum(col_max, m)

    col_max = None
    for rows in chunks:
        col_max = scores(0, rows, col_max)
    for j in range(n_q):
        acc, next_max = None, None
        for rows in chunks:
            if j + 1 < n_q:
                next_max = scores(j + 1, rows, next_max)
            p = jnp.exp(s_scr[j % 2, rows, :] - col_max).astype(BF16)
            part = _dot(vt_ref[0, :, rows], p)
            acc = part if acc is None else acc + part
        o_ref[q_cols(j), :] = _weighted_values(acc).astype(o_ref.dtype)
        col_max = next_max

    if need_ctx:
        s = _dot(k_ref[0, :CTX_LEN, :], qt_ref[0, :, :CTX_LEN])
        p = jnp.exp(s - jnp.max(s, axis=0, keepdims=True)).astype(BF16)
        o_ref[:CTX_LEN, :] = _weighted_values(_dot(vt_ref[0, :, :CTX_LEN], p)).astype(o_ref.dtype)
    else:
        o_ref[:CTX_LEN, :] = jnp.zeros((CTX_LEN, MLA_V), o_ref.dtype)


def _attention(q_t, k, v_t, need_ctx):
    transposed = lambda d: pl.BlockSpec((1, d, TOK), lambda b, h: (h, 0, b))
    return pl.pallas_call(
        functools.partial(_attn_kernel, need_ctx=need_ctx),
        out_shape=jax.ShapeDtypeStruct((NTOK, MLA_HEADS * MLA_V), BF16),
        grid=(BATCH, MLA_HEADS),
        in_specs=[transposed(MLA_QK),
                  pl.BlockSpec((1, TOK, MLA_QK), lambda b, h: (h, b, 0)),
                  transposed(MLA_VT_ROWS)],
        out_specs=pl.BlockSpec((TOK, MLA_V), lambda b, h: (b, h)),
        scratch_shapes=[pltpu.VMEM((2, TOK, ATT_Q_TILE), F32)],
        compiler_params=_params(("parallel", "parallel")),
        name="attention",
    )(q_t, k, v_t)


def _even_w_in(w):
    gates0 = GLA_MAIN_W
    gates1 = gates0 + 2 * GLA_GATE_RANK
    pad = jnp.zeros(w.shape[:-1] + (EV_W - w.shape[-1],), w.dtype)
    return jnp.concatenate([w[..., :gates0], w[..., gates1:], w[..., gates0:gates1], pad], axis=-1).astype(BF16)


def _gate_w(wa, first_row):
    out = jnp.zeros((LANE, GLA_QK_W), wa.dtype)
    return lax.dynamic_update_slice(out, wa, (first_row, 0)).astype(BF16)


def _odd_w_in(w):
    return jnp.concatenate([w, _rot_cols(w[..., OD_KPE0:OD_KPE_ROT0])], axis=-1).astype(BF16)


def _wuq_cols(w):
    w = w.reshape(MLA_Q_RANK, MLA_HEADS, MLA_QK)
    nope = w[:, :, :MLA_NOPE].reshape(MLA_Q_RANK, MLA_NOPE_W)
    rope = w[:, :, MLA_NOPE:]
    return jnp.concatenate([nope, rope.reshape(MLA_Q_RANK, MLA_ROPE_W),
                            _rot_cols(rope).reshape(MLA_Q_RANK, MLA_ROPE_W)], axis=1).astype(BF16)


def _wukv_cols(w):
    w = w.reshape(MLA_KV_RANK, MLA_HEADS, MLA_NOPE + MLA_V)
    return jnp.concatenate([w[:, :, :MLA_NOPE].reshape(MLA_KV_RANK, MLA_NOPE_W),
                            w[:, :, MLA_NOPE:].reshape(MLA_KV_RANK, MLA_HEADS * MLA_V)], axis=1).astype(BF16)


def kernel(x, c, ctx, c_ctx, ada_w, ada_b, norm_mix_g, norm_ffn_g, w_mix_out, ffn_w_in, ffn_w_out, ev_w_in, gla_wa_f, gla_ba_f, gla_wa_b, gla_ba_b, gla_onorm_g, sg_vnorm_g, sg_ws, sg_bs, od_w_in, mla_qa_g, mla_wuq, mla_kva_g, mla_wukv, mla_qn_g, mla_kn_g):
    cvec = jnp.zeros((MOD_ROWS, D_MODEL), F32).at[:BATCH].set(c).at[CTX_MOD_ROW].set(c_ctx)
    mod = _ada_table(cvec, ada_w, ada_b)
    rope_tabs = _rope_tables()
    norm_mix_g = norm_mix_g.reshape(DEPTH, 1, D_MODEL)
    norm_ffn_g = norm_ffn_g.reshape(DEPTH, 1, D_MODEL)
    w_in_stacks = (_even_w_in(ev_w_in), _odd_w_in(od_w_in))
    dense_f32 = (w_mix_out, ffn_w_in, ffn_w_out)
    dense = None

    p, xa = _first_inproj(x, ctx, mod, norm_mix_g, w_in_stacks[0])
    for l in range(DEPTH):
        last = l == DEPTH - 1
        i = l // 2
        if l % 2 == 0:
            ma = _gla(p, _gate_w(gla_wa_f[i], 0), gla_ba_f[i].reshape(1, GLA_QK_W),
                      _gate_w(gla_wa_b[i], GLA_GATE_RANK), gla_ba_b[i].reshape(1, GLA_QK_W), gla_onorm_g[i])
            mb, first_dense = _spatial_gating(p, sg_vnorm_g[i], sg_ws[i].astype(BF16), sg_bs[i],
                                              dense_f32 if l == 0 else ())
            dense = first_dense if l == 0 else dense
        else:
            q_t, k, v_t = _mla_prep(p, rope_tabs, mla_qa_g[i], mla_wuq[i], mla_kva_g[i], mla_wukv[i],
                                    mla_qn_g[i], mla_kn_g[i])
            ma = _fourier(p, not last)
            mb = _attention(q_t, k, v_t, not last)
        nxt = None if last else (norm_mix_g, w_in_stacks[(l + 1) % 2], (l + 1) // 2, dense_f32)
        xa, p, dense = _layer_tail(xa, ma, mb, mod, l, norm_ffn_g, dense, nxt)
    return xa.reshape(BATCH, SEQ, D_MODEL)
```

```python
import functools

import numpy as np
import jax
import jax.numpy as jnp
from jax import lax
from jax.experimental import pallas as pl
from jax.experimental.pallas import tpu as pltpu

D_MODEL = 1024
BATCH = 8
SEQ = 2048
DEPTH = 4
CTX_LEN = 256
GRID_W = 64
RMS_EPS = 1e-6
GLA_HEADS = 4
GLA_DK = 64
GLA_DV = 128
GLA_GATE_RANK = 16
GLA_GATE_NORM = 16.0
GLA_CHUNK = 64
SG_GROUPS = 4
SG_DIM = 128
SG_CHUNK = 128
FT_GROUPS = 4
FT_DIM = 64
MLA_HEADS = 6
MLA_NOPE = 128
MLA_ROPE = 64
MLA_V = 128
MLA_Q_RANK = 384
MLA_KV_RANK = 256
ROPE_BASE = 10000.0
FFN_HIDDEN = 2816

TOK = CTX_LEN + SEQ
NTOK = BATCH * TOK
ROW_TILE = 256
TILES_PER_BATCH = TOK // ROW_TILE
CTX_TILES = CTX_LEN // ROW_TILE
MOD_ROWS = 16
CTX_MOD_ROW = BATCH
LANE = 128
EV_W = 2688
OD_W = 1024
MLA_QK = MLA_NOPE + MLA_ROPE
VMEM_LIMIT = 56 * 1024 * 1024

F32 = jnp.float32
BF16 = jnp.bfloat16


def _dot(a, b):
    return jnp.dot(a, b, preferred_element_type=F32)


def _rms(x, g):
    return x * lax.rsqrt(jnp.mean(x * x, axis=-1, keepdims=True) + RMS_EPS) * g


def _lane_rms(x, g):
    width = x.shape[1]
    ss = _dot((x * x).astype(BF16), jnp.ones((width, LANE), BF16))
    inv = lax.rsqrt(ss * (1.0 / width) + RMS_EPS)
    return x * jnp.concatenate([inv] * (width // LANE), axis=1) * g


def _params(sem):
    return pltpu.CompilerParams(dimension_semantics=sem, vmem_limit_bytes=VMEM_LIMIT)


def _resident(shape):
    zeros = (0,) * len(shape)
    return pl.BlockSpec(shape, lambda *_: zeros, pipeline_mode=pl.Buffered(1))


def _mod_row(i):
    return jnp.where(i % TILES_PER_BATCH < CTX_TILES, CTX_MOD_ROW, i // TILES_PER_BATCH)


MOD_PIECES = 6


def _mod_spec(layer, tile_of_step):
    def index_map(i):
        return (layer * MOD_ROWS + _mod_row(tile_of_step(i)), 0, 0)
    return pl.BlockSpec((1, MOD_PIECES, D_MODEL), index_map)


def _ada_kernel(c_ref, w_ref, b_ref, o_ref):
    c = c_ref[...]
    s = c * jax.nn.sigmoid(c)
    w = w_ref[0]
    w_hi = w.astype(BF16)
    w_lo = (w - w_hi.astype(F32)).astype(BF16)
    s_hi = s.astype(BF16)
    s_lo = (s - s_hi.astype(F32)).astype(BF16)
    o_ref[0] = _dot(s_hi, w_hi) + (_dot(s_hi, w_lo) + _dot(s_lo, w_hi)) + b_ref[0]


def _ada_table(cvec, ada_w, ada_b):
    tn = 3072
    n = MOD_PIECES * D_MODEL
    out = pl.pallas_call(
        _ada_kernel,
        out_shape=jax.ShapeDtypeStruct((DEPTH, MOD_ROWS, n), F32),
        grid=(DEPTH, n // tn),
        in_specs=[pl.BlockSpec((MOD_ROWS, D_MODEL), lambda l, j: (0, 0)),
                  pl.BlockSpec((1, D_MODEL, tn), lambda l, j: (l, 0, j)),
                  pl.BlockSpec((1, 1, tn), lambda l, j: (l, 0, j))],
        out_specs=pl.BlockSpec((1, MOD_ROWS, tn), lambda l, j: (l, 0, j)),
        compiler_params=_params(("arbitrary", "arbitrary")),
        name="ada_table",
    )(cvec, ada_w, ada_b.reshape(DEPTH, 1, n))
    return out.reshape(DEPTH * MOD_ROWS, MOD_PIECES, D_MODEL)


def _layer_block(shape, layer):
    zeros = (0,) * len(shape)
    return pl.BlockSpec((1,) + tuple(shape), lambda *_: (layer,) + zeros, pipeline_mode=pl.Buffered(1))


BF16_SUBLANES = 16


def _copy_plumbing(stacks, layer, n_steps):
    views, in_specs, shapes, out_specs = [], [], [], []
    for w in stacks:
        n_layers, n_rows, cols = w.shape
        rows = next(r for r in range(BF16_SUBLANES, n_rows + 1, BF16_SUBLANES)
                    if n_rows % r == 0 and n_rows // r <= n_steps)
        n_slices = n_rows // rows
        views.append(w.reshape(n_layers * n_rows, cols))
        in_specs.append(pl.BlockSpec(
            (rows, cols), lambda i, n=n_slices: (layer * n + jnp.minimum(i, n - 1), 0)))
        shapes.append(jax.ShapeDtypeStruct((n_rows, cols), BF16))
        out_specs.append(pl.BlockSpec((rows, cols), lambda i, n=n_slices: (jnp.minimum(i, n - 1), 0)))
    return views, in_specs, shapes, out_specs


def _as_layer_stacks(copies, stacks):
    return [c.reshape(1, w.shape[1], w.shape[2]) for c, w in zip(copies, stacks)]


def _modulated_norm(x, g, shift, scale):
    return _rms(x, g) * (1.0 + scale) + shift


def _first_inproj_kernel(x_ref, ctx_ref, g_ref, mod_ref, w_ref, p_ref, xa_ref):
    is_ctx = pl.program_id(0) % TILES_PER_BATCH < CTX_TILES
    x = jnp.where(is_ctx, ctx_ref[...], x_ref[...])
    xa_ref[...] = x
    mod = mod_ref[0]
    z = _modulated_norm(x, g_ref[0], mod[0:1], mod[1:2])
    p_ref[...] = _dot(z.astype(BF16), w_ref[0]).astype(p_ref.dtype)


def _first_inproj(x, ctx, mod, norm_g, w):
    width = w.shape[-1]
    x_tiles = SEQ // ROW_TILE
    x_tile = lambda i: ((i // TILES_PER_BATCH) * x_tiles + jnp.maximum(i % TILES_PER_BATCH - CTX_TILES, 0), 0)
    ctx_tile = lambda i: ((i // TILES_PER_BATCH) * CTX_TILES + jnp.minimum(i % TILES_PER_BATCH, CTX_TILES - 1), 0)
    ident = lambda i: i
    row_out = lambda wd: pl.BlockSpec((ROW_TILE, wd), lambda i: (i, 0))
    return pl.pallas_call(
        _first_inproj_kernel,
        out_shape=(jax.ShapeDtypeStruct((NTOK, width), BF16), jax.ShapeDtypeStruct((NTOK, D_MODEL), F32)),
        grid=(NTOK // ROW_TILE,),
        in_specs=[pl.BlockSpec((ROW_TILE, D_MODEL), x_tile),
                  pl.BlockSpec((ROW_TILE, D_MODEL), ctx_tile),
                  _layer_block((1, D_MODEL), 0),
                  _mod_spec(0, ident),
                  _layer_block((D_MODEL, width), 0)],
        out_specs=(row_out(width), row_out(D_MODEL)),
        compiler_params=_params(("parallel",)),
        name="first_inproj",
    )(x.reshape(BATCH * SEQ, D_MODEL), ctx.reshape(BATCH * CTX_LEN, D_MODEL), norm_g, mod, w)


def _tail_kernel(x_ref, ma_ref, mb_ref, mod_ref, g_ref, wmix_ref, win_ref, wout_ref, *rest, has_next):
    mod = mod_ref[0]
    m = jnp.concatenate([ma_ref[...], mb_ref[...]], axis=1)
    x1 = x_ref[...] + mod[2:3] * _dot(m, wmix_ref[0])
    h = _modulated_norm(x1, g_ref[0], mod[3:4], mod[4:5]).astype(BF16)
    gu = _dot(h, win_ref[0])
    gate = gu[:, :FFN_HIDDEN]
    act = (gate * jax.nn.sigmoid(gate) * gu[:, FFN_HIDDEN:]).astype(BF16)
    x2 = x1 + mod[5:6] * _dot(act, wout_ref[0])
    if has_next:
        next_mod_ref, gn_ref, wn_ref, *copy_refs, o_ref, p_ref, mix_copy, in_copy, out_copy = rest
        next_mod = next_mod_ref[0]
        z = _modulated_norm(x2, gn_ref[0], next_mod[0:1], next_mod[1:2])
        p_ref[...] = _dot(z.astype(BF16), wn_ref[0]).astype(p_ref.dtype)
        for src, dst in zip(copy_refs, (mix_copy, in_copy, out_copy)):
            dst[...] = src[...].astype(BF16)
    else:
        (o_ref,) = rest
    o_ref[...] = x2


def _layer_tail(xa, ma, mb, mod, layer, norm_ffn_g, dense, nxt):
    wmix, win, wout = dense
    if nxt is None:
        x_tiles = SEQ // ROW_TILE
        n_steps = BATCH * x_tiles
        tile = lambda i: (i // x_tiles) * TILES_PER_BATCH + CTX_TILES + i % x_tiles
        out_rows = BATCH * SEQ
    else:
        n_steps = NTOK // ROW_TILE
        tile = lambda i: i
        out_rows = NTOK
    row_in = lambda width: pl.BlockSpec((ROW_TILE, width), lambda i: (tile(i), 0))
    row_out = lambda width: pl.BlockSpec((ROW_TILE, width), lambda i: (i, 0))
    in_specs = [row_in(D_MODEL), row_in(ma.shape[1]), row_in(mb.shape[1]),
                _mod_spec(layer, tile),
                _layer_block((1, D_MODEL), layer),
                _layer_block((D_MODEL, D_MODEL), 0),
                _layer_block((D_MODEL, 2 * FFN_HIDDEN), 0),
                _layer_block((FFN_HIDDEN, D_MODEL), 0)]
    args = [xa, ma, mb, mod, norm_ffn_g, wmix, win, wout]
    out_shape = [jax.ShapeDtypeStruct((out_rows, D_MODEL), F32)]
    out_specs = [row_out(D_MODEL)]
    if nxt is not None:
        norm_mix_g, w_next, idx, dense_f32 = nxt
        width = w_next.shape[-1]
        views, copy_in, copy_shapes, copy_out = _copy_plumbing(dense_f32, layer + 1, n_steps)
        in_specs += [_mod_spec(layer + 1, tile),
                     _layer_block((1, D_MODEL), layer + 1), _layer_block((D_MODEL, width), idx)] + copy_in
        args += [mod, norm_mix_g, w_next] + views
        out_shape += [jax.ShapeDtypeStruct((out_rows, width), BF16)] + copy_shapes
        out_specs += [row_out(width)] + copy_out
    res = pl.pallas_call(
        functools.partial(_tail_kernel, has_next=nxt is not None),
        out_shape=tuple(out_shape),
        grid=(n_steps,),
        in_specs=in_specs,
        out_specs=tuple(out_specs),
        compiler_params=_params(("arbitrary",)),
        name="layer_tail",
    )(*args)
    if nxt is None:
        return res[0], None, None
    return res[0], res[1], _as_layer_stacks(res[2:], dense_f32)


GLA_QK_W = GLA_HEADS * GLA_DK
GLA_V_W = GLA_HEADS * GLA_DV
GLA_MAIN_W = 2 * GLA_QK_W + 2 * GLA_V_W
GLA_NCHUNK = TOK // GLA_CHUNK


GLA_BLOCK = 256
GLA_NBLOCK = TOK // GLA_BLOCK
GLA_CTX_BLOCKS = CTX_LEN // GLA_BLOCK
GLA_UNROLL = 3


def _log_sigmoid(y):
    return jnp.minimum(y, 0.0) - jnp.log(1.0 + jnp.exp(-jnp.abs(y)))


def _gla_kernel(main_ref, gate_ref, waf_ref, baf_ref, wab_ref, bab_ref, og_ref, o_ref,
                of_scr, qd_scr, kd_scr, qe_scr, kl_scr, dec_scr, s_scr):
    C, BLK = GLA_CHUNK, GLA_BLOCK
    cpb = BLK // C
    rb = lax.broadcasted_iota(jnp.int32, (BLK, BLK), 0)
    cb = lax.broadcasted_iota(jnp.int32, (BLK, BLK), 1)
    same_chunk = (rb // C) == (cb // C)
    tri_f = jnp.where(same_chunk & (rb >= cb), 1.0, 0.0).astype(BF16)
    tri_b = jnp.where(same_chunk & (rb <= cb), 1.0, 0.0).astype(BF16)
    r = lax.broadcasted_iota(jnp.int32, (C, C), 0)
    cc = lax.broadcasted_iota(jnp.int32, (C, C), 1)
    lower, upper = r >= cc, r <= cc
    sr = lax.broadcasted_iota(jnp.int32, (BLK, cpb * LANE), 0)
    sc = lax.broadcasted_iota(jnp.int32, (BLK, cpb * LANE), 1)
    chunk_sum = jnp.where(sr // C == sc // LANE, 1.0, 0.0).astype(BF16)

    def split(x):
        hi = x.astype(BF16)
        return hi, (x - hi.astype(F32)).astype(BF16)

    def dot_01(m, x):
        hi, lo = split(x)
        return _dot(jnp.concatenate([m, m], axis=1), jnp.concatenate([hi, lo], axis=0))

    def dot_10(x, m):
        hi, lo = split(x)
        return _dot(jnp.concatenate([hi, lo], axis=1), jnp.concatenate([m, m], axis=0))

    def block_rows(blk):
        return pl.ds(pl.multiple_of(blk * BLK, BLK), BLK)

    def chunk_rows(blk, j):
        return pl.ds(pl.multiple_of(blk * BLK + j * C, C), C)

    def dec_rows(blk, j):
        return pl.ds(pl.multiple_of((blk * cpb + j) * GLA_QK_W, GLA_QK_W), GLA_QK_W)

    def decay_terms(a, q, k, w_ref, bias_ref, tri, end_row):
        g = _log_sigmoid(_dot(a, w_ref[...]) + bias_ref[...]) * (1.0 / GLA_GATE_NORM)
        b = dot_01(tri, g)
        dec_all = jnp.exp(dot_10(g.T, chunk_sum))
        terms = []
        for j in range(cpb):
            sl = slice(j * C, (j + 1) * C)
            bc = b[sl]
            b_mid = bc[C // 2:C // 2 + 1]
            b_end = bc[end_row:end_row + 1]
            d = bc - b_mid
            qd = q[sl] * jnp.exp(d)
            kd = k[sl] * jnp.exp(-d)
            qe = qd * jnp.exp(b_mid)
            kl = kd * jnp.exp(b_end - b_mid)
            dec = dec_all[:, j * LANE:(j + 1) * LANE]
            terms.append((qd.astype(BF16), kd.T.astype(BF16), qe.astype(BF16), kl.T.astype(BF16), dec))
        return terms

    def step(qd, kd_t, qe, kl_t, dec, v, keep):
        outs = []
        for h in range(GLA_HEADS):
            ks = slice(h * GLA_DK, (h + 1) * GLA_DK)
            vh = v[:, h * GLA_DV:(h + 1) * GLA_DV]
            s_h = s_scr[ks, :]
            att = jnp.where(keep, _dot(qd[:, ks], kd_t[ks, :]), 0.0).astype(BF16)
            outs.append(_dot(att, vh) + _dot(qe[:, ks], s_h.astype(BF16)))
            s_scr[ks, :] = s_h * dec[ks, :] + _dot(kl_t[ks, :], vh)
        return outs

    s_scr[...] = jnp.zeros(s_scr.shape, F32)

    def loop1(blk, carry):
        rows = block_rows(blk)
        a = gate_ref[rows, :]
        q = main_ref[rows, 0:GLA_QK_W].astype(F32) * (GLA_DK ** -0.5)
        k = main_ref[rows, GLA_QK_W:2 * GLA_QK_W].astype(F32)
        v = main_ref[rows, 2 * GLA_QK_W:2 * GLA_QK_W + GLA_V_W]
        fwd = decay_terms(a, q, k, waf_ref, baf_ref, tri_f, C - 1)
        bwd = decay_terms(a, q, k, wab_ref, bab_ref, tri_b, 0)
        for j in range(cpb):
            cr = chunk_rows(blk, j)
            qd, kd, qe, kl, dec = bwd[j]
            qd_scr[cr, :] = qd
            kd_scr[dec_rows(blk, j), :] = kd
            qe_scr[cr, :] = qe
            kl_scr[dec_rows(blk, j), :] = kl
            dec_scr[dec_rows(blk, j), :] = dec
            qd, kd, qe, kl, dec = fwd[j]
            outs = step(qd, kd, qe, kl, dec, v[j * C:(j + 1) * C], lower)
            of_scr[cr, :] = jnp.concatenate(outs, axis=1)
        return carry

    lax.fori_loop(0, GLA_NBLOCK, loop1, 0, unroll=GLA_UNROLL)

    s_scr[...] = jnp.zeros(s_scr.shape, F32)

    def loop2(i, carry):
        blk = jnp.where(i < GLA_CTX_BLOCKS, GLA_CTX_BLOCKS - 1 - i, GLA_NBLOCK - 1 - (i - GLA_CTX_BLOCKS))
        for j in reversed(range(cpb)):
            cr = chunk_rows(blk, j)
            v = main_ref[cr, 2 * GLA_QK_W:2 * GLA_QK_W + GLA_V_W]
            kr = dec_rows(blk, j)
            outs = step(qd_scr[cr, :], kd_scr[kr, :], qe_scr[cr, :], kl_scr[kr, :], dec_scr[kr, :], v, upper)
            gout = main_ref[cr, 2 * GLA_QK_W + GLA_V_W:GLA_MAIN_W].astype(F32)
            res = []
            for h in range(GLA_HEADS):
                vs = slice(h * GLA_DV, (h + 1) * GLA_DV)
                o = outs[h] + of_scr[cr, vs]
                gh = gout[:, vs]
                res.append(_rms(o, og_ref[...]) * (gh * jax.nn.sigmoid(gh)))
            o_ref[cr, :] = jnp.concatenate(res, axis=1).astype(o_ref.dtype)
        return carry

    lax.fori_loop(0, GLA_NBLOCK, loop2, 0, unroll=GLA_UNROLL)


def _gla(p, waf, baf, wab, bab, onorm_g):
    gate_col = (EV_W - LANE) // LANE
    qk_scr = pltpu.VMEM((TOK, GLA_QK_W), BF16)
    qk_t_scr = pltpu.VMEM((GLA_NCHUNK * GLA_QK_W, GLA_CHUNK), BF16)
    return pl.pallas_call(
        _gla_kernel,
        out_shape=jax.ShapeDtypeStruct((NTOK, GLA_V_W), BF16),
        grid=(BATCH,),
        in_specs=[pl.BlockSpec((TOK, GLA_MAIN_W), lambda b: (b, 0)),
                  pl.BlockSpec((TOK, LANE), lambda b: (b, gate_col)),
                  _resident((LANE, GLA_QK_W)), _resident((1, GLA_QK_W)),
                  _resident((LANE, GLA_QK_W)), _resident((1, GLA_QK_W)),
                  _resident((1, GLA_DV))],
        out_specs=pl.BlockSpec((TOK, GLA_V_W), lambda b: (b, 0)),
        scratch_shapes=[pltpu.VMEM((TOK, GLA_V_W), F32), qk_scr, qk_t_scr, qk_scr, qk_t_scr,
                        pltpu.VMEM((GLA_NCHUNK * GLA_QK_W, LANE), F32),
                        pltpu.VMEM((GLA_QK_W, GLA_DV), F32)],
        compiler_params=_params(("parallel",)),
        name="gla",
    )(p, p, waf, baf, wab, bab, onorm_g.reshape(1, GLA_DV))


def _gelu(x):
    return 0.5 * x * (1.0 + lax.erf(x * (2.0 ** -0.5)))


SG_TILE = 1152
SG_STEPS = NTOK // SG_TILE


def _sg_kernel(u_ref, v_ref, vg_ref, ws_ref, bs_ref, *rest):
    o_ref = rest[len(rest) // 2]
    for c in range(SG_TILE // SG_CHUNK):
        rows = slice(c * SG_CHUNK, (c + 1) * SG_CHUNK)
        for g in range(SG_GROUPS):
            cols = slice(g * SG_DIM, (g + 1) * SG_DIM)
            u = _gelu(u_ref[rows, cols].astype(F32))
            vn = _lane_rms(_gelu(v_ref[rows, cols].astype(F32)), vg_ref[g:g + 1, :])
            mixed = _dot(ws_ref[g], vn.astype(BF16)) + bs_ref[g]
            o_ref[rows, cols] = (u * mixed).astype(o_ref.dtype)
    n_copy = len(rest) // 2
    for src, dst in zip(rest[:n_copy], rest[n_copy + 1:]):
        dst[...] = src[...].astype(BF16)


def _spatial_gating(p, vnorm_g, ws, bs, dense_f32=()):
    width = SG_GROUPS * SG_DIM
    u_col = GLA_MAIN_W // width
    bias = jnp.broadcast_to(bs[:, :, None], (SG_GROUPS, SG_CHUNK, SG_DIM))
    views, copy_in, copy_shapes, copy_out = _copy_plumbing(dense_f32, 0, SG_STEPS)
    out, *copies = pl.pallas_call(
        _sg_kernel,
        out_shape=(jax.ShapeDtypeStruct((NTOK, width), BF16), *copy_shapes),
        grid=(SG_STEPS,),
        in_specs=[pl.BlockSpec((SG_TILE, width), lambda i: (i, u_col)),
                  pl.BlockSpec((SG_TILE, width), lambda i: (i, u_col + 1)),
                  _resident((SG_GROUPS, SG_DIM)),
                  _resident((SG_GROUPS, SG_CHUNK, SG_CHUNK)),
                  _resident((SG_GROUPS, SG_CHUNK, SG_DIM)),
                  *copy_in],
        out_specs=(pl.BlockSpec((SG_TILE, width), lambda i: (i, 0)), *copy_out),
        compiler_params=_params(("parallel",)),
        name="spatial_gating",
    )(p, p, vnorm_g, ws, bias, *views)
    return out, _as_layer_stacks(copies, dense_f32)


FT_W = FT_GROUPS * FT_DIM


def _dft_cos_sin(n):
    jk = np.outer(np.arange(n), np.arange(n)) % n
    ang = 2.0 * np.pi * jk.astype(np.float64) / n
    return np.cos(ang), np.sin(ang)


def _parity_dft(n):
    p = np.arange(n // 2)[:, None]
    mats = []
    for first in (0, 1):
        t = 2 * np.arange(n // 2)[None, :] + first
        ang = 2.0 * np.pi * ((p * t) % n).astype(np.float64) / n
        mats.append(np.concatenate([np.cos(ang), -np.sin(ang)], axis=1))
    return jnp.asarray(np.stack(mats), F32).astype(BF16)


def _fourier_kernel(h_ref, cs_ref, dft_x_ref, dft_c_ref, o_ref, ab_scr, *, need_ctx):
    ab = _dot(h_ref[...], cs_ref[...])
    n_slab = 2 * FT_W // LANE
    for c in range(n_slab):
        ab_scr[c] = ab[:, c * LANE:(c + 1) * LANE]

    def position_dft(first_row, length, dft_ref):
        half = length // 2
        parts = []
        for parity in (0, 1):
            slabs = [ab_scr[c, pl.ds(first_row + parity, half, stride=2), :].astype(BF16) for c in range(n_slab)]
            a = jnp.concatenate(slabs[:n_slab // 2], axis=1)
            b = jnp.concatenate(slabs[n_slab // 2:], axis=1)
            parts.append(_dot(dft_ref[parity], jnp.concatenate([a, b], axis=0)))
        scale = (length * FT_DIM) ** -0.5
        o_ref[first_row:first_row + half, :] = ((parts[0] + parts[1]) * scale).astype(o_ref.dtype)
        o_ref[first_row + half:first_row + length, :] = ((parts[0] - parts[1]) * scale).astype(o_ref.dtype)

    position_dft(CTX_LEN, SEQ, dft_x_ref)
    if need_ctx:
        position_dft(0, CTX_LEN, dft_c_ref)
    else:
        o_ref[:CTX_LEN, :] = jnp.zeros((CTX_LEN, FT_W), o_ref.dtype)


def _fourier(p, need_ctx):
    cc, sc = _dft_cos_sin(FT_DIM)
    eye = np.eye(FT_GROUPS)
    cs = jnp.asarray(np.concatenate([np.kron(eye, cc), np.kron(eye, sc)], axis=1), F32).astype(BF16)
    return pl.pallas_call(
        functools.partial(_fourier_kernel, need_ctx=need_ctx),
        out_shape=jax.ShapeDtypeStruct((NTOK, FT_W), BF16),
        grid=(BATCH,),
        in_specs=[pl.BlockSpec((TOK, FT_W), lambda b: (b, 0)),
                  _resident((FT_W, 2 * FT_W)),
                  _resident((2, SEQ // 2, SEQ)), _resident((2, CTX_LEN // 2, CTX_LEN))],
        out_specs=pl.BlockSpec((TOK, FT_W), lambda b: (b, 0)),
        scratch_shapes=[pltpu.VMEM((2 * FT_W // LANE, TOK, LANE), F32)],
        compiler_params=_params(("parallel",)),
        name="fourier",
    )(p, cs, _parity_dft(SEQ), _parity_dft(CTX_LEN))


MLA_NOPE_W = MLA_HEADS * MLA_NOPE
MLA_ROPE_W = MLA_HEADS * MLA_ROPE
OD_QA0 = FT_W
OD_KVA0 = OD_QA0 + MLA_Q_RANK
OD_KPE0 = OD_KVA0 + MLA_KV_RANK
OD_KPE_ROT0 = OD_KPE0 + MLA_ROPE


def _rot_cols(w):
    q = MLA_ROPE // 4
    return jnp.concatenate([-w[..., q:2 * q], w[..., 0:q], -w[..., 3 * q:4 * q], w[..., 2 * q:3 * q]], axis=-1)


def _rot_gain(g):
    q = MLA_ROPE // 4
    return jnp.concatenate([g[..., q:2 * q], g[..., 0:q], g[..., 3 * q:4 * q], g[..., 2 * q:3 * q]], axis=-1)


def _rope_tables():
    rows = SEQ // GRID_W
    row_id = np.repeat(np.arange(rows, dtype=np.float32), GRID_W)
    col_id = np.tile(np.arange(GRID_W, dtype=np.float32), rows)
    axis_dim = MLA_ROPE // 2
    inv_freq = (np.float32(ROPE_BASE) ** (-np.arange(0, axis_dim, 2, dtype=np.float32) / np.float32(axis_dim))).astype(np.float32)
    ang_r = row_id[:, None] * inv_freq
    ang_c = col_id[:, None] * inv_freq
    ang = np.concatenate([ang_r, ang_r, ang_c, ang_c], axis=-1).astype(np.float32)
    cos = np.concatenate([np.ones((CTX_LEN, MLA_ROPE), np.float32), np.cos(ang)], axis=0)
    sin = np.concatenate([np.zeros((CTX_LEN, MLA_ROPE), np.float32), np.sin(ang)], axis=0)
    return tuple(jnp.asarray(t, F32) for t in (cos, sin, cos.T, sin.T))


MLA_VT_ROWS = MLA_V + 16


def _mla_heads(p, cos, sin, cos_t, sin_t, qag, wuq_t, kvag, wukv_k, wukv_vt, head_sum, kg,
               qg_nope, qg_rope, qg_rot, qt_ref, k_ref, vt_ref):
    qa = _lane_rms(p[:, OD_QA0:OD_KVA0].astype(F32), qag)
    q_t = _dot(wuq_t, qa.T.astype(BF16))
    for h in range(MLA_HEADS):
        qn = q_t[h * MLA_NOPE:(h + 1) * MLA_NOPE]
        qr = q_t[MLA_NOPE_W + h * MLA_ROPE:MLA_NOPE_W + (h + 1) * MLA_ROPE]
        qrr = q_t[MLA_NOPE_W + MLA_ROPE_W + h * MLA_ROPE:MLA_NOPE_W + MLA_ROPE_W + (h + 1) * MLA_ROPE]
        ss = jnp.sum(qn * qn, axis=0, keepdims=True) + jnp.sum(qr * qr, axis=0, keepdims=True)
        inv = lax.rsqrt(ss * (1.0 / MLA_QK) + RMS_EPS) * (MLA_QK ** -0.5)
        q_rope = qr * qg_rope * cos_t + qrr * qg_rot * sin_t
        qt_ref[h] = jnp.concatenate([qn * qg_nope * inv, q_rope * inv], axis=0).astype(qt_ref.dtype)

    kva = _lane_rms(p[:, OD_KVA0:OD_KPE0].astype(F32), kvag)
    v_t = _dot(wukv_vt, kva.T.astype(BF16))
    ones = jnp.ones((MLA_VT_ROWS - MLA_V, v_t.shape[1]), F32)
    for h in range(MLA_HEADS):
        vt_ref[h] = jnp.concatenate([v_t[h * MLA_V:(h + 1) * MLA_V], ones], axis=0).astype(vt_ref.dtype)
    kn = _dot(kva.astype(BF16), wukv_k)
    kpe = p[:, OD_KPE0:OD_KPE_ROT0].astype(F32)
    kpe_rot = p[:, OD_KPE_ROT0:OD_W].astype(F32)
    sq = jnp.concatenate([kn * kn, kpe * kpe], axis=1).astype(BF16)
    ss = _dot(sq, head_sum)
    inv = lax.rsqrt(ss * (1.0 / MLA_QK) + RMS_EPS)
    k_rope = kpe * kg[:, MLA_NOPE:MLA_QK] * cos + kpe_rot * kg[:, MLA_QK:] * sin
    for h in range(MLA_HEADS):
        k_h = jnp.concatenate([kn[:, h * MLA_NOPE:(h + 1) * MLA_NOPE] * kg[:, :MLA_NOPE], k_rope], axis=1)
        k_ref[h] = (k_h * inv[:, h:h + 1]).astype(k_ref.dtype)


def _key_head_sum_matrix():
    head = np.arange(LANE)[None, :]
    nope = np.repeat(np.arange(MLA_HEADS), MLA_NOPE)[:, None] == head
    rope = np.broadcast_to(head < MLA_HEADS, (MLA_ROPE, LANE))
    return jnp.asarray(np.concatenate([nope, rope], axis=0), BF16)


MLA_TILE = 768


def _mla_prep_kernel(p_ref, *refs):
    *operand_refs, qt_ref, k_ref, vt_ref = refs
    _mla_heads(p_ref[...], *(r[...] for r in operand_refs), qt_ref, k_ref, vt_ref)


def _mla_prep(p, rope_tabs, qa_g, wuq, kva_g, wukv, qn_g, kn_g):
    wukv = _wukv_cols(wukv)
    kgain = jnp.concatenate([kn_g, _rot_gain(kn_g[MLA_NOPE:])]).reshape(1, -1)
    col = lambda g: jnp.broadcast_to(g[:, None], (g.shape[0], MLA_TILE))
    operands = [*rope_tabs, qa_g.reshape(1, -1), _wuq_cols(wuq).T, kva_g.reshape(1, -1),
                wukv[:, :MLA_NOPE_W], wukv[:, MLA_NOPE_W:].T, _key_head_sum_matrix(), kgain,
                col(qn_g[:MLA_NOPE]), col(qn_g[MLA_NOPE:]), col(_rot_gain(qn_g[MLA_NOPE:]))]
    tiles_per_batch = TOK // MLA_TILE
    rope = pl.BlockSpec((MLA_TILE, MLA_ROPE), lambda i: (i % tiles_per_batch, 0))
    rope_t = pl.BlockSpec((MLA_ROPE, MLA_TILE), lambda i: (0, i % tiles_per_batch))
    rows = lambda d: pl.BlockSpec((MLA_HEADS, MLA_TILE, d), lambda i: (0, i, 0))
    cols = lambda d: pl.BlockSpec((MLA_HEADS, d, MLA_TILE), lambda i: (0, 0, i))
    return pl.pallas_call(
        _mla_prep_kernel,
        out_shape=(jax.ShapeDtypeStruct((MLA_HEADS, MLA_QK, NTOK), BF16),
                   jax.ShapeDtypeStruct((MLA_HEADS, NTOK, MLA_QK), BF16),
                   jax.ShapeDtypeStruct((MLA_HEADS, MLA_VT_ROWS, NTOK), BF16)),
        grid=(NTOK // MLA_TILE,),
        in_specs=[pl.BlockSpec((MLA_TILE, OD_W), lambda i: (i, 0)), rope, rope, rope_t, rope_t]
                 + [_resident(a.shape) for a in operands[4:]],
        out_specs=(cols(MLA_QK), rows(MLA_QK), cols(MLA_VT_ROWS)),
        compiler_params=_params(("parallel",)),
        name="mla_prep",
    )(p, *operands)


ATT_Q_TILE = 512


ATT_KEY_CHUNK = TOK


def _weighted_values(acc):
    return (acc[:MLA_V] * (1.0 / acc[MLA_V:MLA_V + 1])).T


def _attn_kernel(qt_ref, k_ref, vt_ref, o_ref, s_scr, *, need_ctx):
    n_q = SEQ // ATT_Q_TILE
    chunks = [slice(c * ATT_KEY_CHUNK, (c + 1) * ATT_KEY_CHUNK) for c in range(TOK // ATT_KEY_CHUNK)]
    q_cols = lambda j: slice(CTX_LEN + j * ATT_Q_TILE, CTX_LEN + (j + 1) * ATT_Q_TILE)

    def scores(j, rows, col_max):
        s = _dot(k_ref[0, rows, :], qt_ref[0, :, q_cols(j)])
        s_scr[j % 2, rows, :] = s
        m = jnp.max(s, axis=0, keepdims=True)
        return m if col_max is None else jnp.maximum(col_max, m)

    col_max = None
    for rows in chunks:
        col_max = scores(0, rows, col_max)
    for j in range(n_q):
        acc, next_max = None, None
        for rows in chunks:
            if j + 1 < n_q:
                next_max = scores(j + 1, rows, next_max)
            p = jnp.exp(s_scr[j % 2, rows, :] - col_max).astype(BF16)
            part = _dot(vt_ref[0, :, rows], p)
            acc = part if acc is None else acc + part
        o_ref[q_cols(j), :] = _weighted_values(acc).astype(o_ref.dtype)
        col_max = next_max

    if need_ctx:
        s = _dot(k_ref[0, :CTX_LEN, :], qt_ref[0, :, :CTX_LEN])
        p = jnp.exp(s - jnp.max(s, axis=0, keepdims=True)).astype(BF16)
        o_ref[:CTX_LEN, :] = _weighted_values(_dot(vt_ref[0, :, :CTX_LEN], p)).astype(o_ref.dtype)
    else:
        o_ref[:CTX_LEN, :] = jnp.zeros((CTX_LEN, MLA_V), o_ref.dtype)


def _attention(q_t, k, v_t, need_ctx):
    transposed = lambda d: pl.BlockSpec((1, d, TOK), lambda b, h: (h, 0, b))
    return pl.pallas_call(
        functools.partial(_attn_kernel, need_ctx=need_ctx),
        out_shape=jax.ShapeDtypeStruct((NTOK, MLA_HEADS * MLA_V), BF16),
        grid=(BATCH, MLA_HEADS),
        in_specs=[transposed(MLA_QK),
                  pl.BlockSpec((1, TOK, MLA_QK), lambda b, h: (h, b, 0)),
                  transposed(MLA_VT_ROWS)],
        out_specs=pl.BlockSpec((TOK, MLA_V), lambda b, h: (b, h)),
        scratch_shapes=[pltpu.VMEM((2, TOK, ATT_Q_TILE), F32)],
        compiler_params=_params(("parallel", "parallel")),
        name="attention",
    )(q_t, k, v_t)


def _even_w_in(w):
    gates0 = GLA_MAIN_W
    gates1 = gates0 + 2 * GLA_GATE_RANK
    pad = jnp.zeros(w.shape[:-1] + (EV_W - w.shape[-1],), w.dtype)
    return jnp.concatenate([w[..., :gates0], w[..., gates1:], w[..., gates0:gates1], pad], axis=-1).astype(BF16)


def _gate_w(wa, first_row):
    out = jnp.zeros((LANE, GLA_QK_W), wa.dtype)
    return lax.dynamic_update_slice(out, wa, (first_row, 0)).astype(BF16)


def _odd_w_in(w):
    return jnp.concatenate([w, _rot_cols(w[..., OD_KPE0:OD_KPE_ROT0])], axis=-1).astype(BF16)


def _wuq_cols(w):
    w = w.reshape(MLA_Q_RANK, MLA_HEADS, MLA_QK)
    nope = w[:, :, :MLA_NOPE].reshape(MLA_Q_RANK, MLA_NOPE_W)
    rope = w[:, :, MLA_NOPE:]
    return jnp.concatenate([nope, rope.reshape(MLA_Q_RANK, MLA_ROPE_W),
                            _rot_cols(rope).reshape(MLA_Q_RANK, MLA_ROPE_W)], axis=1).astype(BF16)


def _wukv_cols(w):
    w = w.reshape(MLA_KV_RANK, MLA_HEADS, MLA_NOPE + MLA_V)
    return jnp.concatenate([w[:, :, :MLA_NOPE].reshape(MLA_KV_RANK, MLA_NOPE_W),
                            w[:, :, MLA_NOPE:].reshape(MLA_KV_RANK, MLA_HEADS * MLA_V)], axis=1).astype(BF16)


def kernel(x, c, ctx, c_ctx, ada_w, ada_b, norm_mix_g, norm_ffn_g, w_mix_out, ffn_w_in, ffn_w_out, ev_w_in, gla_wa_f, gla_ba_f, gla_wa_b, gla_ba_b, gla_onorm_g, sg_vnorm_g, sg_ws, sg_bs, od_w_in, mla_qa_g, mla_wuq, mla_kva_g, mla_wukv, mla_qn_g, mla_kn_g):
    cvec = jnp.zeros((MOD_ROWS, D_MODEL), F32).at[:BATCH].set(c).at[CTX_MOD_ROW].set(c_ctx)
    mod = _ada_table(cvec, ada_w, ada_b)
    rope_tabs = _rope_tables()
    norm_mix_g = norm_mix_g.reshape(DEPTH, 1, D_MODEL)
    norm_ffn_g = norm_ffn_g.reshape(DEPTH, 1, D_MODEL)
    w_in_stacks = (_even_w_in(ev_w_in), _odd_w_in(od_w_in))
    dense_f32 = (w_mix_out, ffn_w_in, ffn_w_out)
    dense = None

    p, xa = _first_inproj(x, ctx, mod, norm_mix_g, w_in_stacks[0])
    for l in range(DEPTH):
        last = l == DEPTH - 1
        i = l // 2
        if l % 2 == 0:
            ma = _gla(p, _gate_w(gla_wa_f[i], 0), gla_ba_f[i].reshape(1, GLA_QK_W),
                      _gate_w(gla_wa_b[i], GLA_GATE_RANK), gla_ba_b[i].reshape(1, GLA_QK_W), gla_onorm_g[i])
            mb, first_dense = _spatial_gating(p, sg_vnorm_g[i], sg_ws[i].astype(BF16), sg_bs[i],
                                              dense_f32 if l == 0 else ())
            dense = first_dense if l == 0 else dense
        else:
            q_t, k, v_t = _mla_prep(p, rope_tabs, mla_qa_g[i], mla_wuq[i], mla_kva_g[i], mla_wukv[i],
                                    mla_qn_g[i], mla_kn_g[i])
            ma = _fourier(p, not last)
            mb = _attention(q_t, k, v_t, not last)
        nxt = None if last else (norm_mix_g, w_in_stacks[(l + 1) % 2], (l + 1) // 2, dense_f32)
        xa, p, dense = _layer_tail(xa, ma, mb, mod, l, norm_ffn_g, dense, nxt)
    return xa.reshape(BATCH, SEQ, D_MODEL)
```

```python
import functools

import numpy as np
import jax
import jax.numpy as jnp
from jax import lax
from jax.experimental import pallas as pl
from jax.experimental.pallas import tpu as pltpu

D_MODEL = 1024
BATCH = 8
SEQ = 2048
DEPTH = 4
CTX_LEN = 256
GRID_W = 64
RMS_EPS = 1e-6
GLA_HEADS = 4
GLA_DK = 64
GLA_DV = 128
GLA_GATE_RANK = 16
GLA_GATE_NORM = 16.0
GLA_CHUNK = 64
SG_GROUPS = 4
SG_DIM = 128
SG_CHUNK = 128
FT_GROUPS = 4
FT_DIM = 64
MLA_HEADS = 6
MLA_NOPE = 128
MLA_ROPE = 64
MLA_V = 128
MLA_Q_RANK = 384
MLA_KV_RANK = 256
ROPE_BASE = 10000.0
FFN_HIDDEN = 2816

TOK = CTX_LEN + SEQ
NTOK = BATCH * TOK
ROW_TILE = 256
TILES_PER_BATCH = TOK // ROW_TILE
CTX_TILES = CTX_LEN // ROW_TILE
MOD_ROWS = 16
CTX_MOD_ROW = BATCH
LANE = 128
EV_W = 2688
OD_W = 1024
MLA_QK = MLA_NOPE + MLA_ROPE
VMEM_LIMIT = 56 * 1024 * 1024

F32 = jnp.float32
BF16 = jnp.bfloat16


def _dot(a, b):
    return jnp.dot(a, b, preferred_element_type=F32)


def _rms(x, g):
    return x * lax.rsqrt(jnp.mean(x * x, axis=-1, keepdims=True) + RMS_EPS) * g


def _lane_rms(x, g):
    width = x.shape[1]
    ss = _dot((x * x).astype(BF16), jnp.ones((width, LANE), BF16))
    inv = lax.rsqrt(ss * (1.0 / width) + RMS_EPS)
    return x * jnp.concatenate([inv] * (width // LANE), axis=1) * g


def _params(sem):
    return pltpu.CompilerParams(dimension_semantics=sem, vmem_limit_bytes=VMEM_LIMIT)


def _resident(shape):
    zeros = (0,) * len(shape)
    return pl.BlockSpec(shape, lambda *_: zeros, pipeline_mode=pl.Buffered(1))


def _mod_row(i):
    return jnp.where(i % TILES_PER_BATCH < CTX_TILES, CTX_MOD_ROW, i // TILES_PER_BATCH)


MOD_PIECES = 6


def _mod_spec(layer, tile_of_step):
    def index_map(i):
        return (layer * MOD_ROWS + _mod_row(tile_of_step(i)), 0, 0)
    return pl.BlockSpec((1, MOD_PIECES, D_MODEL), index_map)


def _ada_kernel(c_ref, w_ref, b_ref, o_ref):
    c = c_ref[...]
    s = c * jax.nn.sigmoid(c)
    w = w_ref[0]
    w_hi = w.astype(BF16)
    w_lo = (w - w_hi.astype(F32)).astype(BF16)
    s_hi = s.astype(BF16)
    s_lo = (s - s_hi.astype(F32)).astype(BF16)
    o_ref[0] = _dot(s_hi, w_hi) + (_dot(s_hi, w_lo) + _dot(s_lo, w_hi)) + b_ref[0]


def _ada_table(cvec, ada_w, ada_b):
    tn = 3072
    n = MOD_PIECES * D_MODEL
    out = pl.pallas_call(
        _ada_kernel,
        out_shape=jax.ShapeDtypeStruct((DEPTH, MOD_ROWS, n), F32),
        grid=(DEPTH, n // tn),
        in_specs=[pl.BlockSpec((MOD_ROWS, D_MODEL), lambda l, j: (0, 0)),
                  pl.BlockSpec((1, D_MODEL, tn), lambda l, j: (l, 0, j)),
                  pl.BlockSpec((1, 1, tn), lambda l, j: (l, 0, j))],
        out_specs=pl.BlockSpec((1, MOD_ROWS, tn), lambda l, j: (l, 0, j)),
        compiler_params=_params(("arbitrary", "arbitrary")),
        name="ada_table",
    )(cvec, ada_w, ada_b.reshape(DEPTH, 1, n))
    return out.reshape(DEPTH * MOD_ROWS, MOD_PIECES, D_MODEL)


def _layer_block(shape, layer):
    zeros = (0,) * len(shape)
    return pl.BlockSpec((1,) + tuple(shape), lambda *_: (layer,) + zeros, pipeline_mode=pl.Buffered(1))


BF16_SUBLANES = 16


def _copy_plumbing(stacks, layer, n_steps):
    views, in_specs, shapes, out_specs = [], [], [], []
    for w in stacks:
        n_layers, n_rows, cols = w.shape
        rows = next(r for r in range(BF16_SUBLANES, n_rows + 1, BF16_SUBLANES)
                    if n_rows % r == 0 and n_rows // r <= n_steps)
        n_slices = n_rows // rows
        views.append(w.reshape(n_layers * n_rows, cols))
        in_specs.append(pl.BlockSpec(
            (rows, cols), lambda i, n=n_slices: (layer * n + jnp.minimum(i, n - 1), 0)))
        shapes.append(jax.ShapeDtypeStruct((n_rows, cols), BF16))
        out_specs.append(pl.BlockSpec((rows, cols), lambda i, n=n_slices: (jnp.minimum(i, n - 1), 0)))
    return views, in_specs, shapes, out_specs


def _as_layer_stacks(copies, stacks):
    return [c.reshape(1, w.shape[1], w.shape[2]) for c, w in zip(copies, stacks)]


def _modulated_norm(x, g, shift, scale):
    return _rms(x, g) * (1.0 + scale) + shift


def _first_inproj_kernel(x_ref, ctx_ref, g_ref, mod_ref, w_ref, *rest):
    *copy_refs, p_ref, xa_ref, mix_copy, in_copy, out_copy = rest
    is_ctx = pl.program_id(0) % TILES_PER_BATCH < CTX_TILES
    x = jnp.where(is_ctx, ctx_ref[...], x_ref[...])
    xa_ref[...] = x
    mod = mod_ref[0]
    z = _modulated_norm(x, g_ref[0], mod[0:1], mod[1:2])
    p_ref[...] = _dot(z.astype(BF16), w_ref[0]).astype(p_ref.dtype)
    for src, dst in zip(copy_refs, (mix_copy, in_copy, out_copy)):
        dst[...] = src[...].astype(BF16)


def _first_inproj(x, ctx, mod, norm_g, w, dense_f32):
    width = w.shape[-1]
    n_steps = NTOK // ROW_TILE
    views, copy_in, copy_shapes, copy_out = _copy_plumbing(dense_f32, 0, n_steps)
    x_tiles = SEQ // ROW_TILE
    x_tile = lambda i: ((i // TILES_PER_BATCH) * x_tiles + jnp.maximum(i % TILES_PER_BATCH - CTX_TILES, 0), 0)
    ctx_tile = lambda i: ((i // TILES_PER_BATCH) * CTX_TILES + jnp.minimum(i % TILES_PER_BATCH, CTX_TILES - 1), 0)
    ident = lambda i: i
    row_out = lambda wd: pl.BlockSpec((ROW_TILE, wd), lambda i: (i, 0))
    p, xa, *copies = pl.pallas_call(
        _first_inproj_kernel,
        out_shape=(jax.ShapeDtypeStruct((NTOK, width), BF16), jax.ShapeDtypeStruct((NTOK, D_MODEL), F32),
                   *copy_shapes),
        grid=(n_steps,),
        in_specs=[pl.BlockSpec((ROW_TILE, D_MODEL), x_tile),
                  pl.BlockSpec((ROW_TILE, D_MODEL), ctx_tile),
                  _layer_block((1, D_MODEL), 0),
                  _mod_spec(0, ident),
                  _layer_block((D_MODEL, width), 0),
                  *copy_in],
        out_specs=(row_out(width), row_out(D_MODEL), *copy_out),
        compiler_params=_params(("arbitrary",)),
        name="first_inproj",
    )(x.reshape(BATCH * SEQ, D_MODEL), ctx.reshape(BATCH * CTX_LEN, D_MODEL), norm_g, mod, w, *views)
    return p, xa, _as_layer_stacks(copies, dense_f32)


def _tail_kernel(x_ref, ma_ref, mb_ref, mod_ref, g_ref, wmix_ref, win_ref, wout_ref, *rest, has_next):
    mod = mod_ref[0]
    m = jnp.concatenate([ma_ref[...], mb_ref[...]], axis=1)
    x1 = x_ref[...] + mod[2:3] * _dot(m, wmix_ref[0])
    h = _modulated_norm(x1, g_ref[0], mod[3:4], mod[4:5]).astype(BF16)
    gu = _dot(h, win_ref[0])
    gate = gu[:, :FFN_HIDDEN]
    act = (gate * jax.nn.sigmoid(gate) * gu[:, FFN_HIDDEN:]).astype(BF16)
    x2 = x1 + mod[5:6] * _dot(act, wout_ref[0])
    if has_next:
        next_mod_ref, gn_ref, wn_ref, *copy_refs, o_ref, p_ref, mix_copy, in_copy, out_copy = rest
        next_mod = next_mod_ref[0]
        z = _modulated_norm(x2, gn_ref[0], next_mod[0:1], next_mod[1:2])
        p_ref[...] = _dot(z.astype(BF16), wn_ref[0]).astype(p_ref.dtype)
        for src, dst in zip(copy_refs, (mix_copy, in_copy, out_copy)):
            dst[...] = src[...].astype(BF16)
    else:
        (o_ref,) = rest
    o_ref[...] = x2


def _layer_tail(xa, ma, mb, mod, layer, norm_ffn_g, dense, nxt):
    wmix, win, wout = dense
    if nxt is None:
        x_tiles = SEQ // ROW_TILE
        n_steps = BATCH * x_tiles
        tile = lambda i: (i // x_tiles) * TILES_PER_BATCH + CTX_TILES + i % x_tiles
        out_rows = BATCH * SEQ
    else:
        n_steps = NTOK // ROW_TILE
        tile = lambda i: i
        out_rows = NTOK
    row_in = lambda width: pl.BlockSpec((ROW_TILE, width), lambda i: (tile(i), 0))
    row_out = lambda width: pl.BlockSpec((ROW_TILE, width), lambda i: (i, 0))
    in_specs = [row_in(D_MODEL), row_in(ma.shape[1]), row_in(mb.shape[1]),
                _mod_spec(layer, tile),
                _layer_block((1, D_MODEL), layer),
                _layer_block((D_MODEL, D_MODEL), 0),
                _layer_block((D_MODEL, 2 * FFN_HIDDEN), 0),
                _layer_block((FFN_HIDDEN, D_MODEL), 0)]
    args = [xa, ma, mb, mod, norm_ffn_g, wmix, win, wout]
    out_shape = [jax.ShapeDtypeStruct((out_rows, D_MODEL), F32)]
    out_specs = [row_out(D_MODEL)]
    if nxt is not None:
        norm_mix_g, w_next, idx, dense_f32 = nxt
        width = w_next.shape[-1]
        views, copy_in, copy_shapes, copy_out = _copy_plumbing(dense_f32, layer + 1, n_steps)
        in_specs += [_mod_spec(layer + 1, tile),
                     _layer_block((1, D_MODEL), layer + 1), _layer_block((D_MODEL, width), idx)] + copy_in
        args += [mod, norm_mix_g, w_next] + views
        out_shape += [jax.ShapeDtypeStruct((out_rows, width), BF16)] + copy_shapes
        out_specs += [row_out(width)] + copy_out
    res = pl.pallas_call(
        functools.partial(_tail_kernel, has_next=nxt is not None),
        out_shape=tuple(out_shape),
        grid=(n_steps,),
        in_specs=in_specs,
        out_specs=tuple(out_specs),
        compiler_params=_params(("arbitrary",)),
        name="layer_tail",
    )(*args)
    if nxt is None:
        return res[0], None, None
    return res[0], res[1], _as_layer_stacks(res[2:], dense_f32)


GLA_QK_W = GLA_HEADS * GLA_DK
GLA_V_W = GLA_HEADS * GLA_DV
GLA_MAIN_W = 2 * GLA_QK_W + 2 * GLA_V_W
GLA_NCHUNK = TOK // GLA_CHUNK


GLA_BLOCK = 256
GLA_NBLOCK = TOK // GLA_BLOCK
GLA_CTX_BLOCKS = CTX_LEN // GLA_BLOCK
GLA_UNROLL = GLA_NBLOCK


def _log_sigmoid(y):
    return jnp.minimum(y, 0.0) - jnp.log(1.0 + jnp.exp(-jnp.abs(y)))


def _gla_kernel(main_ref, gate_ref, waf_ref, baf_ref, wab_ref, bab_ref, og_ref, o_ref,
                of_scr, qd_scr, kd_scr, qe_scr, kl_scr, dec_scr, s_scr):
    C, BLK = GLA_CHUNK, GLA_BLOCK
    cpb = BLK // C
    rb = lax.broadcasted_iota(jnp.int32, (BLK, BLK), 0)
    cb = lax.broadcasted_iota(jnp.int32, (BLK, BLK), 1)
    same_chunk = (rb // C) == (cb // C)
    tri_f = jnp.where(same_chunk & (rb >= cb), 1.0, 0.0).astype(BF16)
    tri_b = jnp.where(same_chunk & (rb <= cb), 1.0, 0.0).astype(BF16)
    r = lax.broadcasted_iota(jnp.int32, (C, C), 0)
    cc = lax.broadcasted_iota(jnp.int32, (C, C), 1)
    lower, upper = r >= cc, r <= cc
    sr = lax.broadcasted_iota(jnp.int32, (BLK, cpb * LANE), 0)
    sc = lax.broadcasted_iota(jnp.int32, (BLK, cpb * LANE), 1)
    chunk_sum = jnp.where(sr // C == sc // LANE, 1.0, 0.0).astype(BF16)

    def split(x):
        hi = x.astype(BF16)
        return hi, (x - hi.astype(F32)).astype(BF16)

    def dot_01(m, x):
        hi, lo = split(x)
        return _dot(jnp.concatenate([m, m], axis=1), jnp.concatenate([hi, lo], axis=0))

    def dot_10(x, m):
        hi, lo = split(x)
        return _dot(jnp.concatenate([hi, lo], axis=1), jnp.concatenate([m, m], axis=0))

    def block_rows(blk):
        return pl.ds(pl.multiple_of(blk * BLK, BLK), BLK)

    def chunk_rows(blk, j):
        return pl.ds(pl.multiple_of(blk * BLK + j * C, C), C)

    def dec_rows(blk, j):
        return pl.ds(pl.multiple_of((blk * cpb + j) * GLA_QK_W, GLA_QK_W), GLA_QK_W)

    def decay_terms(a, q, k, w_ref, bias_ref, tri, end_row):
        g = _log_sigmoid(_dot(a, w_ref[...]) + bias_ref[...]) * (1.0 / GLA_GATE_NORM)
        b = dot_01(tri, g)
        dec_all = jnp.exp(dot_10(g.T, chunk_sum))
        terms = []
        for j in range(cpb):
            sl = slice(j * C, (j + 1) * C)
            bc = b[sl]
            b_mid = bc[C // 2:C // 2 + 1]
            b_end = bc[end_row:end_row + 1]
            d = bc - b_mid
            qd = q[sl] * jnp.exp(d)
            kd = k[sl] * jnp.exp(-d)
            qe = qd * jnp.exp(b_mid)
            kl = kd * jnp.exp(b_end - b_mid)
            dec = dec_all[:, j * LANE:(j + 1) * LANE]
            terms.append((qd.astype(BF16), kd.T.astype(BF16), qe.astype(BF16), kl.T.astype(BF16), dec))
        return terms

    def step(qd, kd_t, qe, kl_t, dec, v, keep):
        outs = []
        for h in range(GLA_HEADS):
            ks = slice(h * GLA_DK, (h + 1) * GLA_DK)
            vh = v[:, h * GLA_DV:(h + 1) * GLA_DV]
            s_h = s_scr[ks, :]
            att = jnp.where(keep, _dot(qd[:, ks], kd_t[ks, :]), 0.0).astype(BF16)
            outs.append(_dot(att, vh) + _dot(qe[:, ks], s_h.astype(BF16)))
            s_scr[ks, :] = s_h * dec[ks, :] + _dot(kl_t[ks, :], vh)
        return outs

    s_scr[...] = jnp.zeros(s_scr.shape, F32)

    def loop1(blk, carry):
        rows = block_rows(blk)
        a = gate_ref[rows, :]
        q = main_ref[rows, 0:GLA_QK_W].astype(F32) * (GLA_DK ** -0.5)
        k = main_ref[rows, GLA_QK_W:2 * GLA_QK_W].astype(F32)
        v = main_ref[rows, 2 * GLA_QK_W:2 * GLA_QK_W + GLA_V_W]
        fwd = decay_terms(a, q, k, waf_ref, baf_ref, tri_f, C - 1)
        bwd = decay_terms(a, q, k, wab_ref, bab_ref, tri_b, 0)
        for j in range(cpb):
            cr = chunk_rows(blk, j)
            qd, kd, qe, kl, dec = bwd[j]
            qd_scr[cr, :] = qd
            kd_scr[dec_rows(blk, j), :] = kd
            qe_scr[cr, :] = qe
            kl_scr[dec_rows(blk, j), :] = kl
            dec_scr[dec_rows(blk, j), :] = dec
            qd, kd, qe, kl, dec = fwd[j]
            outs = step(qd, kd, qe, kl, dec, v[j * C:(j + 1) * C], lower)
            of_scr[cr, :] = jnp.concatenate(outs, axis=1)
        return carry

    lax.fori_loop(0, GLA_NBLOCK, loop1, 0, unroll=GLA_UNROLL)

    s_scr[...] = jnp.zeros(s_scr.shape, F32)

    def loop2(i, carry):
        blk = jnp.where(i < GLA_CTX_BLOCKS, GLA_CTX_BLOCKS - 1 - i, GLA_NBLOCK - 1 - (i - GLA_CTX_BLOCKS))
        for j in reversed(range(cpb)):
            cr = chunk_rows(blk, j)
            v = main_ref[cr, 2 * GLA_QK_W:2 * GLA_QK_W + GLA_V_W]
            kr = dec_rows(blk, j)
            outs = step(qd_scr[cr, :], kd_scr[kr, :], qe_scr[cr, :], kl_scr[kr, :], dec_scr[kr, :], v, upper)
            gout = main_ref[cr, 2 * GLA_QK_W + GLA_V_W:GLA_MAIN_W].astype(F32)
            res = []
            for h in range(GLA_HEADS):
                vs = slice(h * GLA_DV, (h + 1) * GLA_DV)
                o = outs[h] + of_scr[cr, vs]
                gh = gout[:, vs]
                res.append(_rms(o, og_ref[...]) * (gh * jax.nn.sigmoid(gh)))
            o_ref[cr, :] = jnp.concatenate(res, axis=1).astype(o_ref.dtype)
        return carry

    lax.fori_loop(0, GLA_NBLOCK, loop2, 0, unroll=GLA_UNROLL)


def _gla(p, waf, baf, wab, bab, onorm_g):
    gate_col = (EV_W - LANE) // LANE
    qk_scr = pltpu.VMEM((TOK, GLA_QK_W), BF16)
    qk_t_scr = pltpu.VMEM((GLA_NCHUNK * GLA_QK_W, GLA_CHUNK), BF16)
    return pl.pallas_call(
        _gla_kernel,
        out_shape=jax.ShapeDtypeStruct((NTOK, GLA_V_W), BF16),
        grid=(BATCH,),
        in_specs=[pl.BlockSpec((TOK, GLA_MAIN_W), lambda b: (b, 0)),
                  pl.BlockSpec((TOK, LANE), lambda b: (b, gate_col)),
                  _resident((LANE, GLA_QK_W)), _resident((1, GLA_QK_W)),
                  _resident((LANE, GLA_QK_W)), _resident((1, GLA_QK_W)),
                  _resident((1, GLA_DV))],
        out_specs=pl.BlockSpec((TOK, GLA_V_W), lambda b: (b, 0)),
        scratch_shapes=[pltpu.VMEM((TOK, GLA_V_W), F32), qk_scr, qk_t_scr, qk_scr, qk_t_scr,
                        pltpu.VMEM((GLA_NCHUNK * GLA_QK_W, LANE), F32),
                        pltpu.VMEM((GLA_QK_W, GLA_DV), F32)],
        compiler_params=_params(("parallel",)),
        name="gla",
    )(p, p, waf, baf, wab, bab, onorm_g.reshape(1, GLA_DV))


def _gelu(x):
    return 0.5 * x * (1.0 + lax.erf(x * (2.0 ** -0.5)))


SG_TILE = 1152


def _sg_kernel(u_ref, v_ref, vg_ref, ws_ref, bs_ref, o_ref):
    for c in range(SG_TILE // SG_CHUNK):
        rows = slice(c * SG_CHUNK, (c + 1) * SG_CHUNK)
        for g in range(SG_GROUPS):
            cols = slice(g * SG_DIM, (g + 1) * SG_DIM)
            u = _gelu(u_ref[rows, cols].astype(F32))
            vn = _lane_rms(_gelu(v_ref[rows, cols].astype(F32)), vg_ref[g:g + 1, :])
            mixed = _dot(ws_ref[g], vn.astype(BF16)) + bs_ref[g]
            o_ref[rows, cols] = (u * mixed).astype(o_ref.dtype)


def _spatial_gating(p, vnorm_g, ws, bs):
    width = SG_GROUPS * SG_DIM
    u_col = GLA_MAIN_W // width
    bias = jnp.broadcast_to(bs[:, :, None], (SG_GROUPS, SG_CHUNK, SG_DIM))
    return pl.pallas_call(
        _sg_kernel,
        out_shape=jax.ShapeDtypeStruct((NTOK, width), BF16),
        grid=(NTOK // SG_TILE,),
        in_specs=[pl.BlockSpec((SG_TILE, width), lambda i: (i, u_col)),
                  pl.BlockSpec((SG_TILE, width), lambda i: (i, u_col + 1)),
                  _resident((SG_GROUPS, SG_DIM)),
                  _resident((SG_GROUPS, SG_CHUNK, SG_CHUNK)),
                  _resident((SG_GROUPS, SG_CHUNK, SG_DIM))],
        out_specs=pl.BlockSpec((SG_TILE, width), lambda i: (i, 0)),
        compiler_params=_params(("parallel",)),
        name="spatial_gating",
    )(p, p, vnorm_g, ws, bias)


FT_W = FT_GROUPS * FT_DIM


def _dft_cos_sin(n):
    jk = np.outer(np.arange(n), np.arange(n)) % n
    ang = 2.0 * np.pi * jk.astype(np.float64) / n
    return np.cos(ang), np.sin(ang)


def _parity_dft(n):
    p = np.arange(n // 2)[:, None]
    mats = []
    for first in (0, 1):
        t = 2 * np.arange(n // 2)[None, :] + first
        ang = 2.0 * np.pi * ((p * t) % n).astype(np.float64) / n
        mats.append(np.concatenate([np.cos(ang), -np.sin(ang)], axis=1))
    return jnp.asarray(np.stack(mats), F32).astype(BF16)


def _fourier_kernel(h_ref, cs_ref, dft_x_ref, dft_c_ref, o_ref, ab_scr, *, need_ctx):
    ab = _dot(h_ref[...], cs_ref[...])
    n_slab = 2 * FT_W // LANE
    for c in range(n_slab):
        ab_scr[c] = ab[:, c * LANE:(c + 1) * LANE]

    def position_dft(first_row, length, dft_ref):
        half = length // 2
        parts = []
        for parity in (0, 1):
            slabs = [ab_scr[c, pl.ds(first_row + parity, half, stride=2), :].astype(BF16) for c in range(n_slab)]
            a = jnp.concatenate(slabs[:n_slab // 2], axis=1)
            b = jnp.concatenate(slabs[n_slab // 2:], axis=1)
            parts.append(_dot(dft_ref[parity], jnp.concatenate([a, b], axis=0)))
        scale = (length * FT_DIM) ** -0.5
        o_ref[first_row:first_row + half, :] = ((parts[0] + parts[1]) * scale).astype(o_ref.dtype)
        o_ref[first_row + half:first_row + length, :] = ((parts[0] - parts[1]) * scale).astype(o_ref.dtype)

    position_dft(CTX_LEN, SEQ, dft_x_ref)
    if need_ctx:
        position_dft(0, CTX_LEN, dft_c_ref)
    else:
        o_ref[:CTX_LEN, :] = jnp.zeros((CTX_LEN, FT_W), o_ref.dtype)


def _fourier(p, need_ctx):
    cc, sc = _dft_cos_sin(FT_DIM)
    eye = np.eye(FT_GROUPS)
    cs = jnp.asarray(np.concatenate([np.kron(eye, cc), np.kron(eye, sc)], axis=1), F32).astype(BF16)
    return pl.pallas_call(
        functools.partial(_fourier_kernel, need_ctx=need_ctx),
        out_shape=jax.ShapeDtypeStruct((NTOK, FT_W), BF16),
        grid=(BATCH,),
        in_specs=[pl.BlockSpec((TOK, FT_W), lambda b: (b, 0)),
                  _resident((FT_W, 2 * FT_W)),
                  _resident((2, SEQ // 2, SEQ)), _resident((2, CTX_LEN // 2, CTX_LEN))],
        out_specs=pl.BlockSpec((TOK, FT_W), lambda b: (b, 0)),
        scratch_shapes=[pltpu.VMEM((2 * FT_W // LANE, TOK, LANE), F32)],
        compiler_params=_params(("parallel",)),
        name="fourier",
    )(p, cs, _parity_dft(SEQ), _parity_dft(CTX_LEN))


MLA_NOPE_W = MLA_HEADS * MLA_NOPE
MLA_ROPE_W = MLA_HEADS * MLA_ROPE
OD_QA0 = FT_W
OD_KVA0 = OD_QA0 + MLA_Q_RANK
OD_KPE0 = OD_KVA0 + MLA_KV_RANK
OD_KPE_ROT0 = OD_KPE0 + MLA_ROPE


def _rot_cols(w):
    q = MLA_ROPE // 4
    return jnp.concatenate([-w[..., q:2 * q], w[..., 0:q], -w[..., 3 * q:4 * q], w[..., 2 * q:3 * q]], axis=-1)


def _rot_gain(g):
    q = MLA_ROPE // 4
    return jnp.concatenate([g[..., q:2 * q], g[..., 0:q], g[..., 3 * q:4 * q], g[..., 2 * q:3 * q]], axis=-1)


def _rope_tables():
    rows = SEQ // GRID_W
    row_id = np.repeat(np.arange(rows, dtype=np.float32), GRID_W)
    col_id = np.tile(np.arange(GRID_W, dtype=np.float32), rows)
    axis_dim = MLA_ROPE // 2
    inv_freq = (np.float32(ROPE_BASE) ** (-np.arange(0, axis_dim, 2, dtype=np.float32) / np.float32(axis_dim))).astype(np.float32)
    ang_r = row_id[:, None] * inv_freq
    ang_c = col_id[:, None] * inv_freq
    ang = np.concatenate([ang_r, ang_r, ang_c, ang_c], axis=-1).astype(np.float32)
    cos = np.concatenate([np.ones((CTX_LEN, MLA_ROPE), np.float32), np.cos(ang)], axis=0)
    sin = np.concatenate([np.zeros((CTX_LEN, MLA_ROPE), np.float32), np.sin(ang)], axis=0)
    return tuple(jnp.asarray(t, F32) for t in (cos, sin, cos.T, sin.T))


MLA_VT_ROWS = MLA_V + 16


def _mla_heads(p, cos, sin, cos_t, sin_t, qag, wuq_t, kvag, wukv_k, wukv_vt, head_sum, kg,
               qg_nope, qg_rope, qg_rot, qt_ref, k_ref, vt_ref):
    qa = _lane_rms(p[:, OD_QA0:OD_KVA0].astype(F32), qag)
    q_t = _dot(wuq_t, qa.T.astype(BF16))
    for h in range(MLA_HEADS):
        qn = q_t[h * MLA_NOPE:(h + 1) * MLA_NOPE]
        qr = q_t[MLA_NOPE_W + h * MLA_ROPE:MLA_NOPE_W + (h + 1) * MLA_ROPE]
        qrr = q_t[MLA_NOPE_W + MLA_ROPE_W + h * MLA_ROPE:MLA_NOPE_W + MLA_ROPE_W + (h + 1) * MLA_ROPE]
        ss = jnp.sum(qn * qn, axis=0, keepdims=True) + jnp.sum(qr * qr, axis=0, keepdims=True)
        inv = lax.rsqrt(ss * (1.0 / MLA_QK) + RMS_EPS) * (MLA_QK ** -0.5)
        q_rope = qr * qg_rope * cos_t + qrr * qg_rot * sin_t
        qt_ref[h] = jnp.concatenate([qn * qg_nope * inv, q_rope * inv], axis=0).astype(qt_ref.dtype)

    kva = _lane_rms(p[:, OD_KVA0:OD_KPE0].astype(F32), kvag)
    v_t = _dot(wukv_vt, kva.T.astype(BF16))
    ones = jnp.ones((MLA_VT_ROWS - MLA_V, v_t.shape[1]), F32)
    for h in range(MLA_HEADS):
        vt_ref[h] = jnp.concatenate([v_t[h * MLA_V:(h + 1) * MLA_V], ones], axis=0).astype(vt_ref.dtype)
    kn = _dot(kva.astype(BF16), wukv_k)
    kpe = p[:, OD_KPE0:OD_KPE_ROT0].astype(F32)
    kpe_rot = p[:, OD_KPE_ROT0:OD_W].astype(F32)
    sq = jnp.concatenate([kn * kn, kpe * kpe], axis=1).astype(BF16)
    ss = _dot(sq, head_sum)
    inv = lax.rsqrt(ss * (1.0 / MLA_QK) + RMS_EPS)
    k_rope = kpe * kg[:, MLA_NOPE:MLA_QK] * cos + kpe_rot * kg[:, MLA_QK:] * sin
    for h in range(MLA_HEADS):
        k_h = jnp.concatenate([kn[:, h * MLA_NOPE:(h + 1) * MLA_NOPE] * kg[:, :MLA_NOPE], k_rope], axis=1)
        k_ref[h] = (k_h * inv[:, h:h + 1]).astype(k_ref.dtype)


def _key_head_sum_matrix():
    head = np.arange(LANE)[None, :]
    nope = np.repeat(np.arange(MLA_HEADS), MLA_NOPE)[:, None] == head
    rope = np.broadcast_to(head < MLA_HEADS, (MLA_ROPE, LANE))
    return jnp.asarray(np.concatenate([nope, rope], axis=0), BF16)


MLA_TILE = 768


def _mla_prep_kernel(p_ref, *refs):
    *operand_refs, qt_ref, k_ref, vt_ref = refs
    _mla_heads(p_ref[...], *(r[...] for r in operand_refs), qt_ref, k_ref, vt_ref)


def _mla_prep(p, rope_tabs, qa_g, wuq, kva_g, wukv, qn_g, kn_g):
    wukv = _wukv_cols(wukv)
    kgain = jnp.concatenate([kn_g, _rot_gain(kn_g[MLA_NOPE:])]).reshape(1, -1)
    col = lambda g: jnp.broadcast_to(g[:, None], (g.shape[0], MLA_TILE))
    operands = [*rope_tabs, qa_g.reshape(1, -1), _wuq_cols(wuq).T, kva_g.reshape(1, -1),
                wukv[:, :MLA_NOPE_W], wukv[:, MLA_NOPE_W:].T, _key_head_sum_matrix(), kgain,
                col(qn_g[:MLA_NOPE]), col(qn_g[MLA_NOPE:]), col(_rot_gain(qn_g[MLA_NOPE:]))]
    tiles_per_batch = TOK // MLA_TILE
    rope = pl.BlockSpec((MLA_TILE, MLA_ROPE), lambda i: (i % tiles_per_batch, 0))
    rope_t = pl.BlockSpec((MLA_ROPE, MLA_TILE), lambda i: (0, i % tiles_per_batch))
    rows = lambda d: pl.BlockSpec((MLA_HEADS, MLA_TILE, d), lambda i: (0, i, 0))
    cols = lambda d: pl.BlockSpec((MLA_HEADS, d, MLA_TILE), lambda i: (0, 0, i))
    return pl.pallas_call(
        _mla_prep_kernel,
        out_shape=(jax.ShapeDtypeStruct((MLA_HEADS, MLA_QK, NTOK), BF16),
                   jax.ShapeDtypeStruct((MLA_HEADS, NTOK, MLA_QK), BF16),
                   jax.ShapeDtypeStruct((MLA_HEADS, MLA_VT_ROWS, NTOK), BF16)),
        grid=(NTOK // MLA_TILE,),
        in_specs=[pl.BlockSpec((MLA_TILE, OD_W), lambda i: (i, 0)), rope, rope, rope_t, rope_t]
                 + [_resident(a.shape) for a in operands[4:]],
        out_specs=(cols(MLA_QK), rows(MLA_QK), cols(MLA_VT_ROWS)),
        compiler_params=_params(("parallel",)),
        name="mla_prep",
    )(p, *operands)


ATT_Q_TILE = 512


ATT_KEY_CHUNK = TOK


def _weighted_values(acc):
    return (acc[:MLA_V] * (1.0 / acc[MLA_V:MLA_V + 1])).T


def _attn_kernel(qt_ref, k_ref, vt_ref, o_ref, s_scr, *, need_ctx):
    n_q = SEQ // ATT_Q_TILE
    chunks = [slice(c * ATT_KEY_CHUNK, (c + 1) * ATT_KEY_CHUNK) for c in range(TOK // ATT_KEY_CHUNK)]
    q_cols = lambda j: slice(CTX_LEN + j * ATT_Q_TILE, CTX_LEN + (j + 1) * ATT_Q_TILE)

    def scores(j, rows, col_max):
        s = _dot(k_ref[0, rows, :], qt_ref[0, :, q_cols(j)])
        s_scr[j % 2, rows, :] = s
        m = jnp.max(s, axis=0, keepdims=True)
        return m if col_max is None else jnp.maximum(col_max, m)

    col_max = None
    for rows in chunks:
        col_max = scores(0, rows, col_max)
    for j in range(n_q):
        acc, next_max = None, None
        for rows in chunks:
            if j + 1 < n_q:
                next_max = scores(j + 1, rows, next_max)
            p = jnp.exp(s_scr[j % 2, rows, :] - col_max).astype(BF16)
            part = _dot(vt_ref[0, :, rows], p)
            acc = part if acc is None else acc + part
        o_ref[q_cols(j), :] = _weighted_values(acc).astype(o_ref.dtype)
        col_max = next_max

    if need_ctx:
        s = _dot(k_ref[0, :CTX_LEN, :], qt_ref[0, :, :CTX_LEN])
        p = jnp.exp(s - jnp.max(s, axis=0, keepdims=True)).astype(BF16)
        o_ref[:CTX_LEN, :] = _weighted_values(_dot(vt_ref[0, :, :CTX_LEN], p)).astype(o_ref.dtype)
    else:
        o_ref[:CTX_LEN, :] = jnp.zeros((CTX_LEN, MLA_V), o_ref.dtype)


def _attention(q_t, k, v_t, need_ctx):
    transposed = lambda d: pl.BlockSpec((1, d, TOK), lambda b, h: (h, 0, b))
    return pl.pallas_call(
        functools.partial(_attn_kernel, need_ctx=need_ctx),
        out_shape=jax.ShapeDtypeStruct((NTOK, MLA_HEADS * MLA_V), BF16),
        grid=(BATCH, MLA_HEADS),
        in_specs=[transposed(MLA_QK),
                  pl.BlockSpec((1, TOK, MLA_QK), lambda b, h: (h, b, 0)),
                  transposed(MLA_VT_ROWS)],
        out_specs=pl.BlockSpec((TOK, MLA_V), lambda b, h: (b, h)),
        scratch_shapes=[pltpu.VMEM((2, TOK, ATT_Q_TILE), F32)],
        compiler_params=_params(("parallel", "parallel")),
        name="attention",
    )(q_t, k, v_t)


def _even_w_in(w):
    gates0 = GLA_MAIN_W
    gates1 = gates0 + 2 * GLA_GATE_RANK
    w = w.astype(BF16)
    pad = jnp.zeros(w.shape[:-1] + (EV_W - w.shape[-1],), w.dtype)
    return jnp.concatenate([w[..., :gates0], w[..., gates1:], w[..., gates0:gates1], pad], axis=-1)


def _gate_w(wa, first_row):
    out = jnp.zeros((LANE, GLA_QK_W), wa.dtype)
    return lax.dynamic_update_slice(out, wa, (first_row, 0)).astype(BF16)


def _odd_w_in(w):
    w = w.astype(BF16)
    return jnp.concatenate([w, _rot_cols(w[..., OD_KPE0:OD_KPE_ROT0])], axis=-1)


def _wuq_cols(w):
    w = w.reshape(MLA_Q_RANK, MLA_HEADS, MLA_QK)
    nope = w[:, :, :MLA_NOPE].reshape(MLA_Q_RANK, MLA_NOPE_W)
    rope = w[:, :, MLA_NOPE:]
    return jnp.concatenate([nope, rope.reshape(MLA_Q_RANK, MLA_ROPE_W),
                            _rot_cols(rope).reshape(MLA_Q_RANK, MLA_ROPE_W)], axis=1).astype(BF16)


def _wukv_cols(w):
    w = w.reshape(MLA_KV_RANK, MLA_HEADS, MLA_NOPE + MLA_V)
    return jnp.concatenate([w[:, :, :MLA_NOPE].reshape(MLA_KV_RANK, MLA_NOPE_W),
                            w[:, :, MLA_NOPE:].reshape(MLA_KV_RANK, MLA_HEADS * MLA_V)], axis=1).astype(BF16)


def kernel(x, c, ctx, c_ctx, ada_w, ada_b, norm_mix_g, norm_ffn_g, w_mix_out, ffn_w_in, ffn_w_out, ev_w_in, gla_wa_f, gla_ba_f, gla_wa_b, gla_ba_b, gla_onorm_g, sg_vnorm_g, sg_ws, sg_bs, od_w_in, mla_qa_g, mla_wuq, mla_kva_g, mla_wukv, mla_qn_g, mla_kn_g):
    cvec = jnp.zeros((MOD_ROWS, D_MODEL), F32).at[:BATCH].set(c).at[CTX_MOD_ROW].set(c_ctx)
    mod = _ada_table(cvec, ada_w, ada_b)
    rope_tabs = _rope_tables()
    norm_mix_g = norm_mix_g.reshape(DEPTH, 1, D_MODEL)
    norm_ffn_g = norm_ffn_g.reshape(DEPTH, 1, D_MODEL)
    w_in_stacks = (_even_w_in(ev_w_in), _odd_w_in(od_w_in))
    dense_f32 = (w_mix_out, ffn_w_in, ffn_w_out)

    p, xa, dense = _first_inproj(x, ctx, mod, norm_mix_g, w_in_stacks[0], dense_f32)
    for l in range(DEPTH):
        last = l == DEPTH - 1
        i = l // 2
        if l % 2 == 0:
            ma = _gla(p, _gate_w(gla_wa_f[i], 0), gla_ba_f[i].reshape(1, GLA_QK_W),
                      _gate_w(gla_wa_b[i], GLA_GATE_RANK), gla_ba_b[i].reshape(1, GLA_QK_W), gla_onorm_g[i])
            mb = _spatial_gating(p, sg_vnorm_g[i], sg_ws[i].astype(BF16), sg_bs[i])
        else:
            q_t, k, v_t = _mla_prep(p, rope_tabs, mla_qa_g[i], mla_wuq[i], mla_kva_g[i], mla_wukv[i],
                                    mla_qn_g[i], mla_kn_g[i])
            ma = _fourier(p, not last)
            mb = _attention(q_t, k, v_t, not last)
        nxt = None if last else (norm_mix_g, w_in_stacks[(l + 1) % 2], (l + 1) // 2, dense_f32)
        xa, p, dense = _layer_tail(xa, ma, mb, mod, l, norm_ffn_g, dense, nxt)
    return xa.reshape(BATCH, SEQ, D_MODEL)
```

```python
import functools

import numpy as np
import jax
import jax.numpy as jnp
from jax import lax
from jax.experimental import pallas as pl
from jax.experimental.pallas import tpu as pltpu

D_MODEL = 1024
BATCH = 8
SEQ = 2048
DEPTH = 4
CTX_LEN = 256
GRID_W = 64
RMS_EPS = 1e-6
GLA_HEADS = 4
GLA_DK = 64
GLA_DV = 128
GLA_GATE_RANK = 16
GLA_GATE_NORM = 16.0
GLA_CHUNK = 64
SG_GROUPS = 4
SG_DIM = 128
SG_CHUNK = 128
FT_GROUPS = 4
FT_DIM = 64
MLA_HEADS = 6
MLA_NOPE = 128
MLA_ROPE = 64
MLA_V = 128
MLA_Q_RANK = 384
MLA_KV_RANK = 256
ROPE_BASE = 10000.0
FFN_HIDDEN = 2816

TOK = CTX_LEN + SEQ
NTOK = BATCH * TOK
ROW_TILE = 256
TILES_PER_BATCH = TOK // ROW_TILE
CTX_TILES = CTX_LEN // ROW_TILE
MOD_ROWS = 16
CTX_MOD_ROW = BATCH
LANE = 128
EV_W = 2688
OD_W = 1024
MLA_QK = MLA_NOPE + MLA_ROPE
VMEM_LIMIT = 56 * 1024 * 1024

F32 = jnp.float32
BF16 = jnp.bfloat16


def _dot(a, b):
    return jnp.dot(a, b, preferred_element_type=F32)


def _rms(x, g):
    return x * lax.rsqrt(jnp.mean(x * x, axis=-1, keepdims=True) + RMS_EPS) * g


def _lane_rms(x, g):
    width = x.shape[1]
    ss = _dot((x * x).astype(BF16), jnp.ones((width, LANE), BF16))
    inv = lax.rsqrt(ss * (1.0 / width) + RMS_EPS)
    return x * jnp.concatenate([inv] * (width // LANE), axis=1) * g


def _params(sem):
    return pltpu.CompilerParams(dimension_semantics=sem, vmem_limit_bytes=VMEM_LIMIT)


def _resident(shape):
    zeros = (0,) * len(shape)
    return pl.BlockSpec(shape, lambda *_: zeros, pipeline_mode=pl.Buffered(1))


def _mod_row(i):
    return jnp.where(i % TILES_PER_BATCH < CTX_TILES, CTX_MOD_ROW, i // TILES_PER_BATCH)


MOD_PIECES = 6


def _mod_spec(layer, tile_of_step):
    def index_map(i):
        return (layer * MOD_ROWS + _mod_row(tile_of_step(i)), 0, 0)
    return pl.BlockSpec((1, MOD_PIECES, D_MODEL), index_map)


def _ada_kernel(c_ref, w_ref, b_ref, o_ref):
    c = c_ref[...]
    s = c * jax.nn.sigmoid(c)
    w = w_ref[0]
    w_hi = w.astype(BF16)
    w_lo = (w - w_hi.astype(F32)).astype(BF16)
    s_hi = s.astype(BF16)
    s_lo = (s - s_hi.astype(F32)).astype(BF16)
    o_ref[0] = _dot(s_hi, w_hi) + (_dot(s_hi, w_lo) + _dot(s_lo, w_hi)) + b_ref[0]


def _ada_table(cvec, ada_w, ada_b):
    tn = 3072
    n = MOD_PIECES * D_MODEL
    out = pl.pallas_call(
        _ada_kernel,
        out_shape=jax.ShapeDtypeStruct((DEPTH, MOD_ROWS, n), F32),
        grid=(DEPTH, n // tn),
        in_specs=[pl.BlockSpec((MOD_ROWS, D_MODEL), lambda l, j: (0, 0)),
                  pl.BlockSpec((1, D_MODEL, tn), lambda l, j: (l, 0, j)),
                  pl.BlockSpec((1, 1, tn), lambda l, j: (l, 0, j))],
        out_specs=pl.BlockSpec((1, MOD_ROWS, tn), lambda l, j: (l, 0, j)),
        compiler_params=_params(("arbitrary", "arbitrary")),
        name="ada_table",
    )(cvec, ada_w, ada_b.reshape(DEPTH, 1, n))
    return out.reshape(DEPTH * MOD_ROWS, MOD_PIECES, D_MODEL)


def _layer_block(shape, layer):
    zeros = (0,) * len(shape)
    return pl.BlockSpec((1,) + tuple(shape), lambda *_: (layer,) + zeros, pipeline_mode=pl.Buffered(1))


BF16_SUBLANES = 16


def _copy_plumbing(stacks, layer, n_steps):
    views, in_specs, shapes, out_specs = [], [], [], []
    for w in stacks:
        n_layers, n_rows, cols = w.shape
        rows = next(r for r in range(BF16_SUBLANES, n_rows + 1, BF16_SUBLANES)
                    if n_rows % r == 0 and n_rows // r <= n_steps)
        n_slices = n_rows // rows
        views.append(w.reshape(n_layers * n_rows, cols))
        in_specs.append(pl.BlockSpec(
            (rows, cols), lambda i, n=n_slices: (layer * n + jnp.minimum(i, n - 1), 0)))
        shapes.append(jax.ShapeDtypeStruct((n_rows, cols), BF16))
        out_specs.append(pl.BlockSpec((rows, cols), lambda i, n=n_slices: (jnp.minimum(i, n - 1), 0)))
    return views, in_specs, shapes, out_specs


def _as_layer_stacks(copies, stacks):
    return [c.reshape(1, w.shape[1], w.shape[2]) for c, w in zip(copies, stacks)]


def _modulated_norm(x, g, shift, scale):
    return _rms(x, g) * (1.0 + scale) + shift


def _first_inproj_kernel(x_ref, ctx_ref, g_ref, mod_ref, w_ref, *rest):
    *copy_refs, p_ref, xa_ref, mix_copy, in_copy, out_copy = rest
    is_ctx = pl.program_id(0) % TILES_PER_BATCH < CTX_TILES
    x = jnp.where(is_ctx, ctx_ref[...], x_ref[...])
    xa_ref[...] = x
    mod = mod_ref[0]
    z = _modulated_norm(x, g_ref[0], mod[0:1], mod[1:2])
    p_ref[...] = _dot(z.astype(BF16), w_ref[0]).astype(p_ref.dtype)
    for src, dst in zip(copy_refs, (mix_copy, in_copy, out_copy)):
        dst[...] = src[...].astype(BF16)


def _first_inproj(x, ctx, mod, norm_g, w, dense_f32):
    width = w.shape[-1]
    n_steps = NTOK // ROW_TILE
    views, copy_in, copy_shapes, copy_out = _copy_plumbing(dense_f32, 0, n_steps)
    x_tiles = SEQ // ROW_TILE
    x_tile = lambda i: ((i // TILES_PER_BATCH) * x_tiles + jnp.maximum(i % TILES_PER_BATCH - CTX_TILES, 0), 0)
    ctx_tile = lambda i: ((i // TILES_PER_BATCH) * CTX_TILES + jnp.minimum(i % TILES_PER_BATCH, CTX_TILES - 1), 0)
    ident = lambda i: i
    row_out = lambda wd: pl.BlockSpec((ROW_TILE, wd), lambda i: (i, 0))
    p, xa, *copies = pl.pallas_call(
        _first_inproj_kernel,
        out_shape=(jax.ShapeDtypeStruct((NTOK, width), BF16), jax.ShapeDtypeStruct((NTOK, D_MODEL), F32),
                   *copy_shapes),
        grid=(n_steps,),
        in_specs=[pl.BlockSpec((ROW_TILE, D_MODEL), x_tile),
                  pl.BlockSpec((ROW_TILE, D_MODEL), ctx_tile),
                  _layer_block((1, D_MODEL), 0),
                  _mod_spec(0, ident),
                  _layer_block((D_MODEL, width), 0),
                  *copy_in],
        out_specs=(row_out(width), row_out(D_MODEL), *copy_out),
        compiler_params=_params(("arbitrary",)),
        name="first_inproj",
    )(x.reshape(BATCH * SEQ, D_MODEL), ctx.reshape(BATCH * CTX_LEN, D_MODEL), norm_g, mod, w, *views)
    return p, xa, _as_layer_stacks(copies, dense_f32)


def _tail_kernel(x_ref, ma_ref, mb_ref, mod_ref, g_ref, wmix_ref, win_ref, wout_ref, *rest, has_next):
    mod = mod_ref[0]
    m = jnp.concatenate([ma_ref[...], mb_ref[...]], axis=1)
    x1 = x_ref[...] + mod[2:3] * _dot(m, wmix_ref[0])
    h = _modulated_norm(x1, g_ref[0], mod[3:4], mod[4:5]).astype(BF16)
    gu = _dot(h, win_ref[0])
    gate = gu[:, :FFN_HIDDEN]
    act = (gate * jax.nn.sigmoid(gate) * gu[:, FFN_HIDDEN:]).astype(BF16)
    x2 = x1 + mod[5:6] * _dot(act, wout_ref[0])
    if has_next:
        next_mod_ref, gn_ref, wn_ref, *copy_refs, o_ref, p_ref, mix_copy, in_copy, out_copy = rest
        next_mod = next_mod_ref[0]
        z = _modulated_norm(x2, gn_ref[0], next_mod[0:1], next_mod[1:2])
        p_ref[...] = _dot(z.astype(BF16), wn_ref[0]).astype(p_ref.dtype)
        for src, dst in zip(copy_refs, (mix_copy, in_copy, out_copy)):
            dst[...] = src[...].astype(BF16)
    else:
        (o_ref,) = rest
    o_ref[...] = x2


def _layer_tail(xa, ma, mb, mod, layer, norm_ffn_g, dense, nxt):
    wmix, win, wout = dense
    if nxt is None:
        x_tiles = SEQ // ROW_TILE
        n_steps = BATCH * x_tiles
        tile = lambda i: (i // x_tiles) * TILES_PER_BATCH + CTX_TILES + i % x_tiles
        out_rows = BATCH * SEQ
    else:
        n_steps = NTOK // ROW_TILE
        tile = lambda i: i
        out_rows = NTOK
    row_in = lambda width: pl.BlockSpec((ROW_TILE, width), lambda i: (tile(i), 0))
    row_out = lambda width: pl.BlockSpec((ROW_TILE, width), lambda i: (i, 0))
    in_specs = [row_in(D_MODEL), row_in(ma.shape[1]), row_in(mb.shape[1]),
                _mod_spec(layer, tile),
                _layer_block((1, D_MODEL), layer),
                _layer_block((D_MODEL, D_MODEL), 0),
                _layer_block((D_MODEL, 2 * FFN_HIDDEN), 0),
                _layer_block((FFN_HIDDEN, D_MODEL), 0)]
    args = [xa, ma, mb, mod, norm_ffn_g, wmix, win, wout]
    out_shape = [jax.ShapeDtypeStruct((out_rows, D_MODEL), F32)]
    out_specs = [row_out(D_MODEL)]
    if nxt is not None:
        norm_mix_g, w_next, idx, dense_f32 = nxt
        width = w_next.shape[-1]
        views, copy_in, copy_shapes, copy_out = _copy_plumbing(dense_f32, layer + 1, n_steps)
        in_specs += [_mod_spec(layer + 1, tile),
                     _layer_block((1, D_MODEL), layer + 1), _layer_block((D_MODEL, width), idx)] + copy_in
        args += [mod, norm_mix_g, w_next] + views
        out_shape += [jax.ShapeDtypeStruct((out_rows, width), BF16)] + copy_shapes
        out_specs += [row_out(width)] + copy_out
    res = pl.pallas_call(
        functools.partial(_tail_kernel, has_next=nxt is not None),
        out_shape=tuple(out_shape),
        grid=(n_steps,),
        in_specs=in_specs,
        out_specs=tuple(out_specs),
        compiler_params=_params(("arbitrary",)),
        name="layer_tail",
    )(*args)
    if nxt is None:
        return res[0], None, None
    return res[0], res[1], _as_layer_stacks(res[2:], dense_f32)


GLA_QK_W = GLA_HEADS * GLA_DK
GLA_V_W = GLA_HEADS * GLA_DV
GLA_MAIN_W = 2 * GLA_QK_W + 2 * GLA_V_W
GLA_NCHUNK = TOK // GLA_CHUNK


GLA_BLOCK = 256
GLA_NBLOCK = TOK // GLA_BLOCK
GLA_CTX_BLOCKS = CTX_LEN // GLA_BLOCK
GLA_UNROLL = GLA_NBLOCK


LOG2_E = 1.4426950408889634


def _log2_sigmoid(y, scale):
    return jnp.minimum(y, 0.0) * (LOG2_E * scale) - jnp.log2(1.0 + jnp.exp2(jnp.abs(y) * (-LOG2_E))) * scale


def _gla_kernel(main_ref, gate_ref, waf_ref, baf_ref, wab_ref, bab_ref, og_ref, o_ref,
                of_scr, qd_scr, kd_scr, qe_scr, kl_scr, dec_scr, s_scr):
    C, BLK = GLA_CHUNK, GLA_BLOCK
    cpb = BLK // C
    rb = lax.broadcasted_iota(jnp.int32, (BLK, BLK), 0)
    cb = lax.broadcasted_iota(jnp.int32, (BLK, BLK), 1)
    same_chunk = (rb // C) == (cb // C)
    tri_f = jnp.where(same_chunk & (rb >= cb), 1.0, 0.0).astype(BF16)
    tri_b = jnp.where(same_chunk & (rb <= cb), 1.0, 0.0).astype(BF16)
    r = lax.broadcasted_iota(jnp.int32, (C, C), 0)
    cc = lax.broadcasted_iota(jnp.int32, (C, C), 1)
    lower, upper = r >= cc, r <= cc
    sr = lax.broadcasted_iota(jnp.int32, (BLK, cpb * LANE), 0)
    sc = lax.broadcasted_iota(jnp.int32, (BLK, cpb * LANE), 1)
    chunk_sum = jnp.where(sr // C == sc // LANE, 1.0, 0.0).astype(BF16)

    def split(x):
        hi = x.astype(BF16)
        return hi, (x - hi.astype(F32)).astype(BF16)

    def dot_01(m, x):
        hi, lo = split(x)
        return _dot(jnp.concatenate([m, m], axis=1), jnp.concatenate([hi, lo], axis=0))

    def dot_10(x, m):
        hi, lo = split(x)
        return _dot(jnp.concatenate([hi, lo], axis=1), jnp.concatenate([m, m], axis=0))

    def block_rows(blk):
        return pl.ds(pl.multiple_of(blk * BLK, BLK), BLK)

    def chunk_rows(blk, j):
        return pl.ds(pl.multiple_of(blk * BLK + j * C, C), C)

    def dec_rows(blk, j):
        return pl.ds(pl.multiple_of((blk * cpb + j) * GLA_QK_W, GLA_QK_W), GLA_QK_W)

    def decay_terms(a, q, k, w_ref, bias_ref, tri, end_row):
        g = _log2_sigmoid(_dot(a, w_ref[...]) + bias_ref[...], 1.0 / GLA_GATE_NORM)
        b = dot_01(tri, g)
        dec_all = jnp.exp2(dot_10(g.T, chunk_sum))
        terms = []
        for j in range(cpb):
            sl = slice(j * C, (j + 1) * C)
            bc = b[sl]
            b_mid = bc[C // 2:C // 2 + 1]
            b_end = bc[end_row:end_row + 1]
            d = bc - b_mid
            qd = q[sl] * jnp.exp2(d)
            kd = k[sl] * jnp.exp2(-d)
            qe = qd * jnp.exp2(b_mid)
            kl = kd * jnp.exp2(b_end - b_mid)
            dec = dec_all[:, j * LANE:(j + 1) * LANE]
            terms.append((qd.astype(BF16), kd.T.astype(BF16), qe.astype(BF16), kl.T.astype(BF16), dec))
        return terms

    def step(qd, kd_t, qe, kl_t, dec, v, keep):
        outs = []
        for h in range(GLA_HEADS):
            ks = slice(h * GLA_DK, (h + 1) * GLA_DK)
            vh = v[:, h * GLA_DV:(h + 1) * GLA_DV]
            s_h = s_scr[ks, :]
            att = jnp.where(keep, _dot(qd[:, ks], kd_t[ks, :]), 0.0).astype(BF16)
            outs.append(_dot(att, vh) + _dot(qe[:, ks], s_h.astype(BF16)))
            s_scr[ks, :] = s_h * dec[ks, :] + _dot(kl_t[ks, :], vh)
        return outs

    s_scr[...] = jnp.zeros(s_scr.shape, F32)

    def loop1(blk, carry):
        rows = block_rows(blk)
        a = gate_ref[rows, :]
        q = main_ref[rows, 0:GLA_QK_W].astype(F32) * (GLA_DK ** -0.5)
        k = main_ref[rows, GLA_QK_W:2 * GLA_QK_W].astype(F32)
        v = main_ref[rows, 2 * GLA_QK_W:2 * GLA_QK_W + GLA_V_W]
        fwd = decay_terms(a, q, k, waf_ref, baf_ref, tri_f, C - 1)
        bwd = decay_terms(a, q, k, wab_ref, bab_ref, tri_b, 0)
        for j in range(cpb):
            cr = chunk_rows(blk, j)
            qd, kd, qe, kl, dec = bwd[j]
            qd_scr[cr, :] = qd
            kd_scr[dec_rows(blk, j), :] = kd
            qe_scr[cr, :] = qe
            kl_scr[dec_rows(blk, j), :] = kl
            dec_scr[dec_rows(blk, j), :] = dec
            qd, kd, qe, kl, dec = fwd[j]
            outs = step(qd, kd, qe, kl, dec, v[j * C:(j + 1) * C], lower)
            of_scr[cr, :] = jnp.concatenate(outs, axis=1)
        return carry

    lax.fori_loop(0, GLA_NBLOCK, loop1, 0, unroll=GLA_UNROLL)

    s_scr[...] = jnp.zeros(s_scr.shape, F32)

    def loop2(i, carry):
        blk = jnp.where(i < GLA_CTX_BLOCKS, GLA_CTX_BLOCKS - 1 - i, GLA_NBLOCK - 1 - (i - GLA_CTX_BLOCKS))
        for j in reversed(range(cpb)):
            cr = chunk_rows(blk, j)
            v = main_ref[cr, 2 * GLA_QK_W:2 * GLA_QK_W + GLA_V_W]
            kr = dec_rows(blk, j)
            outs = step(qd_scr[cr, :], kd_scr[kr, :], qe_scr[cr, :], kl_scr[kr, :], dec_scr[kr, :], v, upper)
            gout = main_ref[cr, 2 * GLA_QK_W + GLA_V_W:GLA_MAIN_W].astype(F32)
            res = []
            for h in range(GLA_HEADS):
                vs = slice(h * GLA_DV, (h + 1) * GLA_DV)
                o = outs[h] + of_scr[cr, vs]
                gh = gout[:, vs]
                res.append(_rms(o, og_ref[...]) * (gh * jax.nn.sigmoid(gh)))
            o_ref[cr, :] = jnp.concatenate(res, axis=1).astype(o_ref.dtype)
        return carry

    lax.fori_loop(0, GLA_NBLOCK, loop2, 0, unroll=GLA_UNROLL)


def _gla(p, waf, baf, wab, bab, onorm_g):
    gate_col = (EV_W - LANE) // LANE
    qk_scr = pltpu.VMEM((TOK, GLA_QK_W), BF16)
    qk_t_scr = pltpu.VMEM((GLA_NCHUNK * GLA_QK_W, GLA_CHUNK), BF16)
    return pl.pallas_call(
        _gla_kernel,
        out_shape=jax.ShapeDtypeStruct((NTOK, GLA_V_W), BF16),
        grid=(BATCH,),
        in_specs=[pl.BlockSpec((TOK, GLA_MAIN_W), lambda b: (b, 0)),
                  pl.BlockSpec((TOK, LANE), lambda b: (b, gate_col)),
                  _resident((LANE, GLA_QK_W)), _resident((1, GLA_QK_W)),
                  _resident((LANE, GLA_QK_W)), _resident((1, GLA_QK_W)),
                  _resident((1, GLA_DV))],
        out_specs=pl.BlockSpec((TOK, GLA_V_W), lambda b: (b, 0)),
        scratch_shapes=[pltpu.VMEM((TOK, GLA_V_W), F32), qk_scr, qk_t_scr, qk_scr, qk_t_scr,
                        pltpu.VMEM((GLA_NCHUNK * GLA_QK_W, LANE), F32),
                        pltpu.VMEM((GLA_QK_W, GLA_DV), F32)],
        compiler_params=_params(("parallel",)),
        name="gla",
    )(p, p, waf, baf, wab, bab, onorm_g.reshape(1, GLA_DV))


def _gelu(x):
    return 0.5 * x * (1.0 + lax.erf(x * (2.0 ** -0.5)))


SG_TILE = 1152


def _sg_kernel(u_ref, v_ref, vg_ref, ws_ref, bs_ref, o_ref):
    for c in range(SG_TILE // SG_CHUNK):
        rows = slice(c * SG_CHUNK, (c + 1) * SG_CHUNK)
        for g in range(SG_GROUPS):
            cols = slice(g * SG_DIM, (g + 1) * SG_DIM)
            u = _gelu(u_ref[rows, cols].astype(F32))
            vn = _lane_rms(_gelu(v_ref[rows, cols].astype(F32)), vg_ref[g:g + 1, :])
            mixed = _dot(ws_ref[g], vn.astype(BF16)) + bs_ref[g]
            o_ref[rows, cols] = (u * mixed).astype(o_ref.dtype)


def _spatial_gating(p, vnorm_g, ws, bs):
    width = SG_GROUPS * SG_DIM
    u_col = GLA_MAIN_W // width
    bias = jnp.broadcast_to(bs[:, :, None], (SG_GROUPS, SG_CHUNK, SG_DIM))
    return pl.pallas_call(
        _sg_kernel,
        out_shape=jax.ShapeDtypeStruct((NTOK, width), BF16),
        grid=(NTOK // SG_TILE,),
        in_specs=[pl.BlockSpec((SG_TILE, width), lambda i: (i, u_col)),
                  pl.BlockSpec((SG_TILE, width), lambda i: (i, u_col + 1)),
                  _resident((SG_GROUPS, SG_DIM)),
                  _resident((SG_GROUPS, SG_CHUNK, SG_CHUNK)),
                  _resident((SG_GROUPS, SG_CHUNK, SG_DIM))],
        out_specs=pl.BlockSpec((SG_TILE, width), lambda i: (i, 0)),
        compiler_params=_params(("parallel",)),
        name="spatial_gating",
    )(p, p, vnorm_g, ws, bias)


FT_W = FT_GROUPS * FT_DIM


def _dft_cos_sin(n):
    jk = np.outer(np.arange(n), np.arange(n)) % n
    ang = 2.0 * np.pi * jk.astype(np.float64) / n
    return np.cos(ang), np.sin(ang)


def _parity_dft(n):
    p = np.arange(n // 2)[:, None]
    mats = []
    for first in (0, 1):
        t = 2 * np.arange(n // 2)[None, :] + first
        ang = 2.0 * np.pi * ((p * t) % n).astype(np.float64) / n
        mats.append(np.concatenate([np.cos(ang), -np.sin(ang)], axis=1))
    return jnp.asarray(np.stack(mats), F32).astype(BF16)


def _fourier_kernel(h_ref, cs_ref, dft_x_ref, dft_c_ref, o_ref, ab_scr, *, need_ctx):
    ab = _dot(h_ref[...], cs_ref[...])
    n_slab = 2 * FT_W // LANE
    for c in range(n_slab):
        ab_scr[c] = ab[:, c * LANE:(c + 1) * LANE]

    def position_dft(first_row, length, dft_ref):
        half = length // 2
        parts = []
        for parity in (0, 1):
            slabs = [ab_scr[c, pl.ds(first_row + parity, half, stride=2), :].astype(BF16) for c in range(n_slab)]
            a = jnp.concatenate(slabs[:n_slab // 2], axis=1)
            b = jnp.concatenate(slabs[n_slab // 2:], axis=1)
            parts.append(_dot(dft_ref[parity], jnp.concatenate([a, b], axis=0)))
        scale = (length * FT_DIM) ** -0.5
        o_ref[first_row:first_row + half, :] = ((parts[0] + parts[1]) * scale).astype(o_ref.dtype)
        o_ref[first_row + half:first_row + length, :] = ((parts[0] - parts[1]) * scale).astype(o_ref.dtype)

    position_dft(CTX_LEN, SEQ, dft_x_ref)
    if need_ctx:
        position_dft(0, CTX_LEN, dft_c_ref)
    else:
        o_ref[:CTX_LEN, :] = jnp.zeros((CTX_LEN, FT_W), o_ref.dtype)


def _fourier(p, need_ctx):
    cc, sc = _dft_cos_sin(FT_DIM)
    eye = np.eye(FT_GROUPS)
    cs = jnp.asarray(np.concatenate([np.kron(eye, cc), np.kron(eye, sc)], axis=1), F32).astype(BF16)
    return pl.pallas_call(
        functools.partial(_fourier_kernel, need_ctx=need_ctx),
        out_shape=jax.ShapeDtypeStruct((NTOK, FT_W), BF16),
        grid=(BATCH,),
        in_specs=[pl.BlockSpec((TOK, FT_W), lambda b: (b, 0)),
                  _resident((FT_W, 2 * FT_W)),
                  _resident((2, SEQ // 2, SEQ)), _resident((2, CTX_LEN // 2, CTX_LEN))],
        out_specs=pl.BlockSpec((TOK, FT_W), lambda b: (b, 0)),
        scratch_shapes=[pltpu.VMEM((2 * FT_W // LANE, TOK, LANE), F32)],
        compiler_params=_params(("parallel",)),
        name="fourier",
    )(p, cs, _parity_dft(SEQ), _parity_dft(CTX_LEN))


MLA_NOPE_W = MLA_HEADS * MLA_NOPE
MLA_ROPE_W = MLA_HEADS * MLA_ROPE
OD_QA0 = FT_W
OD_KVA0 = OD_QA0 + MLA_Q_RANK
OD_KPE0 = OD_KVA0 + MLA_KV_RANK
OD_KPE_ROT0 = OD_KPE0 + MLA_ROPE


def _rot_cols(w):
    q = MLA_ROPE // 4
    return jnp.concatenate([-w[..., q:2 * q], w[..., 0:q], -w[..., 3 * q:4 * q], w[..., 2 * q:3 * q]], axis=-1)


def _rot_gain(g):
    q = MLA_ROPE // 4
    return jnp.concatenate([g[..., q:2 * q], g[..., 0:q], g[..., 3 * q:4 * q], g[..., 2 * q:3 * q]], axis=-1)


def _rope_tables():
    rows = SEQ // GRID_W
    row_id = np.repeat(np.arange(rows, dtype=np.float32), GRID_W)
    col_id = np.tile(np.arange(GRID_W, dtype=np.float32), rows)
    axis_dim = MLA_ROPE // 2
    inv_freq = (np.float32(ROPE_BASE) ** (-np.arange(0, axis_dim, 2, dtype=np.float32) / np.float32(axis_dim))).astype(np.float32)
    ang_r = row_id[:, None] * inv_freq
    ang_c = col_id[:, None] * inv_freq
    ang = np.concatenate([ang_r, ang_r, ang_c, ang_c], axis=-1).astype(np.float32)
    cos = np.concatenate([np.ones((CTX_LEN, MLA_ROPE), np.float32), np.cos(ang)], axis=0)
    sin = np.concatenate([np.zeros((CTX_LEN, MLA_ROPE), np.float32), np.sin(ang)], axis=0)
    return tuple(jnp.asarray(t, F32) for t in (cos, sin, cos.T, sin.T))


MLA_VT_ROWS = MLA_V + 16


def _mla_heads(p, cos, sin, cos_t, sin_t, qag, wuq_t, kvag, wukv_k, wukv_vt, head_sum, kg,
               qg_nope, qg_rope, qg_rot, qt_ref, k_ref, vt_ref):
    qa = _lane_rms(p[:, OD_QA0:OD_KVA0].astype(F32), qag)
    q_t = _dot(wuq_t, qa.T.astype(BF16))
    for h in range(MLA_HEADS):
        qn = q_t[h * MLA_NOPE:(h + 1) * MLA_NOPE]
        qr = q_t[MLA_NOPE_W + h * MLA_ROPE:MLA_NOPE_W + (h + 1) * MLA_ROPE]
        qrr = q_t[MLA_NOPE_W + MLA_ROPE_W + h * MLA_ROPE:MLA_NOPE_W + MLA_ROPE_W + (h + 1) * MLA_ROPE]
        ss = jnp.sum(qn * qn, axis=0, keepdims=True) + jnp.sum(qr * qr, axis=0, keepdims=True)
        inv = lax.rsqrt(ss * (1.0 / MLA_QK) + RMS_EPS) * (MLA_QK ** -0.5 * LOG2_E)
        q_rope = qr * qg_rope * cos_t + qrr * qg_rot * sin_t
        qt_ref[h] = jnp.concatenate([qn * qg_nope * inv, q_rope * inv], axis=0).astype(qt_ref.dtype)

    kva = _lane_rms(p[:, OD_KVA0:OD_KPE0].astype(F32), kvag)
    v_t = _dot(wukv_vt, kva.T.astype(BF16))
    ones = jnp.ones((MLA_VT_ROWS - MLA_V, v_t.shape[1]), F32)
    for h in range(MLA_HEADS):
        vt_ref[h] = jnp.concatenate([v_t[h * MLA_V:(h + 1) * MLA_V], ones], axis=0).astype(vt_ref.dtype)
    kn = _dot(kva.astype(BF16), wukv_k)
    kpe = p[:, OD_KPE0:OD_KPE_ROT0].astype(F32)
    kpe_rot = p[:, OD_KPE_ROT0:OD_W].astype(F32)
    sq = jnp.concatenate([kn * kn, kpe * kpe], axis=1).astype(BF16)
    ss = _dot(sq, head_sum)
    inv = lax.rsqrt(ss * (1.0 / MLA_QK) + RMS_EPS)
    k_rope = kpe * kg[:, MLA_NOPE:MLA_QK] * cos + kpe_rot * kg[:, MLA_QK:] * sin
    for h in range(MLA_HEADS):
        k_h = jnp.concatenate([kn[:, h * MLA_NOPE:(h + 1) * MLA_NOPE] * kg[:, :MLA_NOPE], k_rope], axis=1)
        k_ref[h] = (k_h * inv[:, h:h + 1]).astype(k_ref.dtype)


def _key_head_sum_matrix():
    head = np.arange(LANE)[None, :]
    nope = np.repeat(np.arange(MLA_HEADS), MLA_NOPE)[:, None] == head
    rope = np.broadcast_to(head < MLA_HEADS, (MLA_ROPE, LANE))
    return jnp.asarray(np.concatenate([nope, rope], axis=0), BF16)


MLA_TILE = 768


def _mla_prep_kernel(p_ref, *refs):
    *operand_refs, qt_ref, k_ref, vt_ref = refs
    _mla_heads(p_ref[...], *(r[...] for r in operand_refs), qt_ref, k_ref, vt_ref)


def _mla_prep(p, rope_tabs, qa_g, wuq, kva_g, wukv, qn_g, kn_g):
    wukv = _wukv_cols(wukv)
    kgain = jnp.concatenate([kn_g, _rot_gain(kn_g[MLA_NOPE:])]).reshape(1, -1)
    col = lambda g: jnp.broadcast_to(g[:, None], (g.shape[0], MLA_TILE))
    operands = [*rope_tabs, qa_g.reshape(1, -1), _wuq_cols(wuq).T, kva_g.reshape(1, -1),
                wukv[:, :MLA_NOPE_W], wukv[:, MLA_NOPE_W:].T, _key_head_sum_matrix(), kgain,
                col(qn_g[:MLA_NOPE]), col(qn_g[MLA_NOPE:]), col(_rot_gain(qn_g[MLA_NOPE:]))]
    tiles_per_batch = TOK // MLA_TILE
    rope = pl.BlockSpec((MLA_TILE, MLA_ROPE), lambda i: (i % tiles_per_batch, 0))
    rope_t = pl.BlockSpec((MLA_ROPE, MLA_TILE), lambda i: (0, i % tiles_per_batch))
    rows = lambda d: pl.BlockSpec((MLA_HEADS, MLA_TILE, d), lambda i: (0, i, 0))
    cols = lambda d: pl.BlockSpec((MLA_HEADS, d, MLA_TILE), lambda i: (0, 0, i))
    return pl.pallas_call(
        _mla_prep_kernel,
        out_shape=(jax.ShapeDtypeStruct((MLA_HEADS, MLA_QK, NTOK), BF16),
                   jax.ShapeDtypeStruct((MLA_HEADS, NTOK, MLA_QK), BF16),
                   jax.ShapeDtypeStruct((MLA_HEADS, MLA_VT_ROWS, NTOK), BF16)),
        grid=(NTOK // MLA_TILE,),
        in_specs=[pl.BlockSpec((MLA_TILE, OD_W), lambda i: (i, 0)), rope, rope, rope_t, rope_t]
                 + [_resident(a.shape) for a in operands[4:]],
        out_specs=(cols(MLA_QK), rows(MLA_QK), cols(MLA_VT_ROWS)),
        compiler_params=_params(("parallel",)),
        name="mla_prep",
    )(p, *operands)


ATT_Q_TILE = 512


ATT_KEY_CHUNK = TOK


def _weighted_values(acc):
    return (acc[:MLA_V] * (1.0 / acc[MLA_V:MLA_V + 1])).T


def _attn_kernel(qt_ref, k_ref, vt_ref, o_ref, s_scr, *, need_ctx):
    n_q = SEQ // ATT_Q_TILE
    chunks = [slice(c * ATT_KEY_CHUNK, (c + 1) * ATT_KEY_CHUNK) for c in range(TOK // ATT_KEY_CHUNK)]
    q_cols = lambda j: slice(CTX_LEN + j * ATT_Q_TILE, CTX_LEN + (j + 1) * ATT_Q_TILE)

    def scores(j, rows, col_max):
        s = _dot(k_ref[0, rows, :], qt_ref[0, :, q_cols(j)])
        s_scr[j % 2, rows, :] = s
        m = jnp.max(s, axis=0, keepdims=True)
        return m if col_max is None else jnp.maximum(col_max, m)

    col_max = None
    for rows in chunks:
        col_max = scores(0, rows, col_max)
    for j in range(n_q):
        acc, next_max = None, None
        for rows in chunks:
            if j + 1 < n_q:
                next_max = scores(j + 1, rows, next_max)
            p = jnp.exp2(s_scr[j % 2, rows, :] - col_max).astype(BF16)
            part = _dot(vt_ref[0, :, rows], p)
            acc = part if acc is None else acc + part
        o_ref[q_cols(j), :] = _weighted_values(acc).astype(o_ref.dtype)
        col_max = next_max

    if need_ctx:
        s = _dot(k_ref[0, :CTX_LEN, :], qt_ref[0, :, :CTX_LEN])
        p = jnp.exp2(s - jnp.max(s, axis=0, keepdims=True)).astype(BF16)
        o_ref[:CTX_LEN, :] = _weighted_values(_dot(vt_ref[0, :, :CTX_LEN], p)).astype(o_ref.dtype)
    else:
        o_ref[:CTX_LEN, :] = jnp.zeros((CTX_LEN, MLA_V), o_ref.dtype)


def _attention(q_t, k, v_t, need_ctx):
    transposed = lambda d: pl.BlockSpec((1, d, TOK), lambda b, h: (h, 0, b))
    return pl.pallas_call(
        functools.partial(_attn_kernel, need_ctx=need_ctx),
        out_shape=jax.ShapeDtypeStruct((NTOK, MLA_HEADS * MLA_V), BF16),
        grid=(BATCH, MLA_HEADS),
        in_specs=[transposed(MLA_QK),
                  pl.BlockSpec((1, TOK, MLA_QK), lambda b, h: (h, b, 0)),
                  transposed(MLA_VT_ROWS)],
        out_specs=pl.BlockSpec((TOK, MLA_V), lambda b, h: (b, h)),
        scratch_shapes=[pltpu.VMEM((2, TOK, ATT_Q_TILE), F32)],
        compiler_params=_params(("parallel", "parallel")),
        name="attention",
    )(q_t, k, v_t)


def _even_w_in(w):
    gates0 = GLA_MAIN_W
    gates1 = gates0 + 2 * GLA_GATE_RANK
    w = w.astype(BF16)
    pad = jnp.zeros(w.shape[:-1] + (EV_W - w.shape[-1],), w.dtype)
    return jnp.concatenate([w[..., :gates0], w[..., gates1:], w[..., gates0:gates1], pad], axis=-1)


def _gate_w(wa, first_row):
    out = jnp.zeros((LANE, GLA_QK_W), wa.dtype)
    return lax.dynamic_update_slice(out, wa, (first_row, 0)).astype(BF16)


def _odd_w_in(w):
    w = w.astype(BF16)
    return jnp.concatenate([w, _rot_cols(w[..., OD_KPE0:OD_KPE_ROT0])], axis=-1)


def _wuq_cols(w):
    w = w.reshape(MLA_Q_RANK, MLA_HEADS, MLA_QK)
    nope = w[:, :, :MLA_NOPE].reshape(MLA_Q_RANK, MLA_NOPE_W)
    rope = w[:, :, MLA_NOPE:]
    return jnp.concatenate([nope, rope.reshape(MLA_Q_RANK, MLA_ROPE_W),
                            _rot_cols(rope).reshape(MLA_Q_RANK, MLA_ROPE_W)], axis=1).astype(BF16)


def _wukv_cols(w):
    w = w.reshape(MLA_KV_RANK, MLA_HEADS, MLA_NOPE + MLA_V)
    return jnp.concatenate([w[:, :, :MLA_NOPE].reshape(MLA_KV_RANK, MLA_NOPE_W),
                            w[:, :, MLA_NOPE:].reshape(MLA_KV_RANK, MLA_HEADS * MLA_V)], axis=1).astype(BF16)


def kernel(x, c, ctx, c_ctx, ada_w, ada_b, norm_mix_g, norm_ffn_g, w_mix_out, ffn_w_in, ffn_w_out, ev_w_in, gla_wa_f, gla_ba_f, gla_wa_b, gla_ba_b, gla_onorm_g, sg_vnorm_g, sg_ws, sg_bs, od_w_in, mla_qa_g, mla_wuq, mla_kva_g, mla_wukv, mla_qn_g, mla_kn_g):
    cvec = jnp.zeros((MOD_ROWS, D_MODEL), F32).at[:BATCH].set(c).at[CTX_MOD_ROW].set(c_ctx)
    mod = _ada_table(cvec, ada_w, ada_b)
    rope_tabs = _rope_tables()
    norm_mix_g = norm_mix_g.reshape(DEPTH, 1, D_MODEL)
    norm_ffn_g = norm_ffn_g.reshape(DEPTH, 1, D_MODEL)
    w_in_stacks = (_even_w_in(ev_w_in), _odd_w_in(od_w_in))
    dense_f32 = (w_mix_out, ffn_w_in, ffn_w_out)

    p, xa, dense = _first_inproj(x, ctx, mod, norm_mix_g, w_in_stacks[0], dense_f32)
    for l in range(DEPTH):
        last = l == DEPTH - 1
        i = l // 2
        if l % 2 == 0:
            ma = _gla(p, _gate_w(gla_wa_f[i], 0), gla_ba_f[i].reshape(1, GLA_QK_W),
                      _gate_w(gla_wa_b[i], GLA_GATE_RANK), gla_ba_b[i].reshape(1, GLA_QK_W), gla_onorm_g[i])
            mb = _spatial_gating(p, sg_vnorm_g[i], sg_ws[i].astype(BF16), sg_bs[i])
        else:
            q_t, k, v_t = _mla_prep(p, rope_tabs, mla_qa_g[i], mla_wuq[i], mla_kva_g[i], mla_wukv[i],
                                    mla_qn_g[i], mla_kn_g[i])
            ma = _fourier(p, not last)
            mb = _attention(q_t, k, v_t, not last)
        nxt = None if last else (norm_mix_g, w_in_stacks[(l + 1) % 2], (l + 1) // 2, dense_f32)
        xa, p, dense = _layer_tail(xa, ma, mb, mod, l, norm_ffn_g, dense, nxt)
    return xa.reshape(BATCH, SEQ, D_MODEL)
```

```python
import functools

import numpy as np
import jax
import jax.numpy as jnp
from jax import lax
from jax.experimental import pallas as pl
from jax.experimental.pallas import tpu as pltpu

D_MODEL = 1024
BATCH = 8
SEQ = 2048
DEPTH = 4
CTX_LEN = 256
GRID_W = 64
RMS_EPS = 1e-6
GLA_HEADS = 4
GLA_DK = 64
GLA_DV = 128
GLA_GATE_RANK = 16
GLA_GATE_NORM = 16.0
GLA_CHUNK = 64
SG_GROUPS = 4
SG_DIM = 128
SG_CHUNK = 128
FT_GROUPS = 4
FT_DIM = 64
MLA_HEADS = 6
MLA_NOPE = 128
MLA_ROPE = 64
MLA_V = 128
MLA_Q_RANK = 384
MLA_KV_RANK = 256
ROPE_BASE = 10000.0
FFN_HIDDEN = 2816

TOK = CTX_LEN + SEQ
NTOK = BATCH * TOK
ROW_TILE = 256
TILES_PER_BATCH = TOK // ROW_TILE
CTX_TILES = CTX_LEN // ROW_TILE
MOD_ROWS = 16
CTX_MOD_ROW = BATCH
LANE = 128
EV_W = 2688
OD_W = 1024
MLA_QK = MLA_NOPE + MLA_ROPE
VMEM_LIMIT = 56 * 1024 * 1024

F32 = jnp.float32
BF16 = jnp.bfloat16


def _dot(a, b):
    return jnp.dot(a, b, preferred_element_type=F32)


def _rms(x, g):
    return x * lax.rsqrt(jnp.mean(x * x, axis=-1, keepdims=True) + RMS_EPS) * g


def _lane_rms(x, g):
    width = x.shape[1]
    ss = _dot((x * x).astype(BF16), jnp.ones((width, LANE), BF16))
    inv = lax.rsqrt(ss * (1.0 / width) + RMS_EPS)
    return x * jnp.concatenate([inv] * (width // LANE), axis=1) * g


def _params(sem):
    return pltpu.CompilerParams(dimension_semantics=sem, vmem_limit_bytes=VMEM_LIMIT)


def _resident(shape):
    zeros = (0,) * len(shape)
    return pl.BlockSpec(shape, lambda *_: zeros, pipeline_mode=pl.Buffered(1))


def _mod_row(i):
    return jnp.where(i % TILES_PER_BATCH < CTX_TILES, CTX_MOD_ROW, i // TILES_PER_BATCH)


MOD_PIECES = 6


def _mod_spec(layer, tile_of_step):
    def index_map(i):
        return (layer * MOD_ROWS + _mod_row(tile_of_step(i)), 0, 0)
    return pl.BlockSpec((1, MOD_PIECES, D_MODEL), index_map)


def _ada_kernel(c_ref, w_ref, b_ref, o_ref):
    c = c_ref[...]
    s = c * jax.nn.sigmoid(c)
    w = w_ref[0]
    w_hi = w.astype(BF16)
    w_lo = (w - w_hi.astype(F32)).astype(BF16)
    s_hi = s.astype(BF16)
    s_lo = (s - s_hi.astype(F32)).astype(BF16)
    o_ref[0] = _dot(s_hi, w_hi) + (_dot(s_hi, w_lo) + _dot(s_lo, w_hi)) + b_ref[0]


def _ada_table(cvec, ada_w, ada_b):
    tn = 3072
    n = MOD_PIECES * D_MODEL
    out = pl.pallas_call(
        _ada_kernel,
        out_shape=jax.ShapeDtypeStruct((DEPTH, MOD_ROWS, n), F32),
        grid=(DEPTH, n // tn),
        in_specs=[pl.BlockSpec((MOD_ROWS, D_MODEL), lambda l, j: (0, 0)),
                  pl.BlockSpec((1, D_MODEL, tn), lambda l, j: (l, 0, j)),
                  pl.BlockSpec((1, 1, tn), lambda l, j: (l, 0, j))],
        out_specs=pl.BlockSpec((1, MOD_ROWS, tn), lambda l, j: (l, 0, j)),
        compiler_params=_params(("arbitrary", "arbitrary")),
        name="ada_table",
    )(cvec, ada_w, ada_b.reshape(DEPTH, 1, n))
    return out.reshape(DEPTH * MOD_ROWS, MOD_PIECES, D_MODEL)


def _layer_block(shape, layer):
    zeros = (0,) * len(shape)
    return pl.BlockSpec((1,) + tuple(shape), lambda *_: (layer,) + zeros, pipeline_mode=pl.Buffered(1))


BF16_SUBLANES = 16


def _copy_plumbing(stacks, layer, n_steps):
    views, in_specs, shapes, out_specs = [], [], [], []
    for w in stacks:
        n_layers, n_rows, cols = w.shape
        rows = next(r for r in range(BF16_SUBLANES, n_rows + 1, BF16_SUBLANES)
                    if n_rows % r == 0 and n_rows // r <= n_steps)
        n_slices = n_rows // rows
        views.append(w.reshape(n_layers * n_rows, cols))
        in_specs.append(pl.BlockSpec(
            (rows, cols), lambda i, n=n_slices: (layer * n + jnp.minimum(i, n - 1), 0)))
        shapes.append(jax.ShapeDtypeStruct((n_rows, cols), BF16))
        out_specs.append(pl.BlockSpec((rows, cols), lambda i, n=n_slices: (jnp.minimum(i, n - 1), 0)))
    return views, in_specs, shapes, out_specs


def _as_layer_stacks(copies, stacks):
    return [c.reshape(1, w.shape[1], w.shape[2]) for c, w in zip(copies, stacks)]


def _modulated_norm(x, g, shift, scale):
    return _rms(x, g) * (1.0 + scale) + shift


def _first_inproj_kernel(x_ref, ctx_ref, g_ref, mod_ref, w_ref, *rest):
    *copy_refs, p_ref, xa_ref, mix_copy, in_copy, out_copy = rest
    is_ctx = pl.program_id(0) % TILES_PER_BATCH < CTX_TILES
    x = jnp.where(is_ctx, ctx_ref[...], x_ref[...])
    xa_ref[...] = x
    mod = mod_ref[0]
    z = _modulated_norm(x, g_ref[0], mod[0:1], mod[1:2])
    p_ref[...] = _dot(z.astype(BF16), w_ref[0]).astype(p_ref.dtype)
    for src, dst in zip(copy_refs, (mix_copy, in_copy, out_copy)):
        dst[...] = src[...].astype(BF16)


def _first_inproj(x, ctx, mod, norm_g, w, dense_f32):
    width = w.shape[-1]
    n_steps = NTOK // ROW_TILE
    views, copy_in, copy_shapes, copy_out = _copy_plumbing(dense_f32, 0, n_steps)
    x_tiles = SEQ // ROW_TILE
    x_tile = lambda i: ((i // TILES_PER_BATCH) * x_tiles + jnp.maximum(i % TILES_PER_BATCH - CTX_TILES, 0), 0)
    ctx_tile = lambda i: ((i // TILES_PER_BATCH) * CTX_TILES + jnp.minimum(i % TILES_PER_BATCH, CTX_TILES - 1), 0)
    ident = lambda i: i
    row_out = lambda wd: pl.BlockSpec((ROW_TILE, wd), lambda i: (i, 0))
    p, xa, *copies = pl.pallas_call(
        _first_inproj_kernel,
        out_shape=(jax.ShapeDtypeStruct((NTOK, width), BF16), jax.ShapeDtypeStruct((NTOK, D_MODEL), F32),
                   *copy_shapes),
        grid=(n_steps,),
        in_specs=[pl.BlockSpec((ROW_TILE, D_MODEL), x_tile),
                  pl.BlockSpec((ROW_TILE, D_MODEL), ctx_tile),
                  _layer_block((1, D_MODEL), 0),
                  _mod_spec(0, ident),
                  _layer_block((D_MODEL, width), 0),
                  *copy_in],
        out_specs=(row_out(width), row_out(D_MODEL), *copy_out),
        compiler_params=_params(("arbitrary",)),
        name="first_inproj",
    )(x.reshape(BATCH * SEQ, D_MODEL), ctx.reshape(BATCH * CTX_LEN, D_MODEL), norm_g, mod, w, *views)
    return p, xa, _as_layer_stacks(copies, dense_f32)


TAIL_TILES = 2


def _tail_kernel(*refs, has_next):
    n = TAIL_TILES
    x_refs, ma_refs, mb_refs, mod_refs = refs[:n], refs[n:2 * n], refs[2 * n:3 * n], refs[3 * n:4 * n]
    g_ref, wmix_ref, win_ref, wout_ref, *rest = refs[4 * n:]
    mods = [r[0] for r in mod_refs]
    stack = lambda tiles: jnp.concatenate(tiles, axis=0)
    unstack = lambda rows: [rows[t * ROW_TILE:(t + 1) * ROW_TILE] for t in range(n)]

    m = stack([jnp.concatenate([a[...], b[...]], axis=1) for a, b in zip(ma_refs, mb_refs)])
    x1 = [x[...] + mod[2:3] * y for x, mod, y in zip(x_refs, mods, unstack(_dot(m, wmix_ref[0])))]
    h = stack([_modulated_norm(x, g_ref[0], mod[3:4], mod[4:5]).astype(BF16) for x, mod in zip(x1, mods)])
    gu = _dot(h, win_ref[0])
    gate = gu[:, :FFN_HIDDEN]
    act = (gate * jax.nn.sigmoid(gate) * gu[:, FFN_HIDDEN:]).astype(BF16)
    x2 = [x + mod[5:6] * d for x, mod, d in zip(x1, mods, unstack(_dot(act, wout_ref[0])))]
    if has_next:
        next_mods = [r[0] for r in rest[:n]]
        gn_ref, wn_ref, *copy_refs, o_ref, p_ref, mix_copy, in_copy, out_copy = rest[n:]
        z = stack([_modulated_norm(x, gn_ref[0], mod[0:1], mod[1:2]).astype(BF16) for x, mod in zip(x2, next_mods)])
        p_ref[...] = _dot(z, wn_ref[0]).astype(p_ref.dtype)
        for src, dst in zip(copy_refs, (mix_copy, in_copy, out_copy)):
            dst[...] = src[...].astype(BF16)
    else:
        (o_ref,) = rest
    o_ref[...] = stack(x2)


def _layer_tail(xa, ma, mb, mod, layer, norm_ffn_g, dense, nxt):
    wmix, win, wout = dense
    if nxt is None:
        x_tiles = SEQ // ROW_TILE
        n_tiles = BATCH * x_tiles
        tile = lambda i: (i // x_tiles) * TILES_PER_BATCH + CTX_TILES + i % x_tiles
        out_rows = BATCH * SEQ
    else:
        n_tiles = NTOK // ROW_TILE
        tile = lambda i: i
        out_rows = NTOK
    n_steps = n_tiles // TAIL_TILES
    tiles = [lambda i, t=t: tile(TAIL_TILES * i + t) for t in range(TAIL_TILES)]
    row_in = lambda width: [pl.BlockSpec((ROW_TILE, width), lambda i, f=f: (f(i), 0)) for f in tiles]
    row_out = lambda width: pl.BlockSpec((TAIL_TILES * ROW_TILE, width), lambda i: (i, 0))
    in_specs = [*row_in(D_MODEL), *row_in(ma.shape[1]), *row_in(mb.shape[1]),
                *[_mod_spec(layer, f) for f in tiles],
                _layer_block((1, D_MODEL), layer),
                _layer_block((D_MODEL, D_MODEL), 0),
                _layer_block((D_MODEL, 2 * FFN_HIDDEN), 0),
                _layer_block((FFN_HIDDEN, D_MODEL), 0)]
    per_tile = lambda a: [a] * TAIL_TILES
    args = [*per_tile(xa), *per_tile(ma), *per_tile(mb), *per_tile(mod), norm_ffn_g, wmix, win, wout]
    out_shape = [jax.ShapeDtypeStruct((out_rows, D_MODEL), F32)]
    out_specs = [row_out(D_MODEL)]
    if nxt is not None:
        norm_mix_g, w_next, idx, dense_f32 = nxt
        width = w_next.shape[-1]
        views, copy_in, copy_shapes, copy_out = _copy_plumbing(dense_f32, layer + 1, n_steps)
        in_specs += [*[_mod_spec(layer + 1, f) for f in tiles],
                     _layer_block((1, D_MODEL), layer + 1), _layer_block((D_MODEL, width), idx)] + copy_in
        args += [*per_tile(mod), norm_mix_g, w_next] + views
        out_shape += [jax.ShapeDtypeStruct((out_rows, width), BF16)] + copy_shapes
        out_specs += [row_out(width)] + copy_out
    res = pl.pallas_call(
        functools.partial(_tail_kernel, has_next=nxt is not None),
        out_shape=tuple(out_shape),
        grid=(n_steps,),
        in_specs=in_specs,
        out_specs=tuple(out_specs),
        compiler_params=_params(("arbitrary",)),
        name="layer_tail",
    )(*args)
    if nxt is None:
        return res[0], None, None
    return res[0], res[1], _as_layer_stacks(res[2:], dense_f32)


GLA_QK_W = GLA_HEADS * GLA_DK
GLA_V_W = GLA_HEADS * GLA_DV
GLA_MAIN_W = 2 * GLA_QK_W + 2 * GLA_V_W
GLA_NCHUNK = TOK // GLA_CHUNK


GLA_BLOCK = 256
GLA_NBLOCK = TOK // GLA_BLOCK
GLA_CTX_BLOCKS = CTX_LEN // GLA_BLOCK
GLA_UNROLL = GLA_NBLOCK


LOG2_E = 1.4426950408889634


def _log2_sigmoid(y, scale):
    return jnp.minimum(y, 0.0) * (LOG2_E * scale) - jnp.log2(1.0 + jnp.exp2(jnp.abs(y) * (-LOG2_E))) * scale


def _gla_kernel(main_ref, gate_ref, waf_ref, baf_ref, wab_ref, bab_ref, og_ref, o_ref,
                of_scr, qd_scr, kd_scr, qe_scr, kl_scr, dec_scr, s_scr):
    C, BLK = GLA_CHUNK, GLA_BLOCK
    cpb = BLK // C
    rb = lax.broadcasted_iota(jnp.int32, (BLK, BLK), 0)
    cb = lax.broadcasted_iota(jnp.int32, (BLK, BLK), 1)
    same_chunk = (rb // C) == (cb // C)
    tri_f = jnp.where(same_chunk & (rb >= cb), 1.0, 0.0).astype(BF16)
    tri_b = jnp.where(same_chunk & (rb <= cb), 1.0, 0.0).astype(BF16)
    r = lax.broadcasted_iota(jnp.int32, (C, C), 0)
    cc = lax.broadcasted_iota(jnp.int32, (C, C), 1)
    lower, upper = r >= cc, r <= cc
    sr = lax.broadcasted_iota(jnp.int32, (BLK, cpb * LANE), 0)
    sc = lax.broadcasted_iota(jnp.int32, (BLK, cpb * LANE), 1)
    chunk_sum = jnp.where(sr // C == sc // LANE, 1.0, 0.0).astype(BF16)

    def split(x):
        hi = x.astype(BF16)
        return hi, (x - hi.astype(F32)).astype(BF16)

    def dot_01(m, x):
        hi, lo = split(x)
        return _dot(jnp.concatenate([m, m], axis=1), jnp.concatenate([hi, lo], axis=0))

    def dot_10(x, m):
        hi, lo = split(x)
        return _dot(jnp.concatenate([hi, lo], axis=1), jnp.concatenate([m, m], axis=0))

    def block_rows(blk):
        return pl.ds(pl.multiple_of(blk * BLK, BLK), BLK)

    def chunk_rows(blk, j):
        return pl.ds(pl.multiple_of(blk * BLK + j * C, C), C)

    def dec_rows(blk, j):
        return pl.ds(pl.multiple_of((blk * cpb + j) * GLA_QK_W, GLA_QK_W), GLA_QK_W)

    def decay_terms(a, q, k, w_ref, bias_ref, tri, end_row):
        g = _log2_sigmoid(_dot(a, w_ref[...]) + bias_ref[...], 1.0 / GLA_GATE_NORM)
        b = dot_01(tri, g)
        dec_all = jnp.exp2(dot_10(g.T, chunk_sum))
        terms = []
        for j in range(cpb):
            sl = slice(j * C, (j + 1) * C)
            bc = b[sl]
            b_mid = bc[C // 2:C // 2 + 1]
            b_end = bc[end_row:end_row + 1]
            d = bc - b_mid
            qd = q[sl] * jnp.exp2(d)
            kd = k[sl] * jnp.exp2(-d)
            qe = qd * jnp.exp2(b_mid)
            kl = kd * jnp.exp2(b_end - b_mid)
            dec = dec_all[:, j * LANE:(j + 1) * LANE]
            terms.append((qd.astype(BF16), kd.T.astype(BF16), qe.astype(BF16), kl.T.astype(BF16), dec))
        return terms

    def step(qd, kd_t, qe, kl_t, dec, v, keep):
        outs = []
        for h in range(GLA_HEADS):
            ks = slice(h * GLA_DK, (h + 1) * GLA_DK)
            vh = v[:, h * GLA_DV:(h + 1) * GLA_DV]
            s_h = s_scr[ks, :]
            att = jnp.where(keep, _dot(qd[:, ks], kd_t[ks, :]), 0.0).astype(BF16)
            outs.append(_dot(att, vh) + _dot(qe[:, ks], s_h.astype(BF16)))
            s_scr[ks, :] = s_h * dec[ks, :] + _dot(kl_t[ks, :], vh)
        return outs

    s_scr[...] = jnp.zeros(s_scr.shape, F32)

    def loop1(blk, carry):
        rows = block_rows(blk)
        a = gate_ref[rows, :]
        q = main_ref[rows, 0:GLA_QK_W].astype(F32) * (GLA_DK ** -0.5)
        k = main_ref[rows, GLA_QK_W:2 * GLA_QK_W].astype(F32)
        v = main_ref[rows, 2 * GLA_QK_W:2 * GLA_QK_W + GLA_V_W]
        fwd = decay_terms(a, q, k, waf_ref, baf_ref, tri_f, C - 1)
        bwd = decay_terms(a, q, k, wab_ref, bab_ref, tri_b, 0)
        for j in range(cpb):
            cr = chunk_rows(blk, j)
            qd, kd, qe, kl, dec = bwd[j]
            qd_scr[cr, :] = qd
            kd_scr[dec_rows(blk, j), :] = kd
            qe_scr[cr, :] = qe
            kl_scr[dec_rows(blk, j), :] = kl
            dec_scr[dec_rows(blk, j), :] = dec
            qd, kd, qe, kl, dec = fwd[j]
            outs = step(qd, kd, qe, kl, dec, v[j * C:(j + 1) * C], lower)
            of_scr[cr, :] = jnp.concatenate(outs, axis=1)
        return carry

    lax.fori_loop(0, GLA_NBLOCK, loop1, 0, unroll=GLA_UNROLL)

    s_scr[...] = jnp.zeros(s_scr.shape, F32)

    def loop2(i, carry):
        blk = jnp.where(i < GLA_CTX_BLOCKS, GLA_CTX_BLOCKS - 1 - i, GLA_NBLOCK - 1 - (i - GLA_CTX_BLOCKS))
        for j in reversed(range(cpb)):
            cr = chunk_rows(blk, j)
            v = main_ref[cr, 2 * GLA_QK_W:2 * GLA_QK_W + GLA_V_W]
            kr = dec_rows(blk, j)
            outs = step(qd_scr[cr, :], kd_scr[kr, :], qe_scr[cr, :], kl_scr[kr, :], dec_scr[kr, :], v, upper)
            gout = main_ref[cr, 2 * GLA_QK_W + GLA_V_W:GLA_MAIN_W].astype(F32)
            res = []
            for h in range(GLA_HEADS):
                vs = slice(h * GLA_DV, (h + 1) * GLA_DV)
                o = outs[h] + of_scr[cr, vs]
                gh = gout[:, vs]
                res.append(_rms(o, og_ref[...]) * (gh * jax.nn.sigmoid(gh)))
            o_ref[cr, :] = jnp.concatenate(res, axis=1).astype(o_ref.dtype)
        return carry

    lax.fori_loop(0, GLA_NBLOCK, loop2, 0, unroll=GLA_UNROLL)


def _gla(p, waf, baf, wab, bab, onorm_g):
    gate_col = (EV_W - LANE) // LANE
    qk_scr = pltpu.VMEM((TOK, GLA_QK_W), BF16)
    qk_t_scr = pltpu.VMEM((GLA_NCHUNK * GLA_QK_W, GLA_CHUNK), BF16)
    return pl.pallas_call(
        _gla_kernel,
        out_shape=jax.ShapeDtypeStruct((NTOK, GLA_V_W), BF16),
        grid=(BATCH,),
        in_specs=[pl.BlockSpec((TOK, GLA_MAIN_W), lambda b: (b, 0)),
                  pl.BlockSpec((TOK, LANE), lambda b: (b, gate_col)),
                  _resident((LANE, GLA_QK_W)), _resident((1, GLA_QK_W)),
                  _resident((LANE, GLA_QK_W)), _resident((1, GLA_QK_W)),
                  _resident((1, GLA_DV))],
        out_specs=pl.BlockSpec((TOK, GLA_V_W), lambda b: (b, 0)),
        scratch_shapes=[pltpu.VMEM((TOK, GLA_V_W), F32), qk_scr, qk_t_scr, qk_scr, qk_t_scr,
                        pltpu.VMEM((GLA_NCHUNK * GLA_QK_W, LANE), F32),
                        pltpu.VMEM((GLA_QK_W, GLA_DV), F32)],
        compiler_params=_params(("parallel",)),
        name="gla",
    )(p, p, waf, baf, wab, bab, onorm_g.reshape(1, GLA_DV))


def _gelu(x):
    return 0.5 * x * (1.0 + lax.erf(x * (2.0 ** -0.5)))


SG_TILE = 1152


def _sg_kernel(u_ref, v_ref, vg_ref, ws_ref, bs_ref, o_ref):
    for c in range(SG_TILE // SG_CHUNK):
        rows = slice(c * SG_CHUNK, (c + 1) * SG_CHUNK)
        for g in range(SG_GROUPS):
            cols = slice(g * SG_DIM, (g + 1) * SG_DIM)
            u = _gelu(u_ref[rows, cols].astype(F32))
            vn = _lane_rms(_gelu(v_ref[rows, cols].astype(F32)), vg_ref[g:g + 1, :])
            mixed = _dot(ws_ref[g], vn.astype(BF16)) + bs_ref[g]
            o_ref[rows, cols] = (u * mixed).astype(o_ref.dtype)


def _spatial_gating(p, vnorm_g, ws, bs):
    width = SG_GROUPS * SG_DIM
    u_col = GLA_MAIN_W // width
    bias = jnp.broadcast_to(bs[:, :, None], (SG_GROUPS, SG_CHUNK, SG_DIM))
    return pl.pallas_call(
        _sg_kernel,
        out_shape=jax.ShapeDtypeStruct((NTOK, width), BF16),
        grid=(NTOK // SG_TILE,),
        in_specs=[pl.BlockSpec((SG_TILE, width), lambda i: (i, u_col)),
                  pl.BlockSpec((SG_TILE, width), lambda i: (i, u_col + 1)),
                  _resident((SG_GROUPS, SG_DIM)),
                  _resident((SG_GROUPS, SG_CHUNK, SG_CHUNK)),
                  _resident((SG_GROUPS, SG_CHUNK, SG_DIM))],
        out_specs=pl.BlockSpec((SG_TILE, width), lambda i: (i, 0)),
        compiler_params=_params(("parallel",)),
        name="spatial_gating",
    )(p, p, vnorm_g, ws, bias)


FT_W = FT_GROUPS * FT_DIM


def _dft_cos_sin(n):
    jk = np.outer(np.arange(n), np.arange(n)) % n
    ang = 2.0 * np.pi * jk.astype(np.float64) / n
    return np.cos(ang), np.sin(ang)


def _parity_dft(n):
    p = np.arange(n // 2)[:, None]
    mats = []
    for first in (0, 1):
        t = 2 * np.arange(n // 2)[None, :] + first
        ang = 2.0 * np.pi * ((p * t) % n).astype(np.float64) / n
        mats.append(np.concatenate([np.cos(ang), -np.sin(ang)], axis=1))
    return jnp.asarray(np.stack(mats), F32).astype(BF16)


def _fourier_kernel(h_ref, cs_ref, dft_x_ref, dft_c_ref, o_ref, ab_scr, *, need_ctx):
    ab = _dot(h_ref[...], cs_ref[...])
    n_slab = 2 * FT_W // LANE
    for c in range(n_slab):
        ab_scr[c] = ab[:, c * LANE:(c + 1) * LANE]

    def position_dft(first_row, length, dft_ref):
        half = length // 2
        parts = []
        for parity in (0, 1):
            slabs = [ab_scr[c, pl.ds(first_row + parity, half, stride=2), :].astype(BF16) for c in range(n_slab)]
            a = jnp.concatenate(slabs[:n_slab // 2], axis=1)
            b = jnp.concatenate(slabs[n_slab // 2:], axis=1)
            parts.append(_dot(dft_ref[parity], jnp.concatenate([a, b], axis=0)))
        scale = (length * FT_DIM) ** -0.5
        o_ref[first_row:first_row + half, :] = ((parts[0] + parts[1]) * scale).astype(o_ref.dtype)
        o_ref[first_row + half:first_row + length, :] = ((parts[0] - parts[1]) * scale).astype(o_ref.dtype)

    position_dft(CTX_LEN, SEQ, dft_x_ref)
    if need_ctx:
        position_dft(0, CTX_LEN, dft_c_ref)
    else:
        o_ref[:CTX_LEN, :] = jnp.zeros((CTX_LEN, FT_W), o_ref.dtype)


def _fourier(p, need_ctx):
    cc, sc = _dft_cos_sin(FT_DIM)
    eye = np.eye(FT_GROUPS)
    cs = jnp.asarray(np.concatenate([np.kron(eye, cc), np.kron(eye, sc)], axis=1), F32).astype(BF16)
    return pl.pallas_call(
        functools.partial(_fourier_kernel, need_ctx=need_ctx),
        out_shape=jax.ShapeDtypeStruct((NTOK, FT_W), BF16),
        grid=(BATCH,),
        in_specs=[pl.BlockSpec((TOK, FT_W), lambda b: (b, 0)),
                  _resident((FT_W, 2 * FT_W)),
                  _resident((2, SEQ // 2, SEQ)), _resident((2, CTX_LEN // 2, CTX_LEN))],
        out_specs=pl.BlockSpec((TOK, FT_W), lambda b: (b, 0)),
        scratch_shapes=[pltpu.VMEM((2 * FT_W // LANE, TOK, LANE), F32)],
        compiler_params=_params(("parallel",)),
        name="fourier",
    )(p, cs, _parity_dft(SEQ), _parity_dft(CTX_LEN))


MLA_NOPE_W = MLA_HEADS * MLA_NOPE
MLA_ROPE_W = MLA_HEADS * MLA_ROPE
OD_QA0 = FT_W
OD_KVA0 = OD_QA0 + MLA_Q_RANK
OD_KPE0 = OD_KVA0 + MLA_KV_RANK
OD_KPE_ROT0 = OD_KPE0 + MLA_ROPE


def _rot_cols(w):
    q = MLA_ROPE // 4
    return jnp.concatenate([-w[..., q:2 * q], w[..., 0:q], -w[..., 3 * q:4 * q], w[..., 2 * q:3 * q]], axis=-1)


def _rot_gain(g):
    q = MLA_ROPE // 4
    return jnp.concatenate([g[..., q:2 * q], g[..., 0:q], g[..., 3 * q:4 * q], g[..., 2 * q:3 * q]], axis=-1)


def _rope_tables():
    rows = SEQ // GRID_W
    row_id = np.repeat(np.arange(rows, dtype=np.float32), GRID_W)
    col_id = np.tile(np.arange(GRID_W, dtype=np.float32), rows)
    axis_dim = MLA_ROPE // 2
    inv_freq = (np.float32(ROPE_BASE) ** (-np.arange(0, axis_dim, 2, dtype=np.float32) / np.float32(axis_dim))).astype(np.float32)
    ang_r = row_id[:, None] * inv_freq
    ang_c = col_id[:, None] * inv_freq
    ang = np.concatenate([ang_r, ang_r, ang_c, ang_c], axis=-1).astype(np.float32)
    cos = np.concatenate([np.ones((CTX_LEN, MLA_ROPE), np.float32), np.cos(ang)], axis=0)
    sin = np.concatenate([np.zeros((CTX_LEN, MLA_ROPE), np.float32), np.sin(ang)], axis=0)
    return tuple(jnp.asarray(t, F32) for t in (cos, sin, cos.T, sin.T))


MLA_VT_ROWS = MLA_V + 16


def _mla_heads(p, cos, sin, cos_t, sin_t, qag, wuq_t, kvag, wukv_k, wukv_vt, head_sum, kg,
               qg_nope, qg_rope, qg_rot, qt_ref, k_ref, vt_ref):
    qa = _lane_rms(p[:, OD_QA0:OD_KVA0].astype(F32), qag)
    q_t = _dot(wuq_t, qa.T.astype(BF16))
    for h in range(MLA_HEADS):
        qn = q_t[h * MLA_NOPE:(h + 1) * MLA_NOPE]
        qr = q_t[MLA_NOPE_W + h * MLA_ROPE:MLA_NOPE_W + (h + 1) * MLA_ROPE]
        qrr = q_t[MLA_NOPE_W + MLA_ROPE_W + h * MLA_ROPE:MLA_NOPE_W + MLA_ROPE_W + (h + 1) * MLA_ROPE]
        ss = jnp.sum(qn * qn, axis=0, keepdims=True) + jnp.sum(qr * qr, axis=0, keepdims=True)
        inv = lax.rsqrt(ss * (1.0 / MLA_QK) + RMS_EPS) * (MLA_QK ** -0.5 * LOG2_E)
        q_rope = qr * qg_rope * cos_t + qrr * qg_rot * sin_t
        qt_ref[h] = jnp.concatenate([qn * qg_nope * inv, q_rope * inv], axis=0).astype(qt_ref.dtype)

    kva = _lane_rms(p[:, OD_KVA0:OD_KPE0].astype(F32), kvag)
    v_t = _dot(wukv_vt, kva.T.astype(BF16))
    ones = jnp.ones((MLA_VT_ROWS - MLA_V, v_t.shape[1]), F32)
    for h in range(MLA_HEADS):
        vt_ref[h] = jnp.concatenate([v_t[h * MLA_V:(h + 1) * MLA_V], ones], axis=0).astype(vt_ref.dtype)
    kn = _dot(kva.astype(BF16), wukv_k)
    kpe = p[:, OD_KPE0:OD_KPE_ROT0].astype(F32)
    kpe_rot = p[:, OD_KPE_ROT0:OD_W].astype(F32)
    sq = jnp.concatenate([kn * kn, kpe * kpe], axis=1).astype(BF16)
    ss = _dot(sq, head_sum)
    inv = lax.rsqrt(ss * (1.0 / MLA_QK) + RMS_EPS)
    k_rope = kpe * kg[:, MLA_NOPE:MLA_QK] * cos + kpe_rot * kg[:, MLA_QK:] * sin
    for h in range(MLA_HEADS):
        k_h = jnp.concatenate([kn[:, h * MLA_NOPE:(h + 1) * MLA_NOPE] * kg[:, :MLA_NOPE], k_rope], axis=1)
        k_ref[h] = (k_h * inv[:, h:h + 1]).astype(k_ref.dtype)


def _key_head_sum_matrix():
    head = np.arange(LANE)[None, :]
    nope = np.repeat(np.arange(MLA_HEADS), MLA_NOPE)[:, None] == head
    rope = np.broadcast_to(head < MLA_HEADS, (MLA_ROPE, LANE))
    return jnp.asarray(np.concatenate([nope, rope], axis=0), BF16)


MLA_TILE = 768


def _mla_prep_kernel(p_ref, *refs):
    *operand_refs, qt_ref, k_ref, vt_ref = refs
    _mla_heads(p_ref[...], *(r[...] for r in operand_refs), qt_ref, k_ref, vt_ref)


def _mla_prep(p, rope_tabs, qa_g, wuq, kva_g, wukv, qn_g, kn_g):
    wukv = _wukv_cols(wukv)
    kgain = jnp.concatenate([kn_g, _rot_gain(kn_g[MLA_NOPE:])]).reshape(1, -1)
    col = lambda g: jnp.broadcast_to(g[:, None], (g.shape[0], MLA_TILE))
    operands = [*rope_tabs, qa_g.reshape(1, -1), _wuq_cols(wuq).T, kva_g.reshape(1, -1),
                wukv[:, :MLA_NOPE_W], wukv[:, MLA_NOPE_W:].T, _key_head_sum_matrix(), kgain,
                col(qn_g[:MLA_NOPE]), col(qn_g[MLA_NOPE:]), col(_rot_gain(qn_g[MLA_NOPE:]))]
    tiles_per_batch = TOK // MLA_TILE
    rope = pl.BlockSpec((MLA_TILE, MLA_ROPE), lambda i: (i % tiles_per_batch, 0))
    rope_t = pl.BlockSpec((MLA_ROPE, MLA_TILE), lambda i: (0, i % tiles_per_batch))
    rows = lambda d: pl.BlockSpec((MLA_HEADS, MLA_TILE, d), lambda i: (0, i, 0))
    cols = lambda d: pl.BlockSpec((MLA_HEADS, d, MLA_TILE), lambda i: (0, 0, i))
    return pl.pallas_call(
        _mla_prep_kernel,
        out_shape=(jax.ShapeDtypeStruct((MLA_HEADS, MLA_QK, NTOK), BF16),
                   jax.ShapeDtypeStruct((MLA_HEADS, NTOK, MLA_QK), BF16),
                   jax.ShapeDtypeStruct((MLA_HEADS, MLA_VT_ROWS, NTOK), BF16)),
        grid=(NTOK // MLA_TILE,),
        in_specs=[pl.BlockSpec((MLA_TILE, OD_W), lambda i: (i, 0)), rope, rope, rope_t, rope_t]
                 + [_resident(a.shape) for a in operands[4:]],
        out_specs=(cols(MLA_QK), rows(MLA_QK), cols(MLA_VT_ROWS)),
        compiler_params=_params(("parallel",)),
        name="mla_prep",
    )(p, *operands)


ATT_Q_TILE = 512


ATT_KEY_CHUNK = TOK


def _weighted_values(acc):
    return (acc[:MLA_V] * (1.0 / acc[MLA_V:MLA_V + 1])).T


def _attn_kernel(qt_ref, k_ref, vt_ref, o_ref, s_scr, *, need_ctx):
    n_q = SEQ // ATT_Q_TILE
    chunks = [slice(c * ATT_KEY_CHUNK, (c + 1) * ATT_KEY_CHUNK) for c in range(TOK // ATT_KEY_CHUNK)]
    q_cols = lambda j: slice(CTX_LEN + j * ATT_Q_TILE, CTX_LEN + (j + 1) * ATT_Q_TILE)

    def scores(j, rows, col_max):
        s = _dot(k_ref[0, rows, :], qt_ref[0, :, q_cols(j)])
        s_scr[j % 2, rows, :] = s
        m = jnp.max(s, axis=0, keepdims=True)
        return m if col_max is None else jnp.maximum(col_max, m)

    col_max = None
    for rows in chunks:
        col_max = scores(0, rows, col_max)
    for j in range(n_q):
        acc, next_max = None, None
        for rows in chunks:
            if j + 1 < n_q:
                next_max = scores(j + 1, rows, next_max)
            p = jnp.exp2(s_scr[j % 2, rows, :] - col_max).astype(BF16)
            part = _dot(vt_ref[0, :, rows], p)
            acc = part if acc is None else acc + part
        o_ref[q_cols(j), :] = _weighted_values(acc).astype(o_ref.dtype)
        col_max = next_max

    if need_ctx:
        s = _dot(k_ref[0, :CTX_LEN, :], qt_ref[0, :, :CTX_LEN])
        p = jnp.exp2(s - jnp.max(s, axis=0, keepdims=True)).astype(BF16)
        o_ref[:CTX_LEN, :] = _weighted_values(_dot(vt_ref[0, :, :CTX_LEN], p)).astype(o_ref.dtype)
    else:
        o_ref[:CTX_LEN, :] = jnp.zeros((CTX_LEN, MLA_V), o_ref.dtype)


def _attention(q_t, k, v_t, need_ctx):
    transposed = lambda d: pl.BlockSpec((1, d, TOK), lambda b, h: (h, 0, b))
    return pl.pallas_call(
        functools.partial(_attn_kernel, need_ctx=need_ctx),
        out_shape=jax.ShapeDtypeStruct((NTOK, MLA_HEADS * MLA_V), BF16),
        grid=(BATCH, MLA_HEADS),
        in_specs=[transposed(MLA_QK),
                  pl.BlockSpec((1, TOK, MLA_QK), lambda b, h: (h, b, 0)),
                  transposed(MLA_VT_ROWS)],
        out_specs=pl.BlockSpec((TOK, MLA_V), lambda b, h: (b, h)),
        scratch_shapes=[pltpu.VMEM((2, TOK, ATT_Q_TILE), F32)],
        compiler_params=_params(("parallel", "parallel")),
        name="attention",
    )(q_t, k, v_t)


def _even_w_in(w):
    gates0 = GLA_MAIN_W
    gates1 = gates0 + 2 * GLA_GATE_RANK
    w = w.astype(BF16)
    pad = jnp.zeros(w.shape[:-1] + (EV_W - w.shape[-1],), w.dtype)
    return jnp.concatenate([w[..., :gates0], w[..., gates1:], w[..., gates0:gates1], pad], axis=-1)


def _gate_w(wa, first_row):
    out = jnp.zeros((LANE, GLA_QK_W), wa.dtype)
    return lax.dynamic_update_slice(out, wa, (first_row, 0)).astype(BF16)


def _odd_w_in(w):
    w = w.astype(BF16)
    return jnp.concatenate([w, _rot_cols(w[..., OD_KPE0:OD_KPE_ROT0])], axis=-1)


def _wuq_cols(w):
    w = w.reshape(MLA_Q_RANK, MLA_HEADS, MLA_QK)
    nope = w[:, :, :MLA_NOPE].reshape(MLA_Q_RANK, MLA_NOPE_W)
    rope = w[:, :, MLA_NOPE:]
    return jnp.concatenate([nope, rope.reshape(MLA_Q_RANK, MLA_ROPE_W),
                            _rot_cols(rope).reshape(MLA_Q_RANK, MLA_ROPE_W)], axis=1).astype(BF16)


def _wukv_cols(w):
    w = w.reshape(MLA_KV_RANK, MLA_HEADS, MLA_NOPE + MLA_V)
    return jnp.concatenate([w[:, :, :MLA_NOPE].reshape(MLA_KV_RANK, MLA_NOPE_W),
                            w[:, :, MLA_NOPE:].reshape(MLA_KV_RANK, MLA_HEADS * MLA_V)], axis=1).astype(BF16)


def kernel(x, c, ctx, c_ctx, ada_w, ada_b, norm_mix_g, norm_ffn_g, w_mix_out, ffn_w_in, ffn_w_out, ev_w_in, gla_wa_f, gla_ba_f, gla_wa_b, gla_ba_b, gla_onorm_g, sg_vnorm_g, sg_ws, sg_bs, od_w_in, mla_qa_g, mla_wuq, mla_kva_g, mla_wukv, mla_qn_g, mla_kn_g):
    cvec = jnp.zeros((MOD_ROWS, D_MODEL), F32).at[:BATCH].set(c).at[CTX_MOD_ROW].set(c_ctx)
    mod = _ada_table(cvec, ada_w, ada_b)
    rope_tabs = _rope_tables()
    norm_mix_g = norm_mix_g.reshape(DEPTH, 1, D_MODEL)
    norm_ffn_g = norm_ffn_g.reshape(DEPTH, 1, D_MODEL)
    w_in_stacks = (_even_w_in(ev_w_in), _odd_w_in(od_w_in))
    dense_f32 = (w_mix_out, ffn_w_in, ffn_w_out)

    p, xa, dense = _first_inproj(x, ctx, mod, norm_mix_g, w_in_stacks[0], dense_f32)
    for l in range(DEPTH):
        last = l == DEPTH - 1
        i = l // 2
        if l % 2 == 0:
            ma = _gla(p, _gate_w(gla_wa_f[i], 0), gla_ba_f[i].reshape(1, GLA_QK_W),
                      _gate_w(gla_wa_b[i], GLA_GATE_RANK), gla_ba_b[i].reshape(1, GLA_QK_W), gla_onorm_g[i])
            mb = _spatial_gating(p, sg_vnorm_g[i], sg_ws[i].astype(BF16), sg_bs[i])
        else:
            q_t, k, v_t = _mla_prep(p, rope_tabs, mla_qa_g[i], mla_wuq[i], mla_kva_g[i], mla_wukv[i],
                                    mla_qn_g[i], mla_kn_g[i])
            ma = _fourier(p, not last)
            mb = _attention(q_t, k, v_t, not last)
        nxt = None if last else (norm_mix_g, w_in_stacks[(l + 1) % 2], (l + 1) // 2, dense_f32)
        xa, p, dense = _layer_tail(xa, ma, mb, mod, l, norm_ffn_g, dense, nxt)
    return xa.reshape(BATCH, SEQ, D_MODEL)
```

```python
import functools

import numpy as np
import jax
import jax.numpy as jnp
from jax import lax
from jax.experimental import pallas as pl
from jax.experimental.pallas import tpu as pltpu

D_MODEL = 1024
BATCH = 8
SEQ = 2048
DEPTH = 4
CTX_LEN = 256
GRID_W = 64
RMS_EPS = 1e-6
GLA_HEADS = 4
GLA_DK = 64
GLA_DV = 128
GLA_GATE_RANK = 16
GLA_GATE_NORM = 16.0
GLA_CHUNK = 64
SG_GROUPS = 4
SG_DIM = 128
SG_CHUNK = 128
FT_GROUPS = 4
FT_DIM = 64
MLA_HEADS = 6
MLA_NOPE = 128
MLA_ROPE = 64
MLA_V = 128
MLA_Q_RANK = 384
MLA_KV_RANK = 256
ROPE_BASE = 10000.0
FFN_HIDDEN = 2816

TOK = CTX_LEN + SEQ
NTOK = BATCH * TOK
ROW_TILE = 256
TILES_PER_BATCH = TOK // ROW_TILE
CTX_TILES = CTX_LEN // ROW_TILE
MOD_ROWS = 16
CTX_MOD_ROW = BATCH
LANE = 128
EV_W = 2688
OD_W = 1024
MLA_QK = MLA_NOPE + MLA_ROPE
VMEM_LIMIT = 56 * 1024 * 1024

F32 = jnp.float32
BF16 = jnp.bfloat16


def _dot(a, b):
    return jnp.dot(a, b, preferred_element_type=F32)


def _rms(x, g):
    return x * lax.rsqrt(jnp.mean(x * x, axis=-1, keepdims=True) + RMS_EPS) * g


def _lane_rms(x, g):
    width = x.shape[1]
    ss = _dot((x * x).astype(BF16), jnp.ones((width, LANE), BF16))
    inv = lax.rsqrt(ss * (1.0 / width) + RMS_EPS)
    return x * jnp.concatenate([inv] * (width // LANE), axis=1) * g


def _params(sem):
    return pltpu.CompilerParams(dimension_semantics=sem, vmem_limit_bytes=VMEM_LIMIT)


def _resident(shape):
    zeros = (0,) * len(shape)
    return pl.BlockSpec(shape, lambda *_: zeros, pipeline_mode=pl.Buffered(1))


def _mod_row(i):
    return jnp.where(i % TILES_PER_BATCH < CTX_TILES, CTX_MOD_ROW, i // TILES_PER_BATCH)


MOD_PIECES = 6


def _mod_spec(layer, tile_of_step):
    def index_map(i):
        return (layer * MOD_ROWS + _mod_row(tile_of_step(i)), 0, 0)
    return pl.BlockSpec((1, MOD_PIECES, D_MODEL), index_map)


def _ada_kernel(c_ref, w_ref, b_ref, o_ref):
    c = c_ref[...]
    s = c * jax.nn.sigmoid(c)
    w = w_ref[0]
    w_hi = w.astype(BF16)
    w_lo = (w - w_hi.astype(F32)).astype(BF16)
    s_hi = s.astype(BF16)
    s_lo = (s - s_hi.astype(F32)).astype(BF16)
    o_ref[0] = _dot(s_hi, w_hi) + (_dot(s_hi, w_lo) + _dot(s_lo, w_hi)) + b_ref[0]


def _ada_table(cvec, ada_w, ada_b):
    tn = 3072
    n = MOD_PIECES * D_MODEL
    out = pl.pallas_call(
        _ada_kernel,
        out_shape=jax.ShapeDtypeStruct((DEPTH, MOD_ROWS, n), F32),
        grid=(DEPTH, n // tn),
        in_specs=[pl.BlockSpec((MOD_ROWS, D_MODEL), lambda l, j: (0, 0)),
                  pl.BlockSpec((1, D_MODEL, tn), lambda l, j: (l, 0, j)),
                  pl.BlockSpec((1, 1, tn), lambda l, j: (l, 0, j))],
        out_specs=pl.BlockSpec((1, MOD_ROWS, tn), lambda l, j: (l, 0, j)),
        compiler_params=_params(("arbitrary", "arbitrary")),
        name="ada_table",
    )(cvec, ada_w, ada_b.reshape(DEPTH, 1, n))
    return out.reshape(DEPTH * MOD_ROWS, MOD_PIECES, D_MODEL)


def _layer_block(shape, layer):
    zeros = (0,) * len(shape)
    return pl.BlockSpec((1,) + tuple(shape), lambda *_: (layer,) + zeros, pipeline_mode=pl.Buffered(1))


BF16_SUBLANES = 16


def _copy_plumbing(stacks, layer, n_steps):
    views, in_specs, shapes, out_specs = [], [], [], []
    for w in stacks:
        n_layers, n_rows, cols = w.shape
        rows = next(r for r in range(BF16_SUBLANES, n_rows + 1, BF16_SUBLANES)
                    if n_rows % r == 0 and n_rows // r <= n_steps)
        n_slices = n_rows // rows
        views.append(w.reshape(n_layers * n_rows, cols))
        in_specs.append(pl.BlockSpec(
            (rows, cols), lambda i, n=n_slices: (layer * n + jnp.minimum(i, n - 1), 0)))
        shapes.append(jax.ShapeDtypeStruct((n_rows, cols), BF16))
        out_specs.append(pl.BlockSpec((rows, cols), lambda i, n=n_slices: (jnp.minimum(i, n - 1), 0)))
    return views, in_specs, shapes, out_specs


def _as_layer_stacks(copies, stacks):
    return [c.reshape(1, w.shape[1], w.shape[2]) for c, w in zip(copies, stacks)]


def _modulated_norm(x, g, shift, scale):
    return _rms(x, g) * (1.0 + scale) + shift


TAIL_TILES = 2


def _first_inproj_kernel(*refs):
    n = TAIL_TILES
    x_refs, ctx_refs, mod_refs = refs[:n], refs[n:2 * n], refs[2 * n:3 * n]
    g_ref, w_ref, *copy_refs, p_ref, xa_ref, mix_copy, in_copy, out_copy = refs[3 * n:]
    rows = []
    for t in range(n):
        is_ctx = (n * pl.program_id(0) + t) % TILES_PER_BATCH < CTX_TILES
        x = jnp.where(is_ctx, ctx_refs[t][...], x_refs[t][...])
        xa_ref[t * ROW_TILE:(t + 1) * ROW_TILE, :] = x
        mod = mod_refs[t][0]
        rows.append(_modulated_norm(x, g_ref[0], mod[0:1], mod[1:2]).astype(BF16))
    p_ref[...] = _dot(jnp.concatenate(rows, axis=0), w_ref[0]).astype(p_ref.dtype)
    for src, dst in zip(copy_refs, (mix_copy, in_copy, out_copy)):
        dst[...] = src[...].astype(BF16)


def _first_inproj(x, ctx, mod, norm_g, w, dense_f32):
    width = w.shape[-1]
    n_steps = NTOK // (TAIL_TILES * ROW_TILE)
    views, copy_in, copy_shapes, copy_out = _copy_plumbing(dense_f32, 0, n_steps)
    x_tiles = SEQ // ROW_TILE
    tiles = [lambda i, t=t: TAIL_TILES * i + t for t in range(TAIL_TILES)]
    x_tile = lambda j: ((j // TILES_PER_BATCH) * x_tiles + jnp.maximum(j % TILES_PER_BATCH - CTX_TILES, 0), 0)
    ctx_tile = lambda j: ((j // TILES_PER_BATCH) * CTX_TILES + jnp.minimum(j % TILES_PER_BATCH, CTX_TILES - 1), 0)
    source = lambda tile_index: [pl.BlockSpec((ROW_TILE, D_MODEL), lambda i, f=f: tile_index(f(i))) for f in tiles]
    row_out = lambda wd: pl.BlockSpec((TAIL_TILES * ROW_TILE, wd), lambda i: (i, 0))
    per_tile = lambda a: [a] * TAIL_TILES
    p, xa, *copies = pl.pallas_call(
        _first_inproj_kernel,
        out_shape=(jax.ShapeDtypeStruct((NTOK, width), BF16), jax.ShapeDtypeStruct((NTOK, D_MODEL), F32),
                   *copy_shapes),
        grid=(n_steps,),
        in_specs=[*source(x_tile), *source(ctx_tile), *[_mod_spec(0, f) for f in tiles],
                  _layer_block((1, D_MODEL), 0),
                  _layer_block((D_MODEL, width), 0),
                  *copy_in],
        out_specs=(row_out(width), row_out(D_MODEL), *copy_out),
        compiler_params=_params(("arbitrary",)),
        name="first_inproj",
    )(*per_tile(x.reshape(BATCH * SEQ, D_MODEL)), *per_tile(ctx.reshape(BATCH * CTX_LEN, D_MODEL)),
      *per_tile(mod), norm_g, w, *views)
    return p, xa, _as_layer_stacks(copies, dense_f32)


def _tail_kernel(*refs, has_next):
    n = TAIL_TILES
    x_refs, ma_refs, mb_refs, mod_refs = refs[:n], refs[n:2 * n], refs[2 * n:3 * n], refs[3 * n:4 * n]
    g_ref, wmix_ref, win_ref, wout_ref, *rest = refs[4 * n:]
    mods = [r[0] for r in mod_refs]
    stack = lambda tiles: jnp.concatenate(tiles, axis=0)
    unstack = lambda rows: [rows[t * ROW_TILE:(t + 1) * ROW_TILE] for t in range(n)]

    m = stack([jnp.concatenate([a[...], b[...]], axis=1) for a, b in zip(ma_refs, mb_refs)])
    x1 = [x[...] + mod[2:3] * y for x, mod, y in zip(x_refs, mods, unstack(_dot(m, wmix_ref[0])))]
    h = stack([_modulated_norm(x, g_ref[0], mod[3:4], mod[4:5]).astype(BF16) for x, mod in zip(x1, mods)])
    gu = _dot(h, win_ref[0])
    gate = gu[:, :FFN_HIDDEN]
    act = (gate * jax.nn.sigmoid(gate) * gu[:, FFN_HIDDEN:]).astype(BF16)
    x2 = [x + mod[5:6] * d for x, mod, d in zip(x1, mods, unstack(_dot(act, wout_ref[0])))]
    if has_next:
        next_mods = [r[0] for r in rest[:n]]
        gn_ref, wn_ref, *copy_refs, o_ref, p_ref, mix_copy, in_copy, out_copy = rest[n:]
        z = stack([_modulated_norm(x, gn_ref[0], mod[0:1], mod[1:2]).astype(BF16) for x, mod in zip(x2, next_mods)])
        p_ref[...] = _dot(z, wn_ref[0]).astype(p_ref.dtype)
        for src, dst in zip(copy_refs, (mix_copy, in_copy, out_copy)):
            dst[...] = src[...].astype(BF16)
    else:
        (o_ref,) = rest
    o_ref[...] = stack(x2)


def _layer_tail(xa, ma, mb, mod, layer, norm_ffn_g, dense, nxt):
    wmix, win, wout = dense
    if nxt is None:
        x_tiles = SEQ // ROW_TILE
        n_tiles = BATCH * x_tiles
        tile = lambda i: (i // x_tiles) * TILES_PER_BATCH + CTX_TILES + i % x_tiles
        out_rows = BATCH * SEQ
    else:
        n_tiles = NTOK // ROW_TILE
        tile = lambda i: i
        out_rows = NTOK
    n_steps = n_tiles // TAIL_TILES
    tiles = [lambda i, t=t: tile(TAIL_TILES * i + t) for t in range(TAIL_TILES)]
    row_in = lambda width: [pl.BlockSpec((ROW_TILE, width), lambda i, f=f: (f(i), 0)) for f in tiles]
    row_out = lambda width: pl.BlockSpec((TAIL_TILES * ROW_TILE, width), lambda i: (i, 0))
    in_specs = [*row_in(D_MODEL), *row_in(ma.shape[1]), *row_in(mb.shape[1]),
                *[_mod_spec(layer, f) for f in tiles],
                _layer_block((1, D_MODEL), layer),
                _layer_block((D_MODEL, D_MODEL), 0),
                _layer_block((D_MODEL, 2 * FFN_HIDDEN), 0),
                _layer_block((FFN_HIDDEN, D_MODEL), 0)]
    per_tile = lambda a: [a] * TAIL_TILES
    args = [*per_tile(xa), *per_tile(ma), *per_tile(mb), *per_tile(mod), norm_ffn_g, wmix, win, wout]
    out_shape = [jax.ShapeDtypeStruct((out_rows, D_MODEL), F32)]
    out_specs = [row_out(D_MODEL)]
    if nxt is not None:
        norm_mix_g, w_next, idx, dense_f32 = nxt
        width = w_next.shape[-1]
        views, copy_in, copy_shapes, copy_out = _copy_plumbing(dense_f32, layer + 1, n_steps)
        in_specs += [*[_mod_spec(layer + 1, f) for f in tiles],
                     _layer_block((1, D_MODEL), layer + 1), _layer_block((D_MODEL, width), idx)] + copy_in
        args += [*per_tile(mod), norm_mix_g, w_next] + views
        out_shape += [jax.ShapeDtypeStruct((out_rows, width), BF16)] + copy_shapes
        out_specs += [row_out(width)] + copy_out
    res = pl.pallas_call(
        functools.partial(_tail_kernel, has_next=nxt is not None),
        out_shape=tuple(out_shape),
        grid=(n_steps,),
        in_specs=in_specs,
        out_specs=tuple(out_specs),
        compiler_params=_params(("arbitrary",)),
        name="layer_tail",
    )(*args)
    if nxt is None:
        return res[0], None, None
    return res[0], res[1], _as_layer_stacks(res[2:], dense_f32)


GLA_QK_W = GLA_HEADS * GLA_DK
GLA_V_W = GLA_HEADS * GLA_DV
GLA_MAIN_W = 2 * GLA_QK_W + 2 * GLA_V_W
GLA_NCHUNK = TOK // GLA_CHUNK


GLA_BLOCK = 256
GLA_NBLOCK = TOK // GLA_BLOCK
GLA_CTX_BLOCKS = CTX_LEN // GLA_BLOCK
GLA_UNROLL = GLA_NBLOCK


LOG2_E = 1.4426950408889634


def _log2_sigmoid(y, scale):
    return jnp.minimum(y, 0.0) * (LOG2_E * scale) - jnp.log2(1.0 + jnp.exp2(jnp.abs(y) * (-LOG2_E))) * scale


def _gla_kernel(main_ref, gate_ref, waf_ref, baf_ref, wab_ref, bab_ref, og_ref, o_ref,
                of_scr, qd_scr, kd_scr, qe_scr, kl_scr, dec_scr, s_scr):
    C, BLK = GLA_CHUNK, GLA_BLOCK
    cpb = BLK // C
    rb = lax.broadcasted_iota(jnp.int32, (BLK, BLK), 0)
    cb = lax.broadcasted_iota(jnp.int32, (BLK, BLK), 1)
    same_chunk = (rb // C) == (cb // C)
    tri_f = jnp.where(same_chunk & (rb >= cb), 1.0, 0.0).astype(BF16)
    tri_b = jnp.where(same_chunk & (rb <= cb), 1.0, 0.0).astype(BF16)
    r = lax.broadcasted_iota(jnp.int32, (C, C), 0)
    cc = lax.broadcasted_iota(jnp.int32, (C, C), 1)
    lower, upper = r >= cc, r <= cc
    sr = lax.broadcasted_iota(jnp.int32, (BLK, cpb * LANE), 0)
    sc = lax.broadcasted_iota(jnp.int32, (BLK, cpb * LANE), 1)
    chunk_sum = jnp.where(sr // C == sc // LANE, 1.0, 0.0).astype(BF16)

    def split(x):
        hi = x.astype(BF16)
        return hi, (x - hi.astype(F32)).astype(BF16)

    def dot_01(m, x):
        hi, lo = split(x)
        return _dot(jnp.concatenate([m, m], axis=1), jnp.concatenate([hi, lo], axis=0))

    def dot_10(x, m):
        hi, lo = split(x)
        return _dot(jnp.concatenate([hi, lo], axis=1), jnp.concatenate([m, m], axis=0))

    def block_rows(blk):
        return pl.ds(pl.multiple_of(blk * BLK, BLK), BLK)

    def chunk_rows(blk, j):
        return pl.ds(pl.multiple_of(blk * BLK + j * C, C), C)

    def dec_rows(blk, j):
        return pl.ds(pl.multiple_of((blk * cpb + j) * GLA_QK_W, GLA_QK_W), GLA_QK_W)

    def decay_terms(a, q, k, w_ref, bias_ref, tri, end_row):
        g = _log2_sigmoid(_dot(a, w_ref[...]) + bias_ref[...], 1.0 / GLA_GATE_NORM)
        b = dot_01(tri, g)
        dec_all = jnp.exp2(dot_10(g.T, chunk_sum))
        terms = []
        for j in range(cpb):
            sl = slice(j * C, (j + 1) * C)
            bc = b[sl]
            b_mid = bc[C // 2:C // 2 + 1]
            b_end = bc[end_row:end_row + 1]
            d = bc - b_mid
            qd = q[sl] * jnp.exp2(d)
            kd = k[sl] * jnp.exp2(-d)
            qe = qd * jnp.exp2(b_mid)
            kl = kd * jnp.exp2(b_end - b_mid)
            dec = dec_all[:, j * LANE:(j + 1) * LANE]
            terms.append((qd.astype(BF16), kd.T.astype(BF16), qe.astype(BF16), kl.T.astype(BF16), dec))
        return terms

    def step(qd, kd_t, qe, kl_t, dec, v, keep):
        outs = []
        for h in range(GLA_HEADS):
            ks = slice(h * GLA_DK, (h + 1) * GLA_DK)
            vh = v[:, h * GLA_DV:(h + 1) * GLA_DV]
            s_h = s_scr[ks, :]
            att = jnp.where(keep, _dot(qd[:, ks], kd_t[ks, :]), 0.0).astype(BF16)
            outs.append(_dot(att, vh) + _dot(qe[:, ks], s_h.astype(BF16)))
            s_scr[ks, :] = s_h * dec[ks, :] + _dot(kl_t[ks, :], vh)
        return outs

    s_scr[...] = jnp.zeros(s_scr.shape, F32)

    def loop1(blk, carry):
        rows = block_rows(blk)
        a = gate_ref[rows, :]
        q = main_ref[rows, 0:GLA_QK_W].astype(F32) * (GLA_DK ** -0.5)
        k = main_ref[rows, GLA_QK_W:2 * GLA_QK_W].astype(F32)
        v = main_ref[rows, 2 * GLA_QK_W:2 * GLA_QK_W + GLA_V_W]
        fwd = decay_terms(a, q, k, waf_ref, baf_ref, tri_f, C - 1)
        bwd = decay_terms(a, q, k, wab_ref, bab_ref, tri_b, 0)
        for j in range(cpb):
            cr = chunk_rows(blk, j)
            qd, kd, qe, kl, dec = bwd[j]
            qd_scr[cr, :] = qd
            kd_scr[dec_rows(blk, j), :] = kd
            qe_scr[cr, :] = qe
            kl_scr[dec_rows(blk, j), :] = kl
            dec_scr[dec_rows(blk, j), :] = dec
            qd, kd, qe, kl, dec = fwd[j]
            outs = step(qd, kd, qe, kl, dec, v[j * C:(j + 1) * C], lower)
            of_scr[cr, :] = jnp.concatenate(outs, axis=1)
        return carry

    lax.fori_loop(0, GLA_NBLOCK, loop1, 0, unroll=GLA_UNROLL)

    s_scr[...] = jnp.zeros(s_scr.shape, F32)

    def loop2(i, carry):
        blk = jnp.where(i < GLA_CTX_BLOCKS, GLA_CTX_BLOCKS - 1 - i, GLA_NBLOCK - 1 - (i - GLA_CTX_BLOCKS))
        for j in reversed(range(cpb)):
            cr = chunk_rows(blk, j)
            v = main_ref[cr, 2 * GLA_QK_W:2 * GLA_QK_W + GLA_V_W]
            kr = dec_rows(blk, j)
            outs = step(qd_scr[cr, :], kd_scr[kr, :], qe_scr[cr, :], kl_scr[kr, :], dec_scr[kr, :], v, upper)
            gout = main_ref[cr, 2 * GLA_QK_W + GLA_V_W:GLA_MAIN_W].astype(F32)
            res = []
            for h in range(GLA_HEADS):
                vs = slice(h * GLA_DV, (h + 1) * GLA_DV)
                o = outs[h] + of_scr[cr, vs]
                gh = gout[:, vs]
                res.append(_rms(o, og_ref[...]) * (gh * jax.nn.sigmoid(gh)))
            o_ref[cr, :] = jnp.concatenate(res, axis=1).astype(o_ref.dtype)
        return carry

    lax.fori_loop(0, GLA_NBLOCK, loop2, 0, unroll=GLA_UNROLL)


def _gla(p, waf, baf, wab, bab, onorm_g):
    gate_col = (EV_W - LANE) // LANE
    qk_scr = pltpu.VMEM((TOK, GLA_QK_W), BF16)
    qk_t_scr = pltpu.VMEM((GLA_NCHUNK * GLA_QK_W, GLA_CHUNK), BF16)
    return pl.pallas_call(
        _gla_kernel,
        out_shape=jax.ShapeDtypeStruct((NTOK, GLA_V_W), BF16),
        grid=(BATCH,),
        in_specs=[pl.BlockSpec((TOK, GLA_MAIN_W), lambda b: (b, 0)),
                  pl.BlockSpec((TOK, LANE), lambda b: (b, gate_col)),
                  _resident((LANE, GLA_QK_W)), _resident((1, GLA_QK_W)),
                  _resident((LANE, GLA_QK_W)), _resident((1, GLA_QK_W)),
                  _resident((1, GLA_DV))],
        out_specs=pl.BlockSpec((TOK, GLA_V_W), lambda b: (b, 0)),
        scratch_shapes=[pltpu.VMEM((TOK, GLA_V_W), F32), qk_scr, qk_t_scr, qk_scr, qk_t_scr,
                        pltpu.VMEM((GLA_NCHUNK * GLA_QK_W, LANE), F32),
                        pltpu.VMEM((GLA_QK_W, GLA_DV), F32)],
        compiler_params=_params(("parallel",)),
        name="gla",
    )(p, p, waf, baf, wab, bab, onorm_g.reshape(1, GLA_DV))


def _gelu(x):
    return 0.5 * x * (1.0 + lax.erf(x * (2.0 ** -0.5)))


SG_TILE = 1152


def _sg_kernel(u_ref, v_ref, vg_ref, ws_ref, bs_ref, o_ref):
    for c in range(SG_TILE // SG_CHUNK):
        rows = slice(c * SG_CHUNK, (c + 1) * SG_CHUNK)
        for g in range(SG_GROUPS):
            cols = slice(g * SG_DIM, (g + 1) * SG_DIM)
            u = _gelu(u_ref[rows, cols].astype(F32))
            vn = _lane_rms(_gelu(v_ref[rows, cols].astype(F32)), vg_ref[g:g + 1, :])
            mixed = _dot(ws_ref[g], vn.astype(BF16)) + bs_ref[g]
            o_ref[rows, cols] = (u * mixed).astype(o_ref.dtype)


def _spatial_gating(p, vnorm_g, ws, bs):
    width = SG_GROUPS * SG_DIM
    u_col = GLA_MAIN_W // width
    bias = jnp.broadcast_to(bs[:, :, None], (SG_GROUPS, SG_CHUNK, SG_DIM))
    return pl.pallas_call(
        _sg_kernel,
        out_shape=jax.ShapeDtypeStruct((NTOK, width), BF16),
        grid=(NTOK // SG_TILE,),
        in_specs=[pl.BlockSpec((SG_TILE, width), lambda i: (i, u_col)),
                  pl.BlockSpec((SG_TILE, width), lambda i: (i, u_col + 1)),
                  _resident((SG_GROUPS, SG_DIM)),
                  _resident((SG_GROUPS, SG_CHUNK, SG_CHUNK)),
                  _resident((SG_GROUPS, SG_CHUNK, SG_DIM))],
        out_specs=pl.BlockSpec((SG_TILE, width), lambda i: (i, 0)),
        compiler_params=_params(("parallel",)),
        name="spatial_gating",
    )(p, p, vnorm_g, ws, bias)


FT_W = FT_GROUPS * FT_DIM


def _dft_cos_sin(n):
    jk = np.outer(np.arange(n), np.arange(n)) % n
    ang = 2.0 * np.pi * jk.astype(np.float64) / n
    return np.cos(ang), np.sin(ang)


def _parity_dft(n):
    p = np.arange(n // 2)[:, None]
    mats = []
    for first in (0, 1):
        t = 2 * np.arange(n // 2)[None, :] + first
        ang = 2.0 * np.pi * ((p * t) % n).astype(np.float64) / n
        mats.append(np.concatenate([np.cos(ang), -np.sin(ang)], axis=1))
    return jnp.asarray(np.stack(mats), F32).astype(BF16)


def _fourier_kernel(h_ref, cs_ref, dft_x_ref, dft_c_ref, o_ref, ab_scr, *, need_ctx):
    ab = _dot(h_ref[...], cs_ref[...])
    n_slab = 2 * FT_W // LANE
    for c in range(n_slab):
        ab_scr[c] = ab[:, c * LANE:(c + 1) * LANE]

    def position_dft(first_row, length, dft_ref):
        half = length // 2
        parts = []
        for parity in (0, 1):
            slabs = [ab_scr[c, pl.ds(first_row + parity, half, stride=2), :].astype(BF16) for c in range(n_slab)]
            a = jnp.concatenate(slabs[:n_slab // 2], axis=1)
            b = jnp.concatenate(slabs[n_slab // 2:], axis=1)
            parts.append(_dot(dft_ref[parity], jnp.concatenate([a, b], axis=0)))
        scale = (length * FT_DIM) ** -0.5
        o_ref[first_row:first_row + half, :] = ((parts[0] + parts[1]) * scale).astype(o_ref.dtype)
        o_ref[first_row + half:first_row + length, :] = ((parts[0] - parts[1]) * scale).astype(o_ref.dtype)

    position_dft(CTX_LEN, SEQ, dft_x_ref)
    if need_ctx:
        position_dft(0, CTX_LEN, dft_c_ref)
    else:
        o_ref[:CTX_LEN, :] = jnp.zeros((CTX_LEN, FT_W), o_ref.dtype)


def _fourier(p, need_ctx):
    cc, sc = _dft_cos_sin(FT_DIM)
    eye = np.eye(FT_GROUPS)
    cs = jnp.asarray(np.concatenate([np.kron(eye, cc), np.kron(eye, sc)], axis=1), F32).astype(BF16)
    return pl.pallas_call(
        functools.partial(_fourier_kernel, need_ctx=need_ctx),
        out_shape=jax.ShapeDtypeStruct((NTOK, FT_W), BF16),
        grid=(BATCH,),
        in_specs=[pl.BlockSpec((TOK, FT_W), lambda b: (b, 0)),
                  _resident((FT_W, 2 * FT_W)),
                  _resident((2, SEQ // 2, SEQ)), _resident((2, CTX_LEN // 2, CTX_LEN))],
        out_specs=pl.BlockSpec((TOK, FT_W), lambda b: (b, 0)),
        scratch_shapes=[pltpu.VMEM((2 * FT_W // LANE, TOK, LANE), F32)],
        compiler_params=_params(("parallel",)),
        name="fourier",
    )(p, cs, _parity_dft(SEQ), _parity_dft(CTX_LEN))


MLA_NOPE_W = MLA_HEADS * MLA_NOPE
MLA_ROPE_W = MLA_HEADS * MLA_ROPE
OD_QA0 = FT_W
OD_KVA0 = OD_QA0 + MLA_Q_RANK
OD_KPE0 = OD_KVA0 + MLA_KV_RANK
OD_KPE_ROT0 = OD_KPE0 + MLA_ROPE


def _rot_cols(w):
    q = MLA_ROPE // 4
    return jnp.concatenate([-w[..., q:2 * q], w[..., 0:q], -w[..., 3 * q:4 * q], w[..., 2 * q:3 * q]], axis=-1)


def _rot_gain(g):
    q = MLA_ROPE // 4
    return jnp.concatenate([g[..., q:2 * q], g[..., 0:q], g[..., 3 * q:4 * q], g[..., 2 * q:3 * q]], axis=-1)


def _rope_tables():
    rows = SEQ // GRID_W
    row_id = np.repeat(np.arange(rows, dtype=np.float32), GRID_W)
    col_id = np.tile(np.arange(GRID_W, dtype=np.float32), rows)
    axis_dim = MLA_ROPE // 2
    inv_freq = (np.float32(ROPE_BASE) ** (-np.arange(0, axis_dim, 2, dtype=np.float32) / np.float32(axis_dim))).astype(np.float32)
    ang_r = row_id[:, None] * inv_freq
    ang_c = col_id[:, None] * inv_freq
    ang = np.concatenate([ang_r, ang_r, ang_c, ang_c], axis=-1).astype(np.float32)
    cos = np.concatenate([np.ones((CTX_LEN, MLA_ROPE), np.float32), np.cos(ang)], axis=0)
    sin = np.concatenate([np.zeros((CTX_LEN, MLA_ROPE), np.float32), np.sin(ang)], axis=0)
    return tuple(jnp.asarray(t, F32) for t in (cos, sin, cos.T, sin.T))


MLA_VT_ROWS = MLA_V + 16


def _mla_heads(p, cos, sin, cos_t, sin_t, qag, wuq_t, kvag, wukv_k, wukv_vt, head_sum, kg,
               qg_nope, qg_rope, qg_rot, qt_ref, k_ref, vt_ref):
    qa = _lane_rms(p[:, OD_QA0:OD_KVA0].astype(F32), qag)
    q_t = _dot(wuq_t, qa.T.astype(BF16))
    for h in range(MLA_HEADS):
        qn = q_t[h * MLA_NOPE:(h + 1) * MLA_NOPE]
        qr = q_t[MLA_NOPE_W + h * MLA_ROPE:MLA_NOPE_W + (h + 1) * MLA_ROPE]
        qrr = q_t[MLA_NOPE_W + MLA_ROPE_W + h * MLA_ROPE:MLA_NOPE_W + MLA_ROPE_W + (h + 1) * MLA_ROPE]
        ss = jnp.sum(qn * qn, axis=0, keepdims=True) + jnp.sum(qr * qr, axis=0, keepdims=True)
        inv = lax.rsqrt(ss * (1.0 / MLA_QK) + RMS_EPS) * (MLA_QK ** -0.5 * LOG2_E)
        q_rope = qr * qg_rope * cos_t + qrr * qg_rot * sin_t
        qt_ref[h] = jnp.concatenate([qn * qg_nope * inv, q_rope * inv], axis=0).astype(qt_ref.dtype)

    kva = _lane_rms(p[:, OD_KVA0:OD_KPE0].astype(F32), kvag)
    v_t = _dot(wukv_vt, kva.T.astype(BF16))
    ones = jnp.ones((MLA_VT_ROWS - MLA_V, v_t.shape[1]), F32)
    for h in range(MLA_HEADS):
        vt_ref[h] = jnp.concatenate([v_t[h * MLA_V:(h + 1) * MLA_V], ones], axis=0).astype(vt_ref.dtype)
    kn = _dot(kva.astype(BF16), wukv_k)
    kpe = p[:, OD_KPE0:OD_KPE_ROT0].astype(F32)
    kpe_rot = p[:, OD_KPE_ROT0:OD_W].astype(F32)
    sq = jnp.concatenate([kn * kn, kpe * kpe], axis=1).astype(BF16)
    ss = _dot(sq, head_sum)
    inv = lax.rsqrt(ss * (1.0 / MLA_QK) + RMS_EPS)
    k_rope = kpe * kg[:, MLA_NOPE:MLA_QK] * cos + kpe_rot * kg[:, MLA_QK:] * sin
    for h in range(MLA_HEADS):
        k_h = jnp.concatenate([kn[:, h * MLA_NOPE:(h + 1) * MLA_NOPE] * kg[:, :MLA_NOPE], k_rope], axis=1)
        k_ref[h] = (k_h * inv[:, h:h + 1]).astype(k_ref.dtype)


def _key_head_sum_matrix():
    head = np.arange(LANE)[None, :]
    nope = np.repeat(np.arange(MLA_HEADS), MLA_NOPE)[:, None] == head
    rope = np.broadcast_to(head < MLA_HEADS, (MLA_ROPE, LANE))
    return jnp.asarray(np.concatenate([nope, rope], axis=0), BF16)


MLA_TILE = 768


def _mla_prep_kernel(p_ref, *refs):
    *operand_refs, qt_ref, k_ref, vt_ref = refs
    _mla_heads(p_ref[...], *(r[...] for r in operand_refs), qt_ref, k_ref, vt_ref)


def _mla_prep(p, rope_tabs, qa_g, wuq, kva_g, wukv, qn_g, kn_g):
    wukv = _wukv_cols(wukv)
    kgain = jnp.concatenate([kn_g, _rot_gain(kn_g[MLA_NOPE:])]).reshape(1, -1)
    col = lambda g: jnp.broadcast_to(g[:, None], (g.shape[0], MLA_TILE))
    operands = [*rope_tabs, qa_g.reshape(1, -1), _wuq_cols(wuq).T, kva_g.reshape(1, -1),
                wukv[:, :MLA_NOPE_W], wukv[:, MLA_NOPE_W:].T, _key_head_sum_matrix(), kgain,
                col(qn_g[:MLA_NOPE]), col(qn_g[MLA_NOPE:]), col(_rot_gain(qn_g[MLA_NOPE:]))]
    tiles_per_batch = TOK // MLA_TILE
    rope = pl.BlockSpec((MLA_TILE, MLA_ROPE), lambda i: (i % tiles_per_batch, 0))
    rope_t = pl.BlockSpec((MLA_ROPE, MLA_TILE), lambda i: (0, i % tiles_per_batch))
    rows = lambda d: pl.BlockSpec((MLA_HEADS, MLA_TILE, d), lambda i: (0, i, 0))
    cols = lambda d: pl.BlockSpec((MLA_HEADS, d, MLA_TILE), lambda i: (0, 0, i))
    return pl.pallas_call(
        _mla_prep_kernel,
        out_shape=(jax.ShapeDtypeStruct((MLA_HEADS, MLA_QK, NTOK), BF16),
                   jax.ShapeDtypeStruct((MLA_HEADS, NTOK, MLA_QK), BF16),
                   jax.ShapeDtypeStruct((MLA_HEADS, MLA_VT_ROWS, NTOK), BF16)),
        grid=(NTOK // MLA_TILE,),
        in_specs=[pl.BlockSpec((MLA_TILE, OD_W), lambda i: (i, 0)), rope, rope, rope_t, rope_t]
                 + [_resident(a.shape) for a in operands[4:]],
        out_specs=(cols(MLA_QK), rows(MLA_QK), cols(MLA_VT_ROWS)),
        compiler_params=_params(("parallel",)),
        name="mla_prep",
    )(p, *operands)


ATT_Q_TILE = 512


ATT_KEY_CHUNK = TOK


def _weighted_values(acc):
    return (acc[:MLA_V] * (1.0 / acc[MLA_V:MLA_V + 1])).T


def _attn_kernel(qt_ref, k_ref, vt_ref, o_ref, s_scr, *, need_ctx):
    n_q = SEQ // ATT_Q_TILE
    chunks = [slice(c * ATT_KEY_CHUNK, (c + 1) * ATT_KEY_CHUNK) for c in range(TOK // ATT_KEY_CHUNK)]
    q_cols = lambda j: slice(CTX_LEN + j * ATT_Q_TILE, CTX_LEN + (j + 1) * ATT_Q_TILE)

    def scores(j, rows, col_max):
        s = _dot(k_ref[0, rows, :], qt_ref[0, :, q_cols(j)])
        s_scr[j % 2, rows, :] = s
        m = jnp.max(s, axis=0, keepdims=True)
        return m if col_max is None else jnp.maximum(col_max, m)

    col_max = None
    for rows in chunks:
        col_max = scores(0, rows, col_max)
    for j in range(n_q):
        acc, next_max = None, None
        for rows in chunks:
            if j + 1 < n_q:
                next_max = scores(j + 1, rows, next_max)
            p = jnp.exp2(s_scr[j % 2, rows, :] - col_max).astype(BF16)
            part = _dot(vt_ref[0, :, rows], p)
            acc = part if acc is None else acc + part
        o_ref[q_cols(j), :] = _weighted_values(acc).astype(o_ref.dtype)
        col_max = next_max

    if need_ctx:
        s = _dot(k_ref[0, :CTX_LEN, :], qt_ref[0, :, :CTX_LEN])
        p = jnp.exp2(s - jnp.max(s, axis=0, keepdims=True)).astype(BF16)
        o_ref[:CTX_LEN, :] = _weighted_values(_dot(vt_ref[0, :, :CTX_LEN], p)).astype(o_ref.dtype)
    else:
        o_ref[:CTX_LEN, :] = jnp.zeros((CTX_LEN, MLA_V), o_ref.dtype)


def _attention(q_t, k, v_t, need_ctx):
    transposed = lambda d: pl.BlockSpec((1, d, TOK), lambda b, h: (h, 0, b))
    return pl.pallas_call(
        functools.partial(_attn_kernel, need_ctx=need_ctx),
        out_shape=jax.ShapeDtypeStruct((NTOK, MLA_HEADS * MLA_V), BF16),
        grid=(BATCH, MLA_HEADS),
        in_specs=[transposed(MLA_QK),
                  pl.BlockSpec((1, TOK, MLA_QK), lambda b, h: (h, b, 0)),
                  transposed(MLA_VT_ROWS)],
        out_specs=pl.BlockSpec((TOK, MLA_V), lambda b, h: (b, h)),
        scratch_shapes=[pltpu.VMEM((2, TOK, ATT_Q_TILE), F32)],
        compiler_params=_params(("parallel", "parallel")),
        name="attention",
    )(q_t, k, v_t)


def _even_w_in(w):
    gates0 = GLA_MAIN_W
    gates1 = gates0 + 2 * GLA_GATE_RANK
    w = w.astype(BF16)
    pad = jnp.zeros(w.shape[:-1] + (EV_W - w.shape[-1],), w.dtype)
    return jnp.concatenate([w[..., :gates0], w[..., gates1:], w[..., gates0:gates1], pad], axis=-1)


def _gate_w(wa, first_row):
    out = jnp.zeros((LANE, GLA_QK_W), wa.dtype)
    return lax.dynamic_update_slice(out, wa, (first_row, 0)).astype(BF16)


def _odd_w_in(w):
    w = w.astype(BF16)
    return jnp.concatenate([w, _rot_cols(w[..., OD_KPE0:OD_KPE_ROT0])], axis=-1)


def _wuq_cols(w):
    w = w.reshape(MLA_Q_RANK, MLA_HEADS, MLA_QK)
    nope = w[:, :, :MLA_NOPE].reshape(MLA_Q_RANK, MLA_NOPE_W)
    rope = w[:, :, MLA_NOPE:]
    return jnp.concatenate([nope, rope.reshape(MLA_Q_RANK, MLA_ROPE_W),
                            _rot_cols(rope).reshape(MLA_Q_RANK, MLA_ROPE_W)], axis=1).astype(BF16)


def _wukv_cols(w):
    w = w.reshape(MLA_KV_RANK, MLA_HEADS, MLA_NOPE + MLA_V)
    return jnp.concatenate([w[:, :, :MLA_NOPE].reshape(MLA_KV_RANK, MLA_NOPE_W),
                            w[:, :, MLA_NOPE:].reshape(MLA_KV_RANK, MLA_HEADS * MLA_V)], axis=1).astype(BF16)


def kernel(x, c, ctx, c_ctx, ada_w, ada_b, norm_mix_g, norm_ffn_g, w_mix_out, ffn_w_in, ffn_w_out, ev_w_in, gla_wa_f, gla_ba_f, gla_wa_b, gla_ba_b, gla_onorm_g, sg_vnorm_g, sg_ws, sg_bs, od_w_in, mla_qa_g, mla_wuq, mla_kva_g, mla_wukv, mla_qn_g, mla_kn_g):
    cvec = jnp.zeros((MOD_ROWS, D_MODEL), F32).at[:BATCH].set(c).at[CTX_MOD_ROW].set(c_ctx)
    mod = _ada_table(cvec, ada_w, ada_b)
    rope_tabs = _rope_tables()
    norm_mix_g = norm_mix_g.reshape(DEPTH, 1, D_MODEL)
    norm_ffn_g = norm_ffn_g.reshape(DEPTH, 1, D_MODEL)
    w_in_stacks = (_even_w_in(ev_w_in), _odd_w_in(od_w_in))
    dense_f32 = (w_mix_out, ffn_w_in, ffn_w_out)

    p, xa, dense = _first_inproj(x, ctx, mod, norm_mix_g, w_in_stacks[0], dense_f32)
    for l in range(DEPTH):
        last = l == DEPTH - 1
        i = l // 2
        if l % 2 == 0:
            ma = _gla(p, _gate_w(gla_wa_f[i], 0), gla_ba_f[i].reshape(1, GLA_QK_W),
                      _gate_w(gla_wa_b[i], GLA_GATE_RANK), gla_ba_b[i].reshape(1, GLA_QK_W), gla_onorm_g[i])
            mb = _spatial_gating(p, sg_vnorm_g[i], sg_ws[i].astype(BF16), sg_bs[i])
        else:
            q_t, k, v_t = _mla_prep(p, rope_tabs, mla_qa_g[i], mla_wuq[i], mla_kva_g[i], mla_wukv[i],
                                    mla_qn_g[i], mla_kn_g[i])
            ma = _fourier(p, not last)
            mb = _attention(q_t, k, v_t, not last)
        nxt = None if last else (norm_mix_g, w_in_stacks[(l + 1) % 2], (l + 1) // 2, dense_f32)
        xa, p, dense = _layer_tail(xa, ma, mb, mod, l, norm_ffn_g, dense, nxt)
    return xa.reshape(BATCH, SEQ, D_MODEL)
```

```python
import functools

import numpy as np
import jax
import jax.numpy as jnp
from jax import lax
from jax.experimental import pallas as pl
from jax.experimental.pallas import tpu as pltpu

D_MODEL = 1024
BATCH = 8
SEQ = 2048
DEPTH = 4
CTX_LEN = 256
GRID_W = 64
RMS_EPS = 1e-6
GLA_HEADS = 4
GLA_DK = 64
GLA_DV = 128
GLA_GATE_RANK = 16
GLA_GATE_NORM = 16.0
GLA_CHUNK = 64
SG_GROUPS = 4
SG_DIM = 128
SG_CHUNK = 128
FT_GROUPS = 4
FT_DIM = 64
MLA_HEADS = 6
MLA_NOPE = 128
MLA_ROPE = 64
MLA_V = 128
MLA_Q_RANK = 384
MLA_KV_RANK = 256
ROPE_BASE = 10000.0
FFN_HIDDEN = 2816

TOK = CTX_LEN + SEQ
NTOK = BATCH * TOK
ROW_TILE = 256
TILES_PER_BATCH = TOK // ROW_TILE
CTX_TILES = CTX_LEN // ROW_TILE
MOD_ROWS = 16
CTX_MOD_ROW = BATCH
LANE = 128
EV_W = 2688
OD_W = 1024
MLA_QK = MLA_NOPE + MLA_ROPE
VMEM_LIMIT = 56 * 1024 * 1024

F32 = jnp.float32
BF16 = jnp.bfloat16


def _dot(a, b):
    return jnp.dot(a, b, preferred_element_type=F32)


def _rms(x, g):
    return x * lax.rsqrt(jnp.mean(x * x, axis=-1, keepdims=True) + RMS_EPS) * g


def _lane_rms(x, g):
    width = x.shape[1]
    ss = _dot((x * x).astype(BF16), jnp.ones((width, LANE), BF16))
    inv = lax.rsqrt(ss * (1.0 / width) + RMS_EPS)
    return x * jnp.concatenate([inv] * (width // LANE), axis=1) * g


def _params(sem):
    return pltpu.CompilerParams(dimension_semantics=sem, vmem_limit_bytes=VMEM_LIMIT)


def _resident(shape):
    zeros = (0,) * len(shape)
    return pl.BlockSpec(shape, lambda *_: zeros, pipeline_mode=pl.Buffered(1))


def _mod_row(i):
    return jnp.where(i % TILES_PER_BATCH < CTX_TILES, CTX_MOD_ROW, i // TILES_PER_BATCH)


MOD_PIECES = 6


def _mod_spec(layer, tile_of_step):
    def index_map(i):
        return (layer * MOD_ROWS + _mod_row(tile_of_step(i)), 0, 0)
    return pl.BlockSpec((1, MOD_PIECES, D_MODEL), index_map)


def _ada_kernel(c_ref, w_ref, b_ref, o_ref):
    c = c_ref[...]
    s = c * jax.nn.sigmoid(c)
    w = w_ref[0]
    w_hi = w.astype(BF16)
    w_lo = (w - w_hi.astype(F32)).astype(BF16)
    s_hi = s.astype(BF16)
    s_lo = (s - s_hi.astype(F32)).astype(BF16)
    o_ref[0] = _dot(s_hi, w_hi) + (_dot(s_hi, w_lo) + _dot(s_lo, w_hi)) + b_ref[0]


def _ada_table(cvec, ada_w, ada_b):
    tn = 3072
    n = MOD_PIECES * D_MODEL
    out = pl.pallas_call(
        _ada_kernel,
        out_shape=jax.ShapeDtypeStruct((DEPTH, MOD_ROWS, n), F32),
        grid=(DEPTH, n // tn),
        in_specs=[pl.BlockSpec((MOD_ROWS, D_MODEL), lambda l, j: (0, 0)),
                  pl.BlockSpec((1, D_MODEL, tn), lambda l, j: (l, 0, j)),
                  pl.BlockSpec((1, 1, tn), lambda l, j: (l, 0, j))],
        out_specs=pl.BlockSpec((1, MOD_ROWS, tn), lambda l, j: (l, 0, j)),
        compiler_params=_params(("arbitrary", "arbitrary")),
        name="ada_table",
    )(cvec, ada_w, ada_b.reshape(DEPTH, 1, n))
    return out.reshape(DEPTH * MOD_ROWS, MOD_PIECES, D_MODEL)


def _layer_block(shape, layer):
    zeros = (0,) * len(shape)
    return pl.BlockSpec((1,) + tuple(shape), lambda *_: (layer,) + zeros, pipeline_mode=pl.Buffered(1))


BF16_SUBLANES = 16


def _copy_plumbing(stacks, layer, n_steps):
    views, in_specs, shapes, out_specs = [], [], [], []
    for w in stacks:
        n_layers, n_rows, cols = w.shape
        rows = next(r for r in range(BF16_SUBLANES, n_rows + 1, BF16_SUBLANES)
                    if n_rows % r == 0 and n_rows // r <= n_steps)
        n_slices = n_rows // rows
        views.append(w.reshape(n_layers * n_rows, cols))
        in_specs.append(pl.BlockSpec(
            (rows, cols), lambda i, n=n_slices: (layer * n + jnp.minimum(i, n - 1), 0)))
        shapes.append(jax.ShapeDtypeStruct((n_rows, cols), BF16))
        out_specs.append(pl.BlockSpec((rows, cols), lambda i, n=n_slices: (jnp.minimum(i, n - 1), 0)))
    return views, in_specs, shapes, out_specs


def _as_layer_stacks(copies, stacks):
    return [c.reshape(1, w.shape[1], w.shape[2]) for c, w in zip(copies, stacks)]


def _modulated_norm(x, g, shift, scale):
    return _rms(x, g) * (1.0 + scale) + shift


TAIL_TILES = 2


def _first_inproj_kernel(*refs):
    n = TAIL_TILES
    x_refs, ctx_refs, mod_refs = refs[:n], refs[n:2 * n], refs[2 * n:3 * n]
    g_ref, w_ref, *copy_refs, p_ref, xa_ref, mix_copy, in_copy, out_copy = refs[3 * n:]
    rows = []
    for t in range(n):
        is_ctx = (n * pl.program_id(0) + t) % TILES_PER_BATCH < CTX_TILES
        x = jnp.where(is_ctx, ctx_refs[t][...], x_refs[t][...])
        xa_ref[t * ROW_TILE:(t + 1) * ROW_TILE, :] = x
        mod = mod_refs[t][0]
        rows.append(_modulated_norm(x, g_ref[0], mod[0:1], mod[1:2]).astype(BF16))
    p_ref[...] = _dot(jnp.concatenate(rows, axis=0), w_ref[0]).astype(p_ref.dtype)
    for src, dst in zip(copy_refs, (mix_copy, in_copy, out_copy)):
        dst[...] = src[...].astype(BF16)


def _first_inproj(x, ctx, mod, norm_g, w, dense_f32):
    width = w.shape[-1]
    n_steps = NTOK // (TAIL_TILES * ROW_TILE)
    views, copy_in, copy_shapes, copy_out = _copy_plumbing(dense_f32, 0, n_steps)
    x_tiles = SEQ // ROW_TILE
    tiles = [lambda i, t=t: TAIL_TILES * i + t for t in range(TAIL_TILES)]
    x_tile = lambda j: ((j // TILES_PER_BATCH) * x_tiles + jnp.maximum(j % TILES_PER_BATCH - CTX_TILES, 0), 0)
    ctx_tile = lambda j: ((j // TILES_PER_BATCH) * CTX_TILES + jnp.minimum(j % TILES_PER_BATCH, CTX_TILES - 1), 0)
    source = lambda tile_index: [pl.BlockSpec((ROW_TILE, D_MODEL), lambda i, f=f: tile_index(f(i))) for f in tiles]
    row_out = lambda wd: pl.BlockSpec((TAIL_TILES * ROW_TILE, wd), lambda i: (i, 0))
    per_tile = lambda a: [a] * TAIL_TILES
    p, xa, *copies = pl.pallas_call(
        _first_inproj_kernel,
        out_shape=(jax.ShapeDtypeStruct((NTOK, width), BF16), jax.ShapeDtypeStruct((NTOK, D_MODEL), F32),
                   *copy_shapes),
        grid=(n_steps,),
        in_specs=[*source(x_tile), *source(ctx_tile), *[_mod_spec(0, f) for f in tiles],
                  _layer_block((1, D_MODEL), 0),
                  _layer_block((D_MODEL, width), 0),
                  *copy_in],
        out_specs=(row_out(width), row_out(D_MODEL), *copy_out),
        compiler_params=_params(("arbitrary",)),
        name="first_inproj",
    )(*per_tile(x.reshape(BATCH * SEQ, D_MODEL)), *per_tile(ctx.reshape(BATCH * CTX_LEN, D_MODEL)),
      *per_tile(mod), norm_g, w, *views)
    return p, xa, _as_layer_stacks(copies, dense_f32)


def _tail_kernel(*refs, has_next):
    n = TAIL_TILES
    x_refs, ma_refs, mb_refs, mod_refs = refs[:n], refs[n:2 * n], refs[2 * n:3 * n], refs[3 * n:4 * n]
    g_ref, wmix_ref, win_ref, wout_ref, *rest = refs[4 * n:]
    mods = [r[0] for r in mod_refs]
    stack = lambda tiles: jnp.concatenate(tiles, axis=0)
    unstack = lambda rows: [rows[t * ROW_TILE:(t + 1) * ROW_TILE] for t in range(n)]

    wide = lambda r: r[...] if len(r.shape) == 2 else jnp.concatenate([r[h] for h in range(r.shape[0])], axis=1)
    m = stack([jnp.concatenate([wide(a), wide(b)], axis=1) for a, b in zip(ma_refs, mb_refs)])
    x1 = [x[...] + mod[2:3] * y for x, mod, y in zip(x_refs, mods, unstack(_dot(m, wmix_ref[0])))]
    h = stack([_modulated_norm(x, g_ref[0], mod[3:4], mod[4:5]).astype(BF16) for x, mod in zip(x1, mods)])
    gu = _dot(h, win_ref[0])
    gate = gu[:, :FFN_HIDDEN]
    act = (gate * jax.nn.sigmoid(gate) * gu[:, FFN_HIDDEN:]).astype(BF16)
    x2 = [x + mod[5:6] * d for x, mod, d in zip(x1, mods, unstack(_dot(act, wout_ref[0])))]
    if has_next:
        next_mods = [r[0] for r in rest[:n]]
        gn_ref, wn_ref, *copy_refs, o_ref, p_ref, mix_copy, in_copy, out_copy = rest[n:]
        z = stack([_modulated_norm(x, gn_ref[0], mod[0:1], mod[1:2]).astype(BF16) for x, mod in zip(x2, next_mods)])
        p_ref[...] = _dot(z, wn_ref[0]).astype(p_ref.dtype)
        for src, dst in zip(copy_refs, (mix_copy, in_copy, out_copy)):
            dst[...] = src[...].astype(BF16)
    else:
        (o_ref,) = rest
    o_ref[...] = stack(x2)


def _layer_tail(xa, ma, mb, mod, layer, norm_ffn_g, dense, nxt):
    wmix, win, wout = dense
    if nxt is None:
        x_tiles = SEQ // ROW_TILE
        n_tiles = BATCH * x_tiles
        tile = lambda i: (i // x_tiles) * TILES_PER_BATCH + CTX_TILES + i % x_tiles
        out_rows = BATCH * SEQ
    else:
        n_tiles = NTOK // ROW_TILE
        tile = lambda i: i
        out_rows = NTOK
    n_steps = n_tiles // TAIL_TILES
    tiles = [lambda i, t=t: tile(TAIL_TILES * i + t) for t in range(TAIL_TILES)]
    row_in = lambda width: [pl.BlockSpec((ROW_TILE, width), lambda i, f=f: (f(i), 0)) for f in tiles]
    row_out = lambda width: pl.BlockSpec((TAIL_TILES * ROW_TILE, width), lambda i: (i, 0))
    def mixer_in(a):
        if a.ndim == 2:
            return row_in(a.shape[1])
        return [pl.BlockSpec((a.shape[0], ROW_TILE, a.shape[2]), lambda i, f=f: (0, f(i), 0)) for f in tiles]

    in_specs = [*row_in(D_MODEL), *mixer_in(ma), *mixer_in(mb),
                *[_mod_spec(layer, f) for f in tiles],
                _layer_block((1, D_MODEL), layer),
                _layer_block((D_MODEL, D_MODEL), 0),
                _layer_block((D_MODEL, 2 * FFN_HIDDEN), 0),
                _layer_block((FFN_HIDDEN, D_MODEL), 0)]
    per_tile = lambda a: [a] * TAIL_TILES
    args = [*per_tile(xa), *per_tile(ma), *per_tile(mb), *per_tile(mod), norm_ffn_g, wmix, win, wout]
    out_shape = [jax.ShapeDtypeStruct((out_rows, D_MODEL), F32)]
    out_specs = [row_out(D_MODEL)]
    if nxt is not None:
        norm_mix_g, w_next, idx, dense_f32 = nxt
        width = w_next.shape[-1]
        views, copy_in, copy_shapes, copy_out = _copy_plumbing(dense_f32, layer + 1, n_steps)
        in_specs += [*[_mod_spec(layer + 1, f) for f in tiles],
                     _layer_block((1, D_MODEL), layer + 1), _layer_block((D_MODEL, width), idx)] + copy_in
        args += [*per_tile(mod), norm_mix_g, w_next] + views
        out_shape += [jax.ShapeDtypeStruct((out_rows, width), BF16)] + copy_shapes
        out_specs += [row_out(width)] + copy_out
    res = pl.pallas_call(
        functools.partial(_tail_kernel, has_next=nxt is not None),
        out_shape=tuple(out_shape),
        grid=(n_steps,),
        in_specs=in_specs,
        out_specs=tuple(out_specs),
        compiler_params=_params(("arbitrary",)),
        name="layer_tail",
    )(*args)
    if nxt is None:
        return res[0], None, None
    return res[0], res[1], _as_layer_stacks(res[2:], dense_f32)


GLA_QK_W = GLA_HEADS * GLA_DK
GLA_V_W = GLA_HEADS * GLA_DV
GLA_MAIN_W = 2 * GLA_QK_W + 2 * GLA_V_W
GLA_NCHUNK = TOK // GLA_CHUNK


GLA_BLOCK = 256
GLA_NBLOCK = TOK // GLA_BLOCK
GLA_CTX_BLOCKS = CTX_LEN // GLA_BLOCK
GLA_UNROLL = GLA_NBLOCK


LOG2_E = 1.4426950408889634


def _log2_sigmoid(y, scale):
    return jnp.minimum(y, 0.0) * (LOG2_E * scale) - jnp.log2(1.0 + jnp.exp2(jnp.abs(y) * (-LOG2_E))) * scale


def _gla_kernel(main_ref, gate_ref, waf_ref, baf_ref, wab_ref, bab_ref, og_ref, o_ref,
                of_scr, qd_scr, kd_scr, qe_scr, kl_scr, dec_scr, s_scr):
    C, BLK = GLA_CHUNK, GLA_BLOCK
    cpb = BLK // C
    rb = lax.broadcasted_iota(jnp.int32, (BLK, BLK), 0)
    cb = lax.broadcasted_iota(jnp.int32, (BLK, BLK), 1)
    same_chunk = (rb // C) == (cb // C)
    tri_f = jnp.where(same_chunk & (rb >= cb), 1.0, 0.0).astype(BF16)
    tri_b = jnp.where(same_chunk & (rb <= cb), 1.0, 0.0).astype(BF16)
    r = lax.broadcasted_iota(jnp.int32, (C, C), 0)
    cc = lax.broadcasted_iota(jnp.int32, (C, C), 1)
    lower, upper = r >= cc, r <= cc
    sr = lax.broadcasted_iota(jnp.int32, (BLK, cpb * LANE), 0)
    sc = lax.broadcasted_iota(jnp.int32, (BLK, cpb * LANE), 1)
    chunk_sum = jnp.where(sr // C == sc // LANE, 1.0, 0.0).astype(BF16)

    def split(x):
        hi = x.astype(BF16)
        return hi, (x - hi.astype(F32)).astype(BF16)

    def dot_01(m, x):
        hi, lo = split(x)
        return _dot(jnp.concatenate([m, m], axis=1), jnp.concatenate([hi, lo], axis=0))

    def dot_10(x, m):
        hi, lo = split(x)
        return _dot(jnp.concatenate([hi, lo], axis=1), jnp.concatenate([m, m], axis=0))

    def block_rows(blk):
        return pl.ds(pl.multiple_of(blk * BLK, BLK), BLK)

    def chunk_rows(blk, j):
        return pl.ds(pl.multiple_of(blk * BLK + j * C, C), C)

    def dec_rows(blk, j):
        return pl.ds(pl.multiple_of((blk * cpb + j) * GLA_QK_W, GLA_QK_W), GLA_QK_W)

    def decay_terms(a, q, k, w_ref, bias_ref, tri, end_row):
        g = _log2_sigmoid(_dot(a, w_ref[...]) + bias_ref[...], 1.0 / GLA_GATE_NORM)
        b = dot_01(tri, g)
        dec_all = jnp.exp2(dot_10(g.T, chunk_sum))
        terms = []
        for j in range(cpb):
            sl = slice(j * C, (j + 1) * C)
            bc = b[sl]
            b_mid = bc[C // 2:C // 2 + 1]
            b_end = bc[end_row:end_row + 1]
            d = bc - b_mid
            qd = q[sl] * jnp.exp2(d)
            kd = k[sl] * jnp.exp2(-d)
            qe = qd * jnp.exp2(b_mid)
            kl = kd * jnp.exp2(b_end - b_mid)
            dec = dec_all[:, j * LANE:(j + 1) * LANE]
            terms.append((qd.astype(BF16), kd.T.astype(BF16), qe.astype(BF16), kl.T.astype(BF16), dec))
        return terms

    def step(qd, kd_t, qe, kl_t, dec, v, keep):
        outs = []
        for h in range(GLA_HEADS):
            ks = slice(h * GLA_DK, (h + 1) * GLA_DK)
            vh = v[:, h * GLA_DV:(h + 1) * GLA_DV]
            s_h = s_scr[ks, :]
            att = jnp.where(keep, _dot(qd[:, ks], kd_t[ks, :]), 0.0).astype(BF16)
            outs.append(_dot(att, vh) + _dot(qe[:, ks], s_h.astype(BF16)))
            s_scr[ks, :] = s_h * dec[ks, :] + _dot(kl_t[ks, :], vh)
        return outs

    s_scr[...] = jnp.zeros(s_scr.shape, F32)

    def loop1(blk, carry):
        rows = block_rows(blk)
        a = gate_ref[rows, :]
        q = main_ref[rows, 0:GLA_QK_W].astype(F32) * (GLA_DK ** -0.5)
        k = main_ref[rows, GLA_QK_W:2 * GLA_QK_W].astype(F32)
        v = main_ref[rows, 2 * GLA_QK_W:2 * GLA_QK_W + GLA_V_W]
        fwd = decay_terms(a, q, k, waf_ref, baf_ref, tri_f, C - 1)
        bwd = decay_terms(a, q, k, wab_ref, bab_ref, tri_b, 0)
        for j in range(cpb):
            cr = chunk_rows(blk, j)
            qd, kd, qe, kl, dec = bwd[j]
            qd_scr[cr, :] = qd
            kd_scr[dec_rows(blk, j), :] = kd
            qe_scr[cr, :] = qe
            kl_scr[dec_rows(blk, j), :] = kl
            dec_scr[dec_rows(blk, j), :] = dec
            qd, kd, qe, kl, dec = fwd[j]
            outs = step(qd, kd, qe, kl, dec, v[j * C:(j + 1) * C], lower)
            of_scr[cr, :] = jnp.concatenate(outs, axis=1)
        return carry

    lax.fori_loop(0, GLA_NBLOCK, loop1, 0, unroll=GLA_UNROLL)

    s_scr[...] = jnp.zeros(s_scr.shape, F32)

    def loop2(i, carry):
        blk = jnp.where(i < GLA_CTX_BLOCKS, GLA_CTX_BLOCKS - 1 - i, GLA_NBLOCK - 1 - (i - GLA_CTX_BLOCKS))
        for j in reversed(range(cpb)):
            cr = chunk_rows(blk, j)
            v = main_ref[cr, 2 * GLA_QK_W:2 * GLA_QK_W + GLA_V_W]
            kr = dec_rows(blk, j)
            outs = step(qd_scr[cr, :], kd_scr[kr, :], qe_scr[cr, :], kl_scr[kr, :], dec_scr[kr, :], v, upper)
            gout = main_ref[cr, 2 * GLA_QK_W + GLA_V_W:GLA_MAIN_W].astype(F32)
            res = []
            for h in range(GLA_HEADS):
                vs = slice(h * GLA_DV, (h + 1) * GLA_DV)
                o = outs[h] + of_scr[cr, vs]
                gh = gout[:, vs]
                res.append(_rms(o, og_ref[...]) * (gh * jax.nn.sigmoid(gh)))
            o_ref[cr, :] = jnp.concatenate(res, axis=1).astype(o_ref.dtype)
        return carry

    lax.fori_loop(0, GLA_NBLOCK, loop2, 0, unroll=GLA_UNROLL)


def _gla(p, waf, baf, wab, bab, onorm_g):
    gate_col = (EV_W - LANE) // LANE
    qk_scr = pltpu.VMEM((TOK, GLA_QK_W), BF16)
    qk_t_scr = pltpu.VMEM((GLA_NCHUNK * GLA_QK_W, GLA_CHUNK), BF16)
    return pl.pallas_call(
        _gla_kernel,
        out_shape=jax.ShapeDtypeStruct((NTOK, GLA_V_W), BF16),
        grid=(BATCH,),
        in_specs=[pl.BlockSpec((TOK, GLA_MAIN_W), lambda b: (b, 0)),
                  pl.BlockSpec((TOK, LANE), lambda b: (b, gate_col)),
                  _resident((LANE, GLA_QK_W)), _resident((1, GLA_QK_W)),
                  _resident((LANE, GLA_QK_W)), _resident((1, GLA_QK_W)),
                  _resident((1, GLA_DV))],
        out_specs=pl.BlockSpec((TOK, GLA_V_W), lambda b: (b, 0)),
        scratch_shapes=[pltpu.VMEM((TOK, GLA_V_W), F32), qk_scr, qk_t_scr, qk_scr, qk_t_scr,
                        pltpu.VMEM((GLA_NCHUNK * GLA_QK_W, LANE), F32),
                        pltpu.VMEM((GLA_QK_W, GLA_DV), F32)],
        compiler_params=_params(("parallel",)),
        name="gla",
    )(p, p, waf, baf, wab, bab, onorm_g.reshape(1, GLA_DV))


def _gelu(x):
    return 0.5 * x * (1.0 + lax.erf(x * (2.0 ** -0.5)))


SG_TILE = 1152


def _sg_kernel(u_ref, v_ref, vg_ref, ws_ref, bs_ref, o_ref):
    for c in range(SG_TILE // SG_CHUNK):
        rows = slice(c * SG_CHUNK, (c + 1) * SG_CHUNK)
        for g in range(SG_GROUPS):
            cols = slice(g * SG_DIM, (g + 1) * SG_DIM)
            u = _gelu(u_ref[rows, cols].astype(F32))
            vn = _lane_rms(_gelu(v_ref[rows, cols].astype(F32)), vg_ref[g:g + 1, :])
            mixed = _dot(ws_ref[g], vn.astype(BF16)) + bs_ref[g]
            o_ref[rows, cols] = (u * mixed).astype(o_ref.dtype)


def _spatial_gating(p, vnorm_g, ws, bs):
    width = SG_GROUPS * SG_DIM
    u_col = GLA_MAIN_W // width
    bias = jnp.broadcast_to(bs[:, :, None], (SG_GROUPS, SG_CHUNK, SG_DIM))
    return pl.pallas_call(
        _sg_kernel,
        out_shape=jax.ShapeDtypeStruct((NTOK, width), BF16),
        grid=(NTOK // SG_TILE,),
        in_specs=[pl.BlockSpec((SG_TILE, width), lambda i: (i, u_col)),
                  pl.BlockSpec((SG_TILE, width), lambda i: (i, u_col + 1)),
                  _resident((SG_GROUPS, SG_DIM)),
                  _resident((SG_GROUPS, SG_CHUNK, SG_CHUNK)),
                  _resident((SG_GROUPS, SG_CHUNK, SG_DIM))],
        out_specs=pl.BlockSpec((SG_TILE, width), lambda i: (i, 0)),
        compiler_params=_params(("parallel",)),
        name="spatial_gating",
    )(p, p, vnorm_g, ws, bias)


FT_W = FT_GROUPS * FT_DIM


def _dft_cos_sin(n):
    jk = np.outer(np.arange(n), np.arange(n)) % n
    ang = 2.0 * np.pi * jk.astype(np.float64) / n
    return np.cos(ang), np.sin(ang)


def _parity_dft(n):
    p = np.arange(n // 2)[:, None]
    mats = []
    for first in (0, 1):
        t = 2 * np.arange(n // 2)[None, :] + first
        ang = 2.0 * np.pi * ((p * t) % n).astype(np.float64) / n
        mats.append(np.concatenate([np.cos(ang), -np.sin(ang)], axis=1))
    return jnp.asarray(np.stack(mats), F32).astype(BF16)


def _fourier_kernel(h_ref, cs_ref, dft_x_ref, dft_c_ref, o_ref, ab_scr, *, need_ctx):
    ab = _dot(h_ref[...], cs_ref[...])
    n_slab = 2 * FT_W // LANE
    for c in range(n_slab):
        ab_scr[c] = ab[:, c * LANE:(c + 1) * LANE]

    def position_dft(first_row, length, dft_ref):
        half = length // 2
        parts = []
        for parity in (0, 1):
            slabs = [ab_scr[c, pl.ds(first_row + parity, half, stride=2), :].astype(BF16) for c in range(n_slab)]
            a = jnp.concatenate(slabs[:n_slab // 2], axis=1)
            b = jnp.concatenate(slabs[n_slab // 2:], axis=1)
            parts.append(_dot(dft_ref[parity], jnp.concatenate([a, b], axis=0)))
        scale = (length * FT_DIM) ** -0.5
        o_ref[first_row:first_row + half, :] = ((parts[0] + parts[1]) * scale).astype(o_ref.dtype)
        o_ref[first_row + half:first_row + length, :] = ((parts[0] - parts[1]) * scale).astype(o_ref.dtype)

    position_dft(CTX_LEN, SEQ, dft_x_ref)
    if need_ctx:
        position_dft(0, CTX_LEN, dft_c_ref)
    else:
        o_ref[:CTX_LEN, :] = jnp.zeros((CTX_LEN, FT_W), o_ref.dtype)


def _fourier(p, need_ctx):
    cc, sc = _dft_cos_sin(FT_DIM)
    eye = np.eye(FT_GROUPS)
    cs = jnp.asarray(np.concatenate([np.kron(eye, cc), np.kron(eye, sc)], axis=1), F32).astype(BF16)
    return pl.pallas_call(
        functools.partial(_fourier_kernel, need_ctx=need_ctx),
        out_shape=jax.ShapeDtypeStruct((NTOK, FT_W), BF16),
        grid=(BATCH,),
        in_specs=[pl.BlockSpec((TOK, FT_W), lambda b: (b, 0)),
                  _resident((FT_W, 2 * FT_W)),
                  _resident((2, SEQ // 2, SEQ)), _resident((2, CTX_LEN // 2, CTX_LEN))],
        out_specs=pl.BlockSpec((TOK, FT_W), lambda b: (b, 0)),
        scratch_shapes=[pltpu.VMEM((2 * FT_W // LANE, TOK, LANE), F32)],
        compiler_params=_params(("parallel",)),
        name="fourier",
    )(p, cs, _parity_dft(SEQ), _parity_dft(CTX_LEN))


MLA_NOPE_W = MLA_HEADS * MLA_NOPE
MLA_ROPE_W = MLA_HEADS * MLA_ROPE
OD_QA0 = FT_W
OD_KVA0 = OD_QA0 + MLA_Q_RANK
OD_KPE0 = OD_KVA0 + MLA_KV_RANK
OD_KPE_ROT0 = OD_KPE0 + MLA_ROPE


def _rot_cols(w):
    q = MLA_ROPE // 4
    return jnp.concatenate([-w[..., q:2 * q], w[..., 0:q], -w[..., 3 * q:4 * q], w[..., 2 * q:3 * q]], axis=-1)


def _rot_gain(g):
    q = MLA_ROPE // 4
    return jnp.concatenate([g[..., q:2 * q], g[..., 0:q], g[..., 3 * q:4 * q], g[..., 2 * q:3 * q]], axis=-1)


def _rope_tables():
    rows = SEQ // GRID_W
    row_id = np.repeat(np.arange(rows, dtype=np.float32), GRID_W)
    col_id = np.tile(np.arange(GRID_W, dtype=np.float32), rows)
    axis_dim = MLA_ROPE // 2
    inv_freq = (np.float32(ROPE_BASE) ** (-np.arange(0, axis_dim, 2, dtype=np.float32) / np.float32(axis_dim))).astype(np.float32)
    ang_r = row_id[:, None] * inv_freq
    ang_c = col_id[:, None] * inv_freq
    ang = np.concatenate([ang_r, ang_r, ang_c, ang_c], axis=-1).astype(np.float32)
    cos = np.concatenate([np.ones((CTX_LEN, MLA_ROPE), np.float32), np.cos(ang)], axis=0)
    sin = np.concatenate([np.zeros((CTX_LEN, MLA_ROPE), np.float32), np.sin(ang)], axis=0)
    return tuple(jnp.asarray(t, F32) for t in (cos, sin, cos.T, sin.T))


MLA_VT_ROWS = MLA_V + 16


def _mla_heads(p, cos, sin, cos_t, sin_t, qag, wuq_t, kvag, wukv_k, wukv_vt, head_sum, kg,
               qg_nope, qg_rope, qg_rot, qt_ref, k_ref, vt_ref):
    qa = _lane_rms(p[:, OD_QA0:OD_KVA0].astype(F32), qag)
    q_t = _dot(wuq_t, qa.T.astype(BF16))
    for h in range(MLA_HEADS):
        qn = q_t[h * MLA_NOPE:(h + 1) * MLA_NOPE]
        qr = q_t[MLA_NOPE_W + h * MLA_ROPE:MLA_NOPE_W + (h + 1) * MLA_ROPE]
        qrr = q_t[MLA_NOPE_W + MLA_ROPE_W + h * MLA_ROPE:MLA_NOPE_W + MLA_ROPE_W + (h + 1) * MLA_ROPE]
        ss = jnp.sum(qn * qn, axis=0, keepdims=True) + jnp.sum(qr * qr, axis=0, keepdims=True)
        inv = lax.rsqrt(ss * (1.0 / MLA_QK) + RMS_EPS) * (MLA_QK ** -0.5 * LOG2_E)
        q_rope = qr * qg_rope * cos_t + qrr * qg_rot * sin_t
        qt_ref[h] = jnp.concatenate([qn * qg_nope * inv, q_rope * inv], axis=0).astype(qt_ref.dtype)

    kva = _lane_rms(p[:, OD_KVA0:OD_KPE0].astype(F32), kvag)
    v_t = _dot(wukv_vt, kva.T.astype(BF16))
    ones = jnp.ones((MLA_VT_ROWS - MLA_V, v_t.shape[1]), F32)
    for h in range(MLA_HEADS):
        vt_ref[h] = jnp.concatenate([v_t[h * MLA_V:(h + 1) * MLA_V], ones], axis=0).astype(vt_ref.dtype)
    kn = _dot(kva.astype(BF16), wukv_k)
    kpe = p[:, OD_KPE0:OD_KPE_ROT0].astype(F32)
    kpe_rot = p[:, OD_KPE_ROT0:OD_W].astype(F32)
    sq = jnp.concatenate([kn * kn, kpe * kpe], axis=1).astype(BF16)
    ss = _dot(sq, head_sum)
    inv = lax.rsqrt(ss * (1.0 / MLA_QK) + RMS_EPS)
    k_rope = kpe * kg[:, MLA_NOPE:MLA_QK] * cos + kpe_rot * kg[:, MLA_QK:] * sin
    for h in range(MLA_HEADS):
        k_h = jnp.concatenate([kn[:, h * MLA_NOPE:(h + 1) * MLA_NOPE] * kg[:, :MLA_NOPE], k_rope], axis=1)
        k_ref[h] = (k_h * inv[:, h:h + 1]).astype(k_ref.dtype)


def _key_head_sum_matrix():
    head = np.arange(LANE)[None, :]
    nope = np.repeat(np.arange(MLA_HEADS), MLA_NOPE)[:, None] == head
    rope = np.broadcast_to(head < MLA_HEADS, (MLA_ROPE, LANE))
    return jnp.asarray(np.concatenate([nope, rope], axis=0), BF16)


MLA_TILE = 768


def _mla_prep_kernel(p_ref, *refs):
    *operand_refs, qt_ref, k_ref, vt_ref = refs
    _mla_heads(p_ref[...], *(r[...] for r in operand_refs), qt_ref, k_ref, vt_ref)


def _mla_prep(p, rope_tabs, qa_g, wuq, kva_g, wukv, qn_g, kn_g):
    wukv = _wukv_cols(wukv)
    kgain = jnp.concatenate([kn_g, _rot_gain(kn_g[MLA_NOPE:])]).reshape(1, -1)
    col = lambda g: jnp.broadcast_to(g[:, None], (g.shape[0], MLA_TILE))
    operands = [*rope_tabs, qa_g.reshape(1, -1), _wuq_cols(wuq).T, kva_g.reshape(1, -1),
                wukv[:, :MLA_NOPE_W], wukv[:, MLA_NOPE_W:].T, _key_head_sum_matrix(), kgain,
                col(qn_g[:MLA_NOPE]), col(qn_g[MLA_NOPE:]), col(_rot_gain(qn_g[MLA_NOPE:]))]
    tiles_per_batch = TOK // MLA_TILE
    rope = pl.BlockSpec((MLA_TILE, MLA_ROPE), lambda i: (i % tiles_per_batch, 0))
    rope_t = pl.BlockSpec((MLA_ROPE, MLA_TILE), lambda i: (0, i % tiles_per_batch))
    rows = lambda d: pl.BlockSpec((MLA_HEADS, MLA_TILE, d), lambda i: (0, i, 0))
    cols = lambda d: pl.BlockSpec((MLA_HEADS, d, MLA_TILE), lambda i: (0, 0, i))
    return pl.pallas_call(
        _mla_prep_kernel,
        out_shape=(jax.ShapeDtypeStruct((MLA_HEADS, MLA_QK, NTOK), BF16),
                   jax.ShapeDtypeStruct((MLA_HEADS, NTOK, MLA_QK), BF16),
                   jax.ShapeDtypeStruct((MLA_HEADS, MLA_VT_ROWS, NTOK), BF16)),
        grid=(NTOK // MLA_TILE,),
        in_specs=[pl.BlockSpec((MLA_TILE, OD_W), lambda i: (i, 0)), rope, rope, rope_t, rope_t]
                 + [_resident(a.shape) for a in operands[4:]],
        out_specs=(cols(MLA_QK), rows(MLA_QK), cols(MLA_VT_ROWS)),
        compiler_params=_params(("parallel",)),
        name="mla_prep",
    )(p, *operands)


ATT_Q_TILE = 512


ATT_KEY_CHUNK = TOK


def _weighted_values(acc):
    return (acc[:MLA_V] * (1.0 / acc[MLA_V:MLA_V + 1])).T


def _attn_kernel(qt_ref, k_ref, vt_ref, o_ref, s_scr, *, need_ctx):
    n_q = SEQ // ATT_Q_TILE
    chunks = [slice(c * ATT_KEY_CHUNK, (c + 1) * ATT_KEY_CHUNK) for c in range(TOK // ATT_KEY_CHUNK)]
    q_cols = lambda j: slice(CTX_LEN + j * ATT_Q_TILE, CTX_LEN + (j + 1) * ATT_Q_TILE)

    def scores(j, rows, col_max):
        s = _dot(k_ref[0, rows, :], qt_ref[0, :, q_cols(j)])
        s_scr[j % 2, rows, :] = s
        m = jnp.max(s, axis=0, keepdims=True)
        return m if col_max is None else jnp.maximum(col_max, m)

    col_max = None
    for rows in chunks:
        col_max = scores(0, rows, col_max)
    for j in range(n_q):
        acc, next_max = None, None
        for rows in chunks:
            if j + 1 < n_q:
                next_max = scores(j + 1, rows, next_max)
            p = jnp.exp2(s_scr[j % 2, rows, :] - col_max).astype(BF16)
            part = _dot(vt_ref[0, :, rows], p)
            acc = part if acc is None else acc + part
        o_ref[0, q_cols(j), :] = _weighted_values(acc).astype(o_ref.dtype)
        col_max = next_max

    if need_ctx:
        s = _dot(k_ref[0, :CTX_LEN, :], qt_ref[0, :, :CTX_LEN])
        p = jnp.exp2(s - jnp.max(s, axis=0, keepdims=True)).astype(BF16)
        o_ref[0, :CTX_LEN, :] = _weighted_values(_dot(vt_ref[0, :, :CTX_LEN], p)).astype(o_ref.dtype)
    else:
        o_ref[0, :CTX_LEN, :] = jnp.zeros((CTX_LEN, MLA_V), o_ref.dtype)


def _attention(q_t, k, v_t, need_ctx):
    transposed = lambda d: pl.BlockSpec((1, d, TOK), lambda b, h: (h, 0, b))
    return pl.pallas_call(
        functools.partial(_attn_kernel, need_ctx=need_ctx),
        out_shape=jax.ShapeDtypeStruct((MLA_HEADS, NTOK, MLA_V), BF16),
        grid=(BATCH, MLA_HEADS),
        in_specs=[transposed(MLA_QK),
                  pl.BlockSpec((1, TOK, MLA_QK), lambda b, h: (h, b, 0)),
                  transposed(MLA_VT_ROWS)],
        out_specs=pl.BlockSpec((1, TOK, MLA_V), lambda b, h: (h, b, 0)),
        scratch_shapes=[pltpu.VMEM((2, TOK, ATT_Q_TILE), F32)],
        compiler_params=_params(("parallel", "parallel")),
        name="attention",
    )(q_t, k, v_t)


def _even_w_in(w):
    gates0 = GLA_MAIN_W
    gates1 = gates0 + 2 * GLA_GATE_RANK
    w = w.astype(BF16)
    pad = jnp.zeros(w.shape[:-1] + (EV_W - w.shape[-1],), w.dtype)
    return jnp.concatenate([w[..., :gates0], w[..., gates1:], w[..., gates0:gates1], pad], axis=-1)


def _gate_w(wa, first_row):
    out = jnp.zeros((LANE, GLA_QK_W), wa.dtype)
    return lax.dynamic_update_slice(out, wa, (first_row, 0)).astype(BF16)


def _odd_w_in(w):
    w = w.astype(BF16)
    return jnp.concatenate([w, _rot_cols(w[..., OD_KPE0:OD_KPE_ROT0])], axis=-1)


def _wuq_cols(w):
    w = w.reshape(MLA_Q_RANK, MLA_HEADS, MLA_QK)
    nope = w[:, :, :MLA_NOPE].reshape(MLA_Q_RANK, MLA_NOPE_W)
    rope = w[:, :, MLA_NOPE:]
    return jnp.concatenate([nope, rope.reshape(MLA_Q_RANK, MLA_ROPE_W),
                            _rot_cols(rope).reshape(MLA_Q_RANK, MLA_ROPE_W)], axis=1).astype(BF16)


def _wukv_cols(w):
    w = w.reshape(MLA_KV_RANK, MLA_HEADS, MLA_NOPE + MLA_V)
    return jnp.concatenate([w[:, :, :MLA_NOPE].reshape(MLA_KV_RANK, MLA_NOPE_W),
                            w[:, :, MLA_NOPE:].reshape(MLA_KV_RANK, MLA_HEADS * MLA_V)], axis=1).astype(BF16)


def kernel(x, c, ctx, c_ctx, ada_w, ada_b, norm_mix_g, norm_ffn_g, w_mix_out, ffn_w_in, ffn_w_out, ev_w_in, gla_wa_f, gla_ba_f, gla_wa_b, gla_ba_b, gla_onorm_g, sg_vnorm_g, sg_ws, sg_bs, od_w_in, mla_qa_g, mla_wuq, mla_kva_g, mla_wukv, mla_qn_g, mla_kn_g):
    cvec = jnp.zeros((MOD_ROWS, D_MODEL), F32).at[:BATCH].set(c).at[CTX_MOD_ROW].set(c_ctx)
    mod = _ada_table(cvec, ada_w, ada_b)
    rope_tabs = _rope_tables()
    norm_mix_g = norm_mix_g.reshape(DEPTH, 1, D_MODEL)
    norm_ffn_g = norm_ffn_g.reshape(DEPTH, 1, D_MODEL)
    w_in_stacks = (_even_w_in(ev_w_in), _odd_w_in(od_w_in))
    dense_f32 = (w_mix_out, ffn_w_in, ffn_w_out)

    p, xa, dense = _first_inproj(x, ctx, mod, norm_mix_g, w_in_stacks[0], dense_f32)
    for l in range(DEPTH):
        last = l == DEPTH - 1
        i = l // 2
        if l % 2 == 0:
            ma = _gla(p, _gate_w(gla_wa_f[i], 0), gla_ba_f[i].reshape(1, GLA_QK_W),
                      _gate_w(gla_wa_b[i], GLA_GATE_RANK), gla_ba_b[i].reshape(1, GLA_QK_W), gla_onorm_g[i])
            mb = _spatial_gating(p, sg_vnorm_g[i], sg_ws[i].astype(BF16), sg_bs[i])
        else:
            q_t, k, v_t = _mla_prep(p, rope_tabs, mla_qa_g[i], mla_wuq[i], mla_kva_g[i], mla_wukv[i],
                                    mla_qn_g[i], mla_kn_g[i])
            ma = _fourier(p, not last)
            mb = _attention(q_t, k, v_t, not last)
        nxt = None if last else (norm_mix_g, w_in_stacks[(l + 1) % 2], (l + 1) // 2, dense_f32)
        xa, p, dense = _layer_tail(xa, ma, mb, mod, l, norm_ffn_g, dense, nxt)
    return xa.reshape(BATCH, SEQ, D_MODEL)
```

```python
import functools

import numpy as np
import jax
import jax.numpy as jnp
from jax import lax
from jax.experimental import pallas as pl
from jax.experimental.pallas import tpu as pltpu

D_MODEL = 1024
BATCH = 8
SEQ = 2048
DEPTH = 4
CTX_LEN = 256
GRID_W = 64
RMS_EPS = 1e-6
GLA_HEADS = 4
GLA_DK = 64
GLA_DV = 128
GLA_GATE_RANK = 16
GLA_GATE_NORM = 16.0
GLA_CHUNK = 64
SG_GROUPS = 4
SG_DIM = 128
SG_CHUNK = 128
FT_GROUPS = 4
FT_DIM = 64
MLA_HEADS = 6
MLA_NOPE = 128
MLA_ROPE = 64
MLA_V = 128
MLA_Q_RANK = 384
MLA_KV_RANK = 256
ROPE_BASE = 10000.0
FFN_HIDDEN = 2816

TOK = CTX_LEN + SEQ
NTOK = BATCH * TOK
ROW_TILE = 256
TILES_PER_BATCH = TOK // ROW_TILE
CTX_TILES = CTX_LEN // ROW_TILE
MOD_ROWS = 16
CTX_MOD_ROW = BATCH
LANE = 128
EV_W = 2688
OD_W = 1024
MLA_QK = MLA_NOPE + MLA_ROPE
VMEM_LIMIT = 56 * 1024 * 1024

F32 = jnp.float32
BF16 = jnp.bfloat16


def _dot(a, b):
    return jnp.dot(a, b, preferred_element_type=F32)


def _rms(x, g):
    return x * lax.rsqrt(jnp.mean(x * x, axis=-1, keepdims=True) + RMS_EPS) * g


def _lane_rms(x, g):
    width = x.shape[1]
    ss = _dot((x * x).astype(BF16), jnp.ones((width, LANE), BF16))
    inv = lax.rsqrt(ss * (1.0 / width) + RMS_EPS)
    return x * jnp.concatenate([inv] * (width // LANE), axis=1) * g


def _params(sem):
    return pltpu.CompilerParams(dimension_semantics=sem, vmem_limit_bytes=VMEM_LIMIT)


def _resident(shape):
    zeros = (0,) * len(shape)
    return pl.BlockSpec(shape, lambda *_: zeros, pipeline_mode=pl.Buffered(1))


def _mod_row(i):
    return jnp.where(i % TILES_PER_BATCH < CTX_TILES, CTX_MOD_ROW, i // TILES_PER_BATCH)


MOD_PIECES = 6


def _mod_spec(layer, tile_of_step):
    def index_map(i):
        return (layer * MOD_ROWS + _mod_row(tile_of_step(i)), 0, 0)
    return pl.BlockSpec((1, MOD_PIECES, D_MODEL), index_map)


def _ada_kernel(c_ref, w_ref, b_ref, o_ref):
    c = c_ref[...]
    s = c * jax.nn.sigmoid(c)
    w = w_ref[0]
    w_hi = w.astype(BF16)
    w_lo = (w - w_hi.astype(F32)).astype(BF16)
    s_hi = s.astype(BF16)
    s_lo = (s - s_hi.astype(F32)).astype(BF16)
    o_ref[0] = _dot(s_hi, w_hi) + (_dot(s_hi, w_lo) + _dot(s_lo, w_hi)) + b_ref[0]


def _ada_table(cvec, ada_w, ada_b):
    tn = 3072
    n = MOD_PIECES * D_MODEL
    out = pl.pallas_call(
        _ada_kernel,
        out_shape=jax.ShapeDtypeStruct((DEPTH, MOD_ROWS, n), F32),
        grid=(DEPTH, n // tn),
        in_specs=[pl.BlockSpec((MOD_ROWS, D_MODEL), lambda l, j: (0, 0)),
                  pl.BlockSpec((1, D_MODEL, tn), lambda l, j: (l, 0, j)),
                  pl.BlockSpec((1, 1, tn), lambda l, j: (l, 0, j))],
        out_specs=pl.BlockSpec((1, MOD_ROWS, tn), lambda l, j: (l, 0, j)),
        compiler_params=_params(("arbitrary", "arbitrary")),
        name="ada_table",
    )(cvec, ada_w, ada_b.reshape(DEPTH, 1, n))
    return out.reshape(DEPTH * MOD_ROWS, MOD_PIECES, D_MODEL)


def _layer_block(shape, layer):
    zeros = (0,) * len(shape)
    return pl.BlockSpec((1,) + tuple(shape), lambda *_: (layer,) + zeros, pipeline_mode=pl.Buffered(1))


BF16_SUBLANES = 16


def _copy_plumbing(stacks, layer, n_steps):
    views, in_specs, shapes, out_specs = [], [], [], []
    for w in stacks:
        n_layers, n_rows, cols = w.shape
        rows = next(r for r in range(BF16_SUBLANES, n_rows + 1, BF16_SUBLANES)
                    if n_rows % r == 0 and n_rows // r <= n_steps)
        n_slices = n_rows // rows
        views.append(w.reshape(n_layers * n_rows, cols))
        in_specs.append(pl.BlockSpec(
            (rows, cols), lambda i, n=n_slices: (layer * n + jnp.minimum(i, n - 1), 0)))
        shapes.append(jax.ShapeDtypeStruct((n_rows, cols), BF16))
        out_specs.append(pl.BlockSpec((rows, cols), lambda i, n=n_slices: (jnp.minimum(i, n - 1), 0)))
    return views, in_specs, shapes, out_specs


def _as_layer_stacks(copies, stacks):
    return [c.reshape(1, w.shape[1], w.shape[2]) for c, w in zip(copies, stacks)]


def _modulated_norm(x, g, shift, scale):
    return _rms(x, g) * (1.0 + scale) + shift


TAIL_TILES = 2


def _first_inproj_kernel(*refs):
    n = TAIL_TILES
    x_refs, ctx_refs, mod_refs = refs[:n], refs[n:2 * n], refs[2 * n:3 * n]
    g_ref, w_ref, *copy_refs, p_ref, xa_ref, mix_copy, in_copy, out_copy = refs[3 * n:]
    rows = []
    for t in range(n):
        is_ctx = (n * pl.program_id(0) + t) % TILES_PER_BATCH < CTX_TILES
        x = jnp.where(is_ctx, ctx_refs[t][...], x_refs[t][...])
        xa_ref[t * ROW_TILE:(t + 1) * ROW_TILE, :] = x
        mod = mod_refs[t][0]
        rows.append(_modulated_norm(x, g_ref[0], mod[0:1], mod[1:2]).astype(BF16))
    p_ref[...] = _dot(jnp.concatenate(rows, axis=0), w_ref[0]).astype(p_ref.dtype)
    for src, dst in zip(copy_refs, (mix_copy, in_copy, out_copy)):
        dst[...] = src[...].astype(BF16)


def _first_inproj(x, ctx, mod, norm_g, w, dense_f32):
    width = w.shape[-1]
    n_steps = NTOK // (TAIL_TILES * ROW_TILE)
    views, copy_in, copy_shapes, copy_out = _copy_plumbing(dense_f32, 0, n_steps)
    x_tiles = SEQ // ROW_TILE
    tiles = [lambda i, t=t: TAIL_TILES * i + t for t in range(TAIL_TILES)]
    x_tile = lambda j: ((j // TILES_PER_BATCH) * x_tiles + jnp.maximum(j % TILES_PER_BATCH - CTX_TILES, 0), 0)
    ctx_tile = lambda j: ((j // TILES_PER_BATCH) * CTX_TILES + jnp.minimum(j % TILES_PER_BATCH, CTX_TILES - 1), 0)
    source = lambda tile_index: [pl.BlockSpec((ROW_TILE, D_MODEL), lambda i, f=f: tile_index(f(i))) for f in tiles]
    row_out = lambda wd: pl.BlockSpec((TAIL_TILES * ROW_TILE, wd), lambda i: (i, 0))
    per_tile = lambda a: [a] * TAIL_TILES
    p, xa, *copies = pl.pallas_call(
        _first_inproj_kernel,
        out_shape=(jax.ShapeDtypeStruct((NTOK, width), BF16), jax.ShapeDtypeStruct((NTOK, D_MODEL), F32),
                   *copy_shapes),
        grid=(n_steps,),
        in_specs=[*source(x_tile), *source(ctx_tile), *[_mod_spec(0, f) for f in tiles],
                  _layer_block((1, D_MODEL), 0),
                  _layer_block((D_MODEL, width), 0),
                  *copy_in],
        out_specs=(row_out(width), row_out(D_MODEL), *copy_out),
        compiler_params=_params(("arbitrary",)),
        name="first_inproj",
    )(*per_tile(x.reshape(BATCH * SEQ, D_MODEL)), *per_tile(ctx.reshape(BATCH * CTX_LEN, D_MODEL)),
      *per_tile(mod), norm_g, w, *views)
    return p, xa, _as_layer_stacks(copies, dense_f32)


def _tail_kernel(*refs, has_next):
    n = TAIL_TILES
    x_refs, ma_refs, mb_refs, mod_refs = refs[:n], refs[n:2 * n], refs[2 * n:3 * n], refs[3 * n:4 * n]
    g_ref, wmix_ref, win_ref, wout_ref, *rest = refs[4 * n:]
    mods = [r[0] for r in mod_refs]
    stack = lambda tiles: jnp.concatenate(tiles, axis=0)
    unstack = lambda rows: [rows[t * ROW_TILE:(t + 1) * ROW_TILE] for t in range(n)]

    m = stack([jnp.concatenate([a[...], b[...]], axis=1) for a, b in zip(ma_refs, mb_refs)])
    x1 = [x[...] + mod[2:3] * y for x, mod, y in zip(x_refs, mods, unstack(_dot(m, wmix_ref[0])))]
    h = stack([_modulated_norm(x, g_ref[0], mod[3:4], mod[4:5]).astype(BF16) for x, mod in zip(x1, mods)])
    gu = _dot(h, win_ref[0])
    gate = gu[:, :FFN_HIDDEN]
    act = (gate * jax.nn.sigmoid(gate) * gu[:, FFN_HIDDEN:]).astype(BF16)
    x2 = [x + mod[5:6] * d for x, mod, d in zip(x1, mods, unstack(_dot(act, wout_ref[0])))]
    if has_next:
        next_mods = [r[0] for r in rest[:n]]
        gn_ref, wn_ref, *copy_refs, o_ref, p_ref, mix_copy, in_copy, out_copy = rest[n:]
        z = stack([_modulated_norm(x, gn_ref[0], mod[0:1], mod[1:2]).astype(BF16) for x, mod in zip(x2, next_mods)])
        p_ref[...] = _dot(z, wn_ref[0]).astype(p_ref.dtype)
        for src, dst in zip(copy_refs, (mix_copy, in_copy, out_copy)):
            dst[...] = src[...].astype(BF16)
    else:
        (o_ref,) = rest
    o_ref[...] = stack(x2)


def _layer_tail(xa, ma, mb, mod, layer, norm_ffn_g, dense, nxt):
    wmix, win, wout = dense
    if nxt is None:
        x_tiles = SEQ // ROW_TILE
        n_tiles = BATCH * x_tiles
        tile = lambda i: (i // x_tiles) * TILES_PER_BATCH + CTX_TILES + i % x_tiles
        out_rows = BATCH * SEQ
    else:
        n_tiles = NTOK // ROW_TILE
        tile = lambda i: i
        out_rows = NTOK
    n_steps = n_tiles // TAIL_TILES
    tiles = [lambda i, t=t: tile(TAIL_TILES * i + t) for t in range(TAIL_TILES)]
    row_in = lambda width: [pl.BlockSpec((ROW_TILE, width), lambda i, f=f: (f(i), 0)) for f in tiles]
    row_out = lambda width: pl.BlockSpec((TAIL_TILES * ROW_TILE, width), lambda i: (i, 0))
    in_specs = [*row_in(D_MODEL), *row_in(ma.shape[1]), *row_in(mb.shape[1]),
                *[_mod_spec(layer, f) for f in tiles],
                _layer_block((1, D_MODEL), layer),
                _layer_block((D_MODEL, D_MODEL), 0),
                _layer_block((D_MODEL, 2 * FFN_HIDDEN), 0),
                _layer_block((FFN_HIDDEN, D_MODEL), 0)]
    per_tile = lambda a: [a] * TAIL_TILES
    args = [*per_tile(xa), *per_tile(ma), *per_tile(mb), *per_tile(mod), norm_ffn_g, wmix, win, wout]
    out_shape = [jax.ShapeDtypeStruct((out_rows, D_MODEL), F32)]
    out_specs = [row_out(D_MODEL)]
    if nxt is not None:
        norm_mix_g, w_next, idx, dense_f32 = nxt
        width = w_next.shape[-1]
        views, copy_in, copy_shapes, copy_out = _copy_plumbing(dense_f32, layer + 1, n_steps)
        in_specs += [*[_mod_spec(layer + 1, f) for f in tiles],
                     _layer_block((1, D_MODEL), layer + 1), _layer_block((D_MODEL, width), idx)] + copy_in
        args += [*per_tile(mod), norm_mix_g, w_next] + views
        out_shape += [jax.ShapeDtypeStruct((out_rows, width), BF16)] + copy_shapes
        out_specs += [row_out(width)] + copy_out
    res = pl.pallas_call(
        functools.partial(_tail_kernel, has_next=nxt is not None),
        out_shape=tuple(out_shape),
        grid=(n_steps,),
        in_specs=in_specs,
        out_specs=tuple(out_specs),
        compiler_params=_params(("arbitrary",)),
        name="layer_tail",
    )(*args)
    if nxt is None:
        return res[0], None, None
    return res[0], res[1], _as_layer_stacks(res[2:], dense_f32)


GLA_QK_W = GLA_HEADS * GLA_DK
GLA_V_W = GLA_HEADS * GLA_DV
GLA_MAIN_W = 2 * GLA_QK_W + 2 * GLA_V_W
GLA_NCHUNK = TOK // GLA_CHUNK


GLA_BLOCK = 256
GLA_NBLOCK = TOK // GLA_BLOCK
GLA_CTX_BLOCKS = CTX_LEN // GLA_BLOCK
GLA_UNROLL = GLA_NBLOCK


LOG2_E = 1.4426950408889634


def _log2_sigmoid(y, scale):
    return jnp.minimum(y, 0.0) * (LOG2_E * scale) - jnp.log2(1.0 + jnp.exp2(jnp.abs(y) * (-LOG2_E))) * scale


def _gla_kernel(main_ref, gate_ref, waf_ref, baf_ref, wab_ref, bab_ref, og_ref, o_ref,
                of_scr, qd_scr, kd_scr, qe_scr, kl_scr, dec_scr, s_scr):
    C, BLK = GLA_CHUNK, GLA_BLOCK
    cpb = BLK // C
    rb = lax.broadcasted_iota(jnp.int32, (BLK, BLK), 0)
    cb = lax.broadcasted_iota(jnp.int32, (BLK, BLK), 1)
    same_chunk = (rb // C) == (cb // C)
    tri_f = jnp.where(same_chunk & (rb >= cb), 1.0, 0.0).astype(BF16)
    tri_b = jnp.where(same_chunk & (rb <= cb), 1.0, 0.0).astype(BF16)
    r = lax.broadcasted_iota(jnp.int32, (C, C), 0)
    cc = lax.broadcasted_iota(jnp.int32, (C, C), 1)
    lower, upper = r >= cc, r <= cc
    sr = lax.broadcasted_iota(jnp.int32, (BLK, cpb * LANE), 0)
    sc = lax.broadcasted_iota(jnp.int32, (BLK, cpb * LANE), 1)
    chunk_sum = jnp.where(sr // C == sc // LANE, 1.0, 0.0).astype(BF16)

    def split(x):
        hi = x.astype(BF16)
        return hi, (x - hi.astype(F32)).astype(BF16)

    def dot_01(m, x):
        hi, lo = split(x)
        return _dot(jnp.concatenate([m, m], axis=1), jnp.concatenate([hi, lo], axis=0))

    def dot_10(x, m):
        hi, lo = split(x)
        return _dot(jnp.concatenate([hi, lo], axis=1), jnp.concatenate([m, m], axis=0))

    def block_rows(blk):
        return pl.ds(pl.multiple_of(blk * BLK, BLK), BLK)

    def chunk_rows(blk, j):
        return pl.ds(pl.multiple_of(blk * BLK + j * C, C), C)

    def dec_rows(blk, j):
        return pl.ds(pl.multiple_of((blk * cpb + j) * GLA_QK_W, GLA_QK_W), GLA_QK_W)

    def decay_terms(a, q, k, w_ref, bias_ref, tri, end_row):
        g = _log2_sigmoid(_dot(a, w_ref[...]) + bias_ref[...], 1.0 / GLA_GATE_NORM)
        b = dot_01(tri, g)
        dec_all = jnp.exp2(dot_10(g.T, chunk_sum))
        terms = []
        for j in range(cpb):
            sl = slice(j * C, (j + 1) * C)
            bc = b[sl]
            b_mid = bc[C // 2:C // 2 + 1]
            b_end = bc[end_row:end_row + 1]
            d = bc - b_mid
            qd = q[sl] * jnp.exp2(d)
            kd = k[sl] * jnp.exp2(-d)
            qe = qd * jnp.exp2(b_mid)
            kl = kd * jnp.exp2(b_end - b_mid)
            dec = dec_all[:, j * LANE:(j + 1) * LANE]
            terms.append((qd.astype(BF16), kd.T.astype(BF16), qe.astype(BF16), kl.T.astype(BF16), dec))
        return terms

    def step(qd, kd_t, qe, kl_t, dec, v, keep):
        outs = []
        for h in range(GLA_HEADS):
            ks = slice(h * GLA_DK, (h + 1) * GLA_DK)
            vh = v[:, h * GLA_DV:(h + 1) * GLA_DV]
            s_h = s_scr[ks, :]
            att = jnp.where(keep, _dot(qd[:, ks], kd_t[ks, :]), 0.0).astype(BF16)
            outs.append(_dot(att, vh) + _dot(qe[:, ks], s_h.astype(BF16)))
            s_scr[ks, :] = s_h * dec[ks, :] + _dot(kl_t[ks, :], vh)
        return outs

    s_scr[...] = jnp.zeros(s_scr.shape, F32)

    def loop1(blk, carry):
        rows = block_rows(blk)
        a = gate_ref[rows, :]
        q = main_ref[rows, 0:GLA_QK_W].astype(F32) * (GLA_DK ** -0.5)
        k = main_ref[rows, GLA_QK_W:2 * GLA_QK_W].astype(F32)
        v = main_ref[rows, 2 * GLA_QK_W:2 * GLA_QK_W + GLA_V_W]
        fwd = decay_terms(a, q, k, waf_ref, baf_ref, tri_f, C - 1)
        bwd = decay_terms(a, q, k, wab_ref, bab_ref, tri_b, 0)
        for j in range(cpb):
            cr = chunk_rows(blk, j)
            qd, kd, qe, kl, dec = bwd[j]
            qd_scr[cr, :] = qd
            kd_scr[dec_rows(blk, j), :] = kd
            qe_scr[cr, :] = qe
            kl_scr[dec_rows(blk, j), :] = kl
            dec_scr[dec_rows(blk, j), :] = dec
            qd, kd, qe, kl, dec = fwd[j]
            outs = step(qd, kd, qe, kl, dec, v[j * C:(j + 1) * C], lower)
            of_scr[cr, :] = jnp.concatenate(outs, axis=1)
        return carry

    lax.fori_loop(0, GLA_NBLOCK, loop1, 0, unroll=GLA_UNROLL)

    s_scr[...] = jnp.zeros(s_scr.shape, F32)

    def loop2(i, carry):
        blk = jnp.where(i < GLA_CTX_BLOCKS, GLA_CTX_BLOCKS - 1 - i, GLA_NBLOCK - 1 - (i - GLA_CTX_BLOCKS))
        for j in reversed(range(cpb)):
            cr = chunk_rows(blk, j)
            v = main_ref[cr, 2 * GLA_QK_W:2 * GLA_QK_W + GLA_V_W]
            kr = dec_rows(blk, j)
            outs = step(qd_scr[cr, :], kd_scr[kr, :], qe_scr[cr, :], kl_scr[kr, :], dec_scr[kr, :], v, upper)
            gout = main_ref[cr, 2 * GLA_QK_W + GLA_V_W:GLA_MAIN_W].astype(F32)
            res = []
            for h in range(GLA_HEADS):
                vs = slice(h * GLA_DV, (h + 1) * GLA_DV)
                o = outs[h] + of_scr[cr, vs]
                gh = gout[:, vs]
                res.append(_rms(o, og_ref[...]) * (gh * jax.nn.sigmoid(gh)))
            o_ref[cr, :] = jnp.concatenate(res, axis=1).astype(o_ref.dtype)
        return carry

    lax.fori_loop(0, GLA_NBLOCK, loop2, 0, unroll=GLA_UNROLL)


def _gla(p, waf, baf, wab, bab, onorm_g):
    gate_col = (EV_W - LANE) // LANE
    qk_scr = pltpu.VMEM((TOK, GLA_QK_W), BF16)
    qk_t_scr = pltpu.VMEM((GLA_NCHUNK * GLA_QK_W, GLA_CHUNK), BF16)
    return pl.pallas_call(
        _gla_kernel,
        out_shape=jax.ShapeDtypeStruct((NTOK, GLA_V_W), BF16),
        grid=(BATCH,),
        in_specs=[pl.BlockSpec((TOK, GLA_MAIN_W), lambda b: (b, 0)),
                  pl.BlockSpec((TOK, LANE), lambda b: (b, gate_col)),
                  _resident((LANE, GLA_QK_W)), _resident((1, GLA_QK_W)),
                  _resident((LANE, GLA_QK_W)), _resident((1, GLA_QK_W)),
                  _resident((1, GLA_DV))],
        out_specs=pl.BlockSpec((TOK, GLA_V_W), lambda b: (b, 0)),
        scratch_shapes=[pltpu.VMEM((TOK, GLA_V_W), F32), qk_scr, qk_t_scr, qk_scr, qk_t_scr,
                        pltpu.VMEM((GLA_NCHUNK * GLA_QK_W, LANE), F32),
                        pltpu.VMEM((GLA_QK_W, GLA_DV), F32)],
        compiler_params=_params(("parallel",)),
        name="gla",
    )(p, p, waf, baf, wab, bab, onorm_g.reshape(1, GLA_DV))


def _gelu(x):
    return 0.5 * x * (1.0 + lax.erf(x * (2.0 ** -0.5)))


SG_TILE = 1152


def _sg_kernel(u_ref, v_ref, vg_ref, ws_ref, bs_ref, o_ref):
    for c in range(SG_TILE // SG_CHUNK):
        rows = slice(c * SG_CHUNK, (c + 1) * SG_CHUNK)
        for g in range(SG_GROUPS):
            cols = slice(g * SG_DIM, (g + 1) * SG_DIM)
            u = _gelu(u_ref[rows, cols].astype(F32))
            vn = _lane_rms(_gelu(v_ref[rows, cols].astype(F32)), vg_ref[g:g + 1, :])
            mixed = _dot(ws_ref[g], vn.astype(BF16)) + bs_ref[g]
            o_ref[rows, cols] = (u * mixed).astype(o_ref.dtype)


def _spatial_gating(p, vnorm_g, ws, bs):
    width = SG_GROUPS * SG_DIM
    u_col = GLA_MAIN_W // width
    bias = jnp.broadcast_to(bs[:, :, None], (SG_GROUPS, SG_CHUNK, SG_DIM))
    return pl.pallas_call(
        _sg_kernel,
        out_shape=jax.ShapeDtypeStruct((NTOK, width), BF16),
        grid=(NTOK // SG_TILE,),
        in_specs=[pl.BlockSpec((SG_TILE, width), lambda i: (i, u_col)),
                  pl.BlockSpec((SG_TILE, width), lambda i: (i, u_col + 1)),
                  _resident((SG_GROUPS, SG_DIM)),
                  _resident((SG_GROUPS, SG_CHUNK, SG_CHUNK)),
                  _resident((SG_GROUPS, SG_CHUNK, SG_DIM))],
        out_specs=pl.BlockSpec((SG_TILE, width), lambda i: (i, 0)),
        compiler_params=_params(("parallel",)),
        name="spatial_gating",
    )(p, p, vnorm_g, ws, bias)


FT_W = FT_GROUPS * FT_DIM


def _dft_cos_sin(n):
    jk = np.outer(np.arange(n), np.arange(n)) % n
    ang = 2.0 * np.pi * jk.astype(np.float64) / n
    return np.cos(ang), np.sin(ang)


def _parity_dft(n):
    p = np.arange(n // 2)[:, None]
    mats = []
    for first in (0, 1):
        t = 2 * np.arange(n // 2)[None, :] + first
        ang = 2.0 * np.pi * ((p * t) % n).astype(np.float64) / n
        mats.append(np.concatenate([np.cos(ang), -np.sin(ang)], axis=1))
    return jnp.asarray(np.stack(mats), F32).astype(BF16)


def _fourier_kernel(h_ref, cs_ref, dft_x_ref, dft_c_ref, o_ref, ab_scr, *, need_ctx):
    ab = _dot(h_ref[...], cs_ref[...])
    n_slab = 2 * FT_W // LANE
    for c in range(n_slab):
        ab_scr[c] = ab[:, c * LANE:(c + 1) * LANE]

    def position_dft(first_row, length, dft_ref):
        half = length // 2
        parts = []
        for parity in (0, 1):
            slabs = [ab_scr[c, pl.ds(first_row + parity, half, stride=2), :].astype(BF16) for c in range(n_slab)]
            a = jnp.concatenate(slabs[:n_slab // 2], axis=1)
            b = jnp.concatenate(slabs[n_slab // 2:], axis=1)
            parts.append(_dot(dft_ref[parity], jnp.concatenate([a, b], axis=0)))
        scale = (length * FT_DIM) ** -0.5
        o_ref[first_row:first_row + half, :] = ((parts[0] + parts[1]) * scale).astype(o_ref.dtype)
        o_ref[first_row + half:first_row + length, :] = ((parts[0] - parts[1]) * scale).astype(o_ref.dtype)

    position_dft(CTX_LEN, SEQ, dft_x_ref)
    if need_ctx:
        position_dft(0, CTX_LEN, dft_c_ref)
    else:
        o_ref[:CTX_LEN, :] = jnp.zeros((CTX_LEN, FT_W), o_ref.dtype)


def _fourier(p, need_ctx):
    cc, sc = _dft_cos_sin(FT_DIM)
    eye = np.eye(FT_GROUPS)
    cs = jnp.asarray(np.concatenate([np.kron(eye, cc), np.kron(eye, sc)], axis=1), F32).astype(BF16)
    return pl.pallas_call(
        functools.partial(_fourier_kernel, need_ctx=need_ctx),
        out_shape=jax.ShapeDtypeStruct((NTOK, FT_W), BF16),
        grid=(BATCH,),
        in_specs=[pl.BlockSpec((TOK, FT_W), lambda b: (b, 0)),
                  _resident((FT_W, 2 * FT_W)),
                  _resident((2, SEQ // 2, SEQ)), _resident((2, CTX_LEN // 2, CTX_LEN))],
        out_specs=pl.BlockSpec((TOK, FT_W), lambda b: (b, 0)),
        scratch_shapes=[pltpu.VMEM((2 * FT_W // LANE, TOK, LANE), F32)],
        compiler_params=_params(("parallel",)),
        name="fourier",
    )(p, cs, _parity_dft(SEQ), _parity_dft(CTX_LEN))


MLA_NOPE_W = MLA_HEADS * MLA_NOPE
MLA_ROPE_W = MLA_HEADS * MLA_ROPE
OD_QA0 = FT_W
OD_KVA0 = OD_QA0 + MLA_Q_RANK
OD_KPE0 = OD_KVA0 + MLA_KV_RANK
OD_KPE_ROT0 = OD_KPE0 + MLA_ROPE


def _rot_cols(w):
    q = MLA_ROPE // 4
    return jnp.concatenate([-w[..., q:2 * q], w[..., 0:q], -w[..., 3 * q:4 * q], w[..., 2 * q:3 * q]], axis=-1)


def _rot_gain(g):
    q = MLA_ROPE // 4
    return jnp.concatenate([g[..., q:2 * q], g[..., 0:q], g[..., 3 * q:4 * q], g[..., 2 * q:3 * q]], axis=-1)


def _rope_tables():
    rows = SEQ // GRID_W
    row_id = np.repeat(np.arange(rows, dtype=np.float32), GRID_W)
    col_id = np.tile(np.arange(GRID_W, dtype=np.float32), rows)
    axis_dim = MLA_ROPE // 2
    inv_freq = (np.float32(ROPE_BASE) ** (-np.arange(0, axis_dim, 2, dtype=np.float32) / np.float32(axis_dim))).astype(np.float32)
    ang_r = row_id[:, None] * inv_freq
    ang_c = col_id[:, None] * inv_freq
    ang = np.concatenate([ang_r, ang_r, ang_c, ang_c], axis=-1).astype(np.float32)
    cos = np.concatenate([np.ones((CTX_LEN, MLA_ROPE), np.float32), np.cos(ang)], axis=0)
    sin = np.concatenate([np.zeros((CTX_LEN, MLA_ROPE), np.float32), np.sin(ang)], axis=0)
    return tuple(jnp.asarray(t, F32) for t in (cos, sin, cos.T, sin.T))


MLA_VT_ROWS = MLA_V + 16


def _mla_heads(p, cos, sin, cos_t, sin_t, qag, wuq_t, kvag, wukv_k, wukv_vt, head_sum, kg,
               qg_nope, qg_rope, qg_rot, qt_ref, k_ref, vt_ref):
    qa = _lane_rms(p[:, OD_QA0:OD_KVA0].astype(F32), qag)
    q_t = _dot(wuq_t, qa.T.astype(BF16))
    for h in range(MLA_HEADS):
        qn = q_t[h * MLA_NOPE:(h + 1) * MLA_NOPE]
        qr = q_t[MLA_NOPE_W + h * MLA_ROPE:MLA_NOPE_W + (h + 1) * MLA_ROPE]
        qrr = q_t[MLA_NOPE_W + MLA_ROPE_W + h * MLA_ROPE:MLA_NOPE_W + MLA_ROPE_W + (h + 1) * MLA_ROPE]
        ss = jnp.sum(qn * qn, axis=0, keepdims=True) + jnp.sum(qr * qr, axis=0, keepdims=True)
        inv = lax.rsqrt(ss * (1.0 / MLA_QK) + RMS_EPS) * (MLA_QK ** -0.5 * LOG2_E)
        q_rope = qr * qg_rope * cos_t + qrr * qg_rot * sin_t
        qt_ref[h] = jnp.concatenate([qn * qg_nope * inv, q_rope * inv], axis=0).astype(qt_ref.dtype)

    kva = _lane_rms(p[:, OD_KVA0:OD_KPE0].astype(F32), kvag)
    v_t = _dot(wukv_vt, kva.T.astype(BF16))
    ones = jnp.ones((MLA_VT_ROWS - MLA_V, v_t.shape[1]), F32)
    for h in range(MLA_HEADS):
        vt_ref[h] = jnp.concatenate([v_t[h * MLA_V:(h + 1) * MLA_V], ones], axis=0).astype(vt_ref.dtype)
    kn = _dot(kva.astype(BF16), wukv_k)
    kpe = p[:, OD_KPE0:OD_KPE_ROT0].astype(F32)
    kpe_rot = p[:, OD_KPE_ROT0:OD_W].astype(F32)
    sq = jnp.concatenate([kn * kn, kpe * kpe], axis=1).astype(BF16)
    ss = _dot(sq, head_sum)
    inv = lax.rsqrt(ss * (1.0 / MLA_QK) + RMS_EPS)
    k_rope = kpe * kg[:, MLA_NOPE:MLA_QK] * cos + kpe_rot * kg[:, MLA_QK:] * sin
    for h in range(MLA_HEADS):
        k_h = jnp.concatenate([kn[:, h * MLA_NOPE:(h + 1) * MLA_NOPE] * kg[:, :MLA_NOPE], k_rope], axis=1)
        k_ref[h] = (k_h * inv[:, h:h + 1]).astype(k_ref.dtype)


def _key_head_sum_matrix():
    head = np.arange(LANE)[None, :]
    nope = np.repeat(np.arange(MLA_HEADS), MLA_NOPE)[:, None] == head
    rope = np.broadcast_to(head < MLA_HEADS, (MLA_ROPE, LANE))
    return jnp.asarray(np.concatenate([nope, rope], axis=0), BF16)


MLA_TILE = 768


def _mla_prep_kernel(p_ref, *refs):
    *operand_refs, qt_ref, k_ref, vt_ref = refs
    _mla_heads(p_ref[...], *(r[...] for r in operand_refs), qt_ref, k_ref, vt_ref)


def _mla_prep(p, rope_tabs, qa_g, wuq, kva_g, wukv, qn_g, kn_g):
    wukv = _wukv_cols(wukv)
    kgain = jnp.concatenate([kn_g, _rot_gain(kn_g[MLA_NOPE:])]).reshape(1, -1)
    col = lambda g: jnp.broadcast_to(g[:, None], (g.shape[0], MLA_TILE))
    operands = [*rope_tabs, qa_g.reshape(1, -1), _wuq_cols(wuq).T, kva_g.reshape(1, -1),
                wukv[:, :MLA_NOPE_W], wukv[:, MLA_NOPE_W:].T, _key_head_sum_matrix(), kgain,
                col(qn_g[:MLA_NOPE]), col(qn_g[MLA_NOPE:]), col(_rot_gain(qn_g[MLA_NOPE:]))]
    tiles_per_batch = TOK // MLA_TILE
    rope = pl.BlockSpec((MLA_TILE, MLA_ROPE), lambda i: (i % tiles_per_batch, 0))
    rope_t = pl.BlockSpec((MLA_ROPE, MLA_TILE), lambda i: (0, i % tiles_per_batch))
    rows = lambda d: pl.BlockSpec((MLA_HEADS, MLA_TILE, d), lambda i: (0, i, 0))
    cols = lambda d: pl.BlockSpec((MLA_HEADS, d, MLA_TILE), lambda i: (0, 0, i))
    return pl.pallas_call(
        _mla_prep_kernel,
        out_shape=(jax.ShapeDtypeStruct((MLA_HEADS, MLA_QK, NTOK), BF16),
                   jax.ShapeDtypeStruct((MLA_HEADS, NTOK, MLA_QK), BF16),
                   jax.ShapeDtypeStruct((MLA_HEADS, MLA_VT_ROWS, NTOK), BF16)),
        grid=(NTOK // MLA_TILE,),
        in_specs=[pl.BlockSpec((MLA_TILE, OD_W), lambda i: (i, 0)), rope, rope, rope_t, rope_t]
                 + [_resident(a.shape) for a in operands[4:]],
        out_specs=(cols(MLA_QK), rows(MLA_QK), cols(MLA_VT_ROWS)),
        compiler_params=_params(("parallel",)),
        name="mla_prep",
    )(p, *operands)


ATT_Q_TILE = 512


ATT_KEY_CHUNK = TOK


def _weighted_values(acc):
    return (acc[:MLA_V] * (1.0 / acc[MLA_V:MLA_V + 1])).T


def _attn_kernel(qt_ref, k_ref, vt_ref, o_ref, s_scr, *, need_ctx):
    n_q = SEQ // ATT_Q_TILE
    chunks = [slice(c * ATT_KEY_CHUNK, (c + 1) * ATT_KEY_CHUNK) for c in range(TOK // ATT_KEY_CHUNK)]
    q_cols = lambda j: slice(CTX_LEN + j * ATT_Q_TILE, CTX_LEN + (j + 1) * ATT_Q_TILE)

    for hh in range(ATT_STEP_HEADS):
        out = slice(hh * MLA_V, (hh + 1) * MLA_V)

        def scores(j, rows, col_max):
            s = _dot(k_ref[hh, rows, :], qt_ref[hh, :, q_cols(j)])
            s_scr[j % 2, rows, :] = s
            m = jnp.max(s, axis=0, keepdims=True)
            return m if col_max is None else jnp.maximum(col_max, m)

        col_max = None
        for rows in chunks:
            col_max = scores(0, rows, col_max)
        for j in range(n_q):
            acc, next_max = None, None
            for rows in chunks:
                if j + 1 < n_q:
                    next_max = scores(j + 1, rows, next_max)
                p = jnp.exp2(s_scr[j % 2, rows, :] - col_max).astype(BF16)
                part = _dot(vt_ref[hh, :, rows], p)
                acc = part if acc is None else acc + part
            o_ref[q_cols(j), out] = _weighted_values(acc).astype(o_ref.dtype)
            col_max = next_max

        if need_ctx:
            s = _dot(k_ref[hh, :CTX_LEN, :], qt_ref[hh, :, :CTX_LEN])
            p = jnp.exp2(s - jnp.max(s, axis=0, keepdims=True)).astype(BF16)
            o_ref[:CTX_LEN, out] = _weighted_values(_dot(vt_ref[hh, :, :CTX_LEN], p)).astype(o_ref.dtype)
        else:
            o_ref[:CTX_LEN, out] = jnp.zeros((CTX_LEN, MLA_V), o_ref.dtype)


ATT_STEP_HEADS = 2


def _attention(q_t, k, v_t, need_ctx):
    transposed = lambda d: pl.BlockSpec((ATT_STEP_HEADS, d, TOK), lambda b, h: (h, 0, b))
    return pl.pallas_call(
        functools.partial(_attn_kernel, need_ctx=need_ctx),
        out_shape=jax.ShapeDtypeStruct((NTOK, MLA_HEADS * MLA_V), BF16),
        grid=(BATCH, MLA_HEADS // ATT_STEP_HEADS),
        in_specs=[transposed(MLA_QK),
                  pl.BlockSpec((ATT_STEP_HEADS, TOK, MLA_QK), lambda b, h: (h, b, 0)),
                  transposed(MLA_VT_ROWS)],
        out_specs=pl.BlockSpec((TOK, ATT_STEP_HEADS * MLA_V), lambda b, h: (b, h)),
        scratch_shapes=[pltpu.VMEM((2, TOK, ATT_Q_TILE), F32)],
        compiler_params=_params(("parallel", "parallel")),
        name="attention",
    )(q_t, k, v_t)


def _even_w_in(w):
    gates0 = GLA_MAIN_W
    gates1 = gates0 + 2 * GLA_GATE_RANK
    w = w.astype(BF16)
    pad = jnp.zeros(w.shape[:-1] + (EV_W - w.shape[-1],), w.dtype)
    return jnp.concatenate([w[..., :gates0], w[..., gates1:], w[..., gates0:gates1], pad], axis=-1)


def _gate_w(wa, first_row):
    out = jnp.zeros((LANE, GLA_QK_W), wa.dtype)
    return lax.dynamic_update_slice(out, wa, (first_row, 0)).astype(BF16)


def _odd_w_in(w):
    w = w.astype(BF16)
    return jnp.concatenate([w, _rot_cols(w[..., OD_KPE0:OD_KPE_ROT0])], axis=-1)


def _wuq_cols(w):
    w = w.reshape(MLA_Q_RANK, MLA_HEADS, MLA_QK)
    nope = w[:, :, :MLA_NOPE].reshape(MLA_Q_RANK, MLA_NOPE_W)
    rope = w[:, :, MLA_NOPE:]
    return jnp.concatenate([nope, rope.reshape(MLA_Q_RANK, MLA_ROPE_W),
                            _rot_cols(rope).reshape(MLA_Q_RANK, MLA_ROPE_W)], axis=1).astype(BF16)


def _wukv_cols(w):
    w = w.reshape(MLA_KV_RANK, MLA_HEADS, MLA_NOPE + MLA_V)
    return jnp.concatenate([w[:, :, :MLA_NOPE].reshape(MLA_KV_RANK, MLA_NOPE_W),
                            w[:, :, MLA_NOPE:].reshape(MLA_KV_RANK, MLA_HEADS * MLA_V)], axis=1).astype(BF16)


def kernel(x, c, ctx, c_ctx, ada_w, ada_b, norm_mix_g, norm_ffn_g, w_mix_out, ffn_w_in, ffn_w_out, ev_w_in, gla_wa_f, gla_ba_f, gla_wa_b, gla_ba_b, gla_onorm_g, sg_vnorm_g, sg_ws, sg_bs, od_w_in, mla_qa_g, mla_wuq, mla_kva_g, mla_wukv, mla_qn_g, mla_kn_g):
    cvec = jnp.zeros((MOD_ROWS, D_MODEL), F32).at[:BATCH].set(c).at[CTX_MOD_ROW].set(c_ctx)
    mod = _ada_table(cvec, ada_w, ada_b)
    rope_tabs = _rope_tables()
    norm_mix_g = norm_mix_g.reshape(DEPTH, 1, D_MODEL)
    norm_ffn_g = norm_ffn_g.reshape(DEPTH, 1, D_MODEL)
    w_in_stacks = (_even_w_in(ev_w_in), _odd_w_in(od_w_in))
    dense_f32 = (w_mix_out, ffn_w_in, ffn_w_out)

    p, xa, dense = _first_inproj(x, ctx, mod, norm_mix_g, w_in_stacks[0], dense_f32)
    for l in range(DEPTH):
        last = l == DEPTH - 1
        i = l // 2
        if l % 2 == 0:
            ma = _gla(p, _gate_w(gla_wa_f[i], 0), gla_ba_f[i].reshape(1, GLA_QK_W),
                      _gate_w(gla_wa_b[i], GLA_GATE_RANK), gla_ba_b[i].reshape(1, GLA_QK_W), gla_onorm_g[i])
            mb = _spatial_gating(p, sg_vnorm_g[i], sg_ws[i].astype(BF16), sg_bs[i])
        else:
            q_t, k, v_t = _mla_prep(p, rope_tabs, mla_qa_g[i], mla_wuq[i], mla_kva_g[i], mla_wukv[i],
                                    mla_qn_g[i], mla_kn_g[i])
            ma = _fourier(p, not last)
            mb = _attention(q_t, k, v_t, not last)
        nxt = None if last else (norm_mix_g, w_in_stacks[(l + 1) % 2], (l + 1) // 2, dense_f32)
        xa, p, dense = _layer_tail(xa, ma, mb, mod, l, norm_ffn_g, dense, nxt)
    return xa.reshape(BATCH, SEQ, D_MODEL)
```
